```python
import math
import jax, jax.numpy as jnp
from jax import lax
import numpy as np

D_MODEL = 1024
BATCH = 8
SEQ = 2048
DEPTH = 1
DEC_BATCH = 4
DEC_SEQ = 4096
PAST_LEN = 128

D_MIX = D_MODEL
MLA_HEADS = 8
MLA_NOPE = 64
MLA_ROPE = 32
MLA_V = 64
Q_RANK = 256
KV_RANK = 128
DIFF_HEADS = 4
DIFF_HEAD_DIM = 64
DIFF_ROT = DIFF_HEAD_DIM // 4
ROPE_THETA = 500000.0
Q_BLOCK = 128
N_GROUPS = 4
EXPERTS_PER_GROUP = 8
N_EXPERTS = N_GROUPS * EXPERTS_PER_GROUP
TOP_K_IN_GROUP = 2
EXPERT_FF = 256
EPS = 1e-6

D_DIFF_QK = DIFF_HEADS * 2 * DIFF_HEAD_DIM
D_DIFF_V = DIFF_HEADS * 2 * DIFF_HEAD_DIM
SPLIT_POINTS = (Q_RANK, Q_RANK + KV_RANK, Q_RANK + KV_RANK + MLA_ROPE,
                Q_RANK + KV_RANK + MLA_ROPE + D_DIFF_QK,
                Q_RANK + KV_RANK + MLA_ROPE + 2 * D_DIFF_QK)
D_IN = Q_RANK + KV_RANK + MLA_ROPE + 2 * D_DIFF_QK + D_DIFF_V

kernel_name = "hybrid_mla_diffattn_hmoe_encoder"


def _rmsnorm(x, g):
    xf = x.astype(jnp.float32)
    y = xf * lax.rsqrt(jnp.mean(xf * xf, axis=-1, keepdims=True) + EPS)
    return (y * g.astype(jnp.float32)).astype(x.dtype)


def _rope(x, rot_dim):
    S = x.shape[-2]
    half = rot_dim // 2
    inv = ROPE_THETA ** (-jnp.arange(half, dtype=jnp.float32) / half)
    ang = jnp.arange(S, dtype=jnp.float32)[:, None] * inv[None, :]
    cos, sin = jnp.cos(ang), jnp.sin(ang)
    xr = x[..., :rot_dim].astype(jnp.float32)
    x1, x2 = xr[..., :half], xr[..., half:]
    rot = jnp.concatenate([x1 * cos - x2 * sin, x2 * cos + x1 * sin], axis=-1).astype(x.dtype)
    return jnp.concatenate([rot, x[..., rot_dim:]], axis=-1)


def _blocked_attention(q, k, v, scale):
    B, H, S, dk = q.shape
    dv = v.shape[-1]
    nb = S // Q_BLOCK
    qb = q.reshape(B, H, nb, Q_BLOCK, dk).transpose(2, 0, 1, 3, 4)

    def one(qi):
        s = jnp.einsum('bhqd,bhkd->bhqk', qi, k).astype(jnp.float32) * scale
        p = jax.nn.softmax(s, axis=-1)
        return jnp.einsum('bhqk,bhkd->bhqd', p.astype(v.dtype), v)

    o = lax.map(one, qb)
    return o.transpose(1, 2, 0, 3, 4).reshape(B, H, S, dv)


def _blocked_diff_attention(q, k, v, lam, scale):
    B, H, _, S, d = q.shape
    dv = v.shape[-1]
    nb = S // Q_BLOCK
    qb = q.reshape(B, H, 2, nb, Q_BLOCK, d).transpose(3, 0, 1, 2, 4, 5)

    def one(qi):
        s = jnp.einsum('bhcqd,bhckd->bhcqk', qi, k).astype(jnp.float32) * scale
        p = jax.nn.softmax(s, axis=-1)
        a = p[:, :, 0] - lam * p[:, :, 1]
        return jnp.einsum('bhqk,bhkd->bhqd', a.astype(v.dtype), v)

    o = lax.map(one, qb)
    return o.transpose(1, 2, 0, 3, 4).reshape(B, H, S, dv)


def _mla(c_q, c_kv, k_pe, q_norm, w_uq, kv_norm, w_ukv):
    B, S, _ = c_q.shape
    q = (_rmsnorm(c_q, q_norm) @ w_uq).reshape(B, S, MLA_HEADS, MLA_NOPE + MLA_ROPE).transpose(0, 2, 1, 3)
    q = jnp.concatenate([q[..., :MLA_NOPE], _rope(q[..., MLA_NOPE:], MLA_ROPE)], axis=-1)
    kv = (_rmsnorm(c_kv, kv_norm) @ w_ukv).reshape(B, S, MLA_HEADS, MLA_NOPE + MLA_V).transpose(0, 2, 1, 3)
    k_nope, v = kv[..., :MLA_NOPE], kv[..., MLA_NOPE:]
    k_rot = _rope(k_pe[:, None], MLA_ROPE)
    k = jnp.concatenate([k_nope, jnp.broadcast_to(k_rot, (B, MLA_HEADS, S, MLA_ROPE))], axis=-1)
    o = _blocked_attention(q, k, v, (MLA_NOPE + MLA_ROPE) ** -0.5)
    return o.transpose(0, 2, 1, 3).reshape(B, S, MLA_HEADS * MLA_V)


def _diff_attention(dq, dk, dv, lq1, lk1, lq2, lk2, subln, lambda_init):
    B, S, _ = dq.shape
    q = _rope(dq.reshape(B, S, DIFF_HEADS, 2, DIFF_HEAD_DIM).transpose(0, 2, 3, 1, 4), DIFF_ROT)
    k = _rope(dk.reshape(B, S, DIFF_HEADS, 2, DIFF_HEAD_DIM).transpose(0, 2, 3, 1, 4), DIFF_ROT)
    v = dv.reshape(B, S, DIFF_HEADS, 2 * DIFF_HEAD_DIM).transpose(0, 2, 1, 3)
    lam = (jnp.exp(jnp.sum(lq1.astype(jnp.float32) * lk1.astype(jnp.float32)))
           - jnp.exp(jnp.sum(lq2.astype(jnp.float32) * lk2.astype(jnp.float32))) + lambda_init)
    o = _blocked_diff_attention(q, k, v, lam, DIFF_HEAD_DIM ** -0.5)
    o = _rmsnorm(o, subln) * (1.0 - lambda_init)
    return o.transpose(0, 2, 1, 3).reshape(B, S, DIFF_HEADS * 2 * DIFF_HEAD_DIM)


def _hier_moe(h, router_group, router_group_bias, router_expert, router_expert_bias, w_gate, w_up, w_down):
    B, S, D = h.shape
    t = h.reshape(B * S, D)
    gl = (t @ router_group).astype(jnp.float32) + router_group_bias.astype(jnp.float32)
    gp = jax.nn.softmax(gl, axis=-1)
    p_g, g_idx = lax.top_k(gp, 1)
    el = ((t @ router_expert).astype(jnp.float32)
          + router_expert_bias.astype(jnp.float32)).reshape(-1, N_GROUPS, EXPERTS_PER_GROUP)
    within = jnp.einsum('tg,tge->te', jax.nn.one_hot(g_idx[:, 0], N_GROUPS, dtype=jnp.float32), el)
    wp = jax.nn.softmax(within, axis=-1)
    w2, e2 = lax.top_k(wp, TOP_K_IN_GROUP)
    w2 = w2 / jnp.sum(w2, axis=-1, keepdims=True)
    e_global = g_idx * EXPERTS_PER_GROUP + e2
    gates = jnp.einsum('tk,tke->te', p_g * w2,
                       jax.nn.one_hot(e_global, N_EXPERTS, dtype=jnp.float32)).astype(t.dtype)
    out = jnp.zeros_like(t)
    for e in range(N_EXPERTS):
        y = (jax.nn.silu(t @ w_gate[e]) * (t @ w_up[e])) @ w_down[e]
        out = out + gates[:, e:e + 1] * y
    return out.reshape(B, S, D)


def _forward(x, attn_norm, w_in, q_norm, w_uq, kv_norm, w_ukv, diff_lq1, diff_lk1, diff_lq2, diff_lk2,
             diff_subln, w_out, ffn_norm, router_group, router_group_bias, router_expert,
             router_expert_bias, w_gate, w_up, w_down, final_norm):
    for l in range(DEPTH):
        lambda_init = 0.8 - 0.6 * math.exp(-0.3 * l)
        h = _rmsnorm(x, attn_norm[l])
        proj = h @ w_in[l]
        c_q, c_kv, k_pe, dq, dk, dv = jnp.split(proj, SPLIT_POINTS, axis=-1)
        a = _mla(c_q, c_kv, k_pe, q_norm[l], w_uq[l], kv_norm[l], w_ukv[l])
        b = _diff_attention(dq, dk, dv, diff_lq1[l], diff_lk1[l], diff_lq2[l], diff_lk2[l],
                            diff_subln[l], lambda_init)
        x = x + jnp.concatenate([a, b], axis=-1) @ w_out[l]
        x = x + _hier_moe(_rmsnorm(x, ffn_norm[l]), router_group[l], router_group_bias[l],
                          router_expert[l], router_expert_bias[l], w_gate[l], w_up[l], w_down[l])
    return _rmsnorm(x, final_norm)


def setup_inputs(seed: int = 0) -> dict:
    key = jax.random.key(seed)
    ks = jax.random.split(key, 24)
    L = DEPTH

    def w(k, shape, fan_in):
        return jax.random.normal(k, shape, jnp.float32) * (fan_in ** -0.5)

    def gain(k, shape):
        return 1.0 + 0.02 * jax.random.normal(k, shape, jnp.float32)

    return {
        "x_prompt": jax.random.normal(ks[0], (BATCH, SEQ, D_MODEL), jnp.float32),
        "x_sample": jax.random.normal(ks[1], (DEC_BATCH, DEC_SEQ, D_MODEL), jnp.float32),
        "attn_norm": gain(ks[2], (L, D_MODEL)),
        "w_in": w(ks[3], (L, D_MODEL, D_IN), D_MODEL),
        "q_norm": gain(ks[4], (L, Q_RANK)),
        "w_uq": w(ks[5], (L, Q_RANK, MLA_HEADS * (MLA_NOPE + MLA_ROPE)), Q_RANK),
        "kv_norm": gain(ks[6], (L, KV_RANK)),
        "w_ukv": w(ks[7], (L, KV_RANK, MLA_HEADS * (MLA_NOPE + MLA_V)), KV_RANK),
        "diff_lq1": 0.1 * jax.random.normal(ks[8], (L, DIFF_HEAD_DIM), jnp.float32),
        "diff_lk1": 0.1 * jax.random.normal(ks[9], (L, DIFF_HEAD_DIM), jnp.float32),
        "diff_lq2": 0.1 * jax.random.normal(ks[10], (L, DIFF_HEAD_DIM), jnp.float32),
        "diff_lk2": 0.1 * jax.random.normal(ks[11], (L, DIFF_HEAD_DIM), jnp.float32),
        "diff_subln": gain(ks[12], (L, 2 * DIFF_HEAD_DIM)),
        "w_out": w(ks[13], (L, D_MIX, D_MODEL), D_MIX),
        "ffn_norm": gain(ks[14], (L, D_MODEL)),
        "router_group": w(ks[15], (L, D_MODEL, N_GROUPS), D_MODEL),
        "router_group_bias": 0.01 * jax.random.normal(ks[16], (L, N_GROUPS), jnp.float32),
        "router_expert": w(ks[17], (L, D_MODEL, N_EXPERTS), D_MODEL),
        "router_expert_bias": 0.01 * jax.random.normal(ks[18], (L, N_EXPERTS), jnp.float32),
        "w_gate": w(ks[19], (L, N_EXPERTS, D_MODEL, EXPERT_FF), D_MODEL),
        "w_up": w(ks[20], (L, N_EXPERTS, D_MODEL, EXPERT_FF), D_MODEL),
        "w_down": w(ks[21], (L, N_EXPERTS, EXPERT_FF, D_MODEL), EXPERT_FF),
        "final_norm": gain(ks[22], (D_MODEL,)),
    }


def reference(x_prompt, x_sample, attn_norm, w_in, q_norm, w_uq, kv_norm, w_ukv, diff_lq1, diff_lk1,
              diff_lq2, diff_lk2, diff_subln, w_out, ffn_norm, router_group, router_group_bias,
              router_expert, router_expert_bias, w_gate, w_up, w_down, final_norm):
    y_prompt = _forward(x_prompt, attn_norm, w_in, q_norm, w_uq, kv_norm, w_ukv, diff_lq1, diff_lk1,
                        diff_lq2, diff_lk2, diff_subln, w_out, ffn_norm, router_group, router_group_bias,
                        router_expert, router_expert_bias, w_gate, w_up, w_down, final_norm)
    y_sample = _forward(x_sample, attn_norm, w_in, q_norm, w_uq, kv_norm, w_ukv, diff_lq1, diff_lk1,
                        diff_lq2, diff_lk2, diff_subln, w_out, ffn_norm, router_group, router_group_bias,
                        router_expert, router_expert_bias, w_gate, w_up, w_down, final_norm)
    return (y_prompt, y_sample)
```

```python
import functools
import math

import numpy as np
import jax
import jax.numpy as jnp
from jax import lax
from jax.experimental import pallas as pl
from jax.experimental.pallas import tpu as pltpu

D_MODEL = 1024
MLA_HEADS = 8
MLA_NOPE = 64
MLA_ROPE = 32
MLA_V = 64
Q_RANK = 256
KV_RANK = 128
DIFF_HEADS = 4
DIFF_HEAD_DIM = 64
DIFF_ROT = DIFF_HEAD_DIM // 4
ROPE_THETA = 500000.0
N_GROUPS = 4
EXPERTS_PER_GROUP = 8
N_EXPERTS = N_GROUPS * EXPERTS_PER_GROUP
EXPERT_FF = 256
EPS = 1e-6
LAMBDA_INIT = 0.8 - 0.6 * math.exp(-0.3 * 0)

LANES = 128
D_DIFF = DIFF_HEADS * 2 * DIFF_HEAD_DIM
D_MLA_PAD = MLA_HEADS * LANES
LOG2E = math.log2(math.e)
VMEM_LIMIT = 56 * 1024 * 1024

BF16 = jnp.bfloat16
F32 = jnp.float32

_C_Q = 0
_C_KV = _C_Q + Q_RANK
_C_PE = _C_KV + KV_RANK
_C_DQ = _C_PE + LANES
_C_DQS = _C_DQ + D_DIFF
_C_DK = _C_DQS + D_DIFF
_C_DKS = _C_DK + D_DIFF
_C_DV = _C_DKS + D_DIFF
_C_END = _C_DV + D_DIFF


def _rms(x, g):
    return x * lax.rsqrt(jnp.mean(x * x, axis=-1, keepdims=True) + EPS) * g


def _dot(a, b):
    return jnp.dot(a, b, preferred_element_type=F32)


def _dot_nt(a, b):
    return lax.dot_general(a, b, (((1,), (1,)), ((), ())), preferred_element_type=F32)


def _proj_kernel(x_ref, an_ref, w1_ref, qn_ref, wq_ref, kvn_ref, wkv_ref, wb_ref, vone_ref,
                 tqc_ref, tqs_ref, tpe_ref, tdqc_ref, tdqs_ref, tdkc_ref, tdks_ref,
                 q_ref, k_ref, v_ref, dq_ref, dk_ref, dv_ref):
    hn = _rms(x_ref[...], an_ref[...]).astype(BF16)

    def proj(lo, hi):
        return _dot(hn, w1_ref[:, lo:hi])

    cq = _rms(proj(_C_Q, _C_KV), qn_ref[...]).astype(BF16)
    tqc = tqc_ref[...]
    tqs = tqs_ref[...]
    for h in range(MLA_HEADS):
        lo = h * LANES
        qa = _dot(cq, wq_ref[:, lo:lo + LANES])
        qs = _dot(cq, wq_ref[:, D_MLA_PAD + lo:D_MLA_PAD + lo + LANES])
        q_ref[:, lo:lo + LANES] = (qa * tqc + qs * tqs).astype(BF16)

    ckv = _rms(proj(_C_KV, _C_PE), kvn_ref[...]).astype(BF16)
    pe = proj(_C_PE, _C_DQ) * tpe_ref[...]
    pe_hi = pe.astype(BF16)
    pe_lo = (pe - pe_hi.astype(F32)).astype(BF16)
    wb = wb_ref[...]
    k_all = _dot(ckv, wkv_ref[:, 0:D_MLA_PAD]) + _dot(pe_hi, wb) + _dot(pe_lo, wb)
    k_ref[...] = k_all.astype(BF16)
    v_ref[...] = (_dot(ckv, wkv_ref[:, D_MLA_PAD:2 * D_MLA_PAD]) + vone_ref[...]).astype(BF16)

    tdqc, tdqs, tdkc, tdks = tdqc_ref[...], tdqs_ref[...], tdkc_ref[...], tdks_ref[...]
    for h in range(DIFF_HEADS):
        lo = h * LANES
        dq = proj(_C_DQ + lo, _C_DQ + lo + LANES) * tdqc + proj(_C_DQS + lo, _C_DQS + lo + LANES) * tdqs
        dq_ref[:, lo:lo + LANES] = dq.astype(BF16)
        dk = proj(_C_DK + lo, _C_DK + lo + LANES) * tdkc + proj(_C_DKS + lo, _C_DKS + lo + LANES) * tdks
        dk_ref[:, lo:lo + LANES] = dk.astype(BF16)
    dv_ref[...] = proj(_C_DV, _C_END).astype(BF16)


def _proj_call(x2, packed, tables, seq, tm):
    T = x2.shape[0]
    nper = seq // tm
    row = lambda i: (i, 0)
    const = lambda i: (0, 0)
    tab = lambda i: (i % nper, 0)
    full = lambda a: pl.BlockSpec(a.shape, const)
    tspec = pl.BlockSpec((tm, LANES), tab)
    outs = [jax.ShapeDtypeStruct((T, D_MLA_PAD), BF16)] * 3 + [jax.ShapeDtypeStruct((T, D_DIFF), BF16)] * 3
    ospecs = [pl.BlockSpec((tm, D_MLA_PAD), row)] * 3 + [pl.BlockSpec((tm, D_DIFF), row)] * 3
    ws = [packed[k] for k in ("attn_norm", "w1", "q_norm", "wq", "kv_norm", "wkv", "wb", "vone")]
    return pl.pallas_call(
        _proj_kernel,
        grid=(T // tm,),
        in_specs=[pl.BlockSpec((tm, D_MODEL), row)] + [full(w) for w in ws] + [tspec] * 7,
        out_specs=ospecs,
        out_shape=outs,
        compiler_params=pltpu.CompilerParams(dimension_semantics=("arbitrary",), vmem_limit_bytes=VMEM_LIMIT),
        name="proj",
    )(x2, *ws, *tables)


def _mla_kernel(q_ref, k_ref, v_ref, o_ref):
    s = _dot_nt(q_ref[...], k_ref[...])
    m = jnp.max(s, axis=-1, keepdims=True)
    p = jnp.exp2(s - m).astype(BF16)
    oe = _dot(p, v_ref[...])
    o_ref[...] = (oe / oe[:, MLA_V:MLA_V + 1]).astype(BF16)


def _mla_call(q, k, v, tq):
    B, S, _ = q.shape
    qspec = pl.BlockSpec((None, tq, LANES), lambda b, h, i: (b, i, h))
    kspec = pl.BlockSpec((None, S, LANES), lambda b, h, i: (b, 0, h))
    return pl.pallas_call(
        _mla_kernel,
        grid=(B, MLA_HEADS, S // tq),
        in_specs=[qspec, kspec, kspec],
        out_specs=qspec,
        out_shape=jax.ShapeDtypeStruct((B, S, D_MLA_PAD), BF16),
        compiler_params=pltpu.CompilerParams(dimension_semantics=("arbitrary",) * 3, vmem_limit_bytes=VMEM_LIMIT),
        name="mla_attn",
    )(q, k, v)


def _diff_kernel(lq1_ref, lk1_ref, lq2_ref, lk2_ref, g_ref, q_ref, k_ref, v_ref, o_ref):
    lam = (jnp.exp(jnp.sum(lq1_ref[...] * lk1_ref[...], axis=-1, keepdims=True))
           - jnp.exp(jnp.sum(lq2_ref[...] * lk2_ref[...], axis=-1, keepdims=True)) + LAMBDA_INIT)
    q = q_ref[...]
    k = k_ref[...]
    first = lax.broadcasted_iota(jnp.int32, q.shape, 1) < DIFF_HEAD_DIM
    zero = jnp.zeros_like(q)

    def probs(qc):
        s = _dot_nt(qc, k)
        e = jnp.exp2(s - jnp.max(s, axis=-1, keepdims=True))
        return e * (1.0 / jnp.sum(e, axis=-1, keepdims=True))

    a = probs(jnp.where(first, q, zero)) - lam * probs(jnp.where(first, zero, q))
    o = _dot(a.astype(BF16), v_ref[...])
    o_ref[...] = (_rms(o, g_ref[...]) * (1.0 - LAMBDA_INIT)).astype(BF16)


def _diff_call(lams, subln, dq, dk, dv, tq):
    B, S, _ = dq.shape
    small = lambda a: pl.BlockSpec(a.shape, lambda b, h, i: (0, 0))
    qspec = pl.BlockSpec((None, tq, LANES), lambda b, h, i: (b, i, h))
    kspec = pl.BlockSpec((None, S, LANES), lambda b, h, i: (b, 0, h))
    return pl.pallas_call(
        _diff_kernel,
        grid=(B, DIFF_HEADS, S // tq),
        in_specs=[small(a) for a in lams] + [small(subln), qspec, kspec, kspec],
        out_specs=qspec,
        out_shape=jax.ShapeDtypeStruct((B, S, D_DIFF), BF16),
        compiler_params=pltpu.CompilerParams(dimension_semantics=("arbitrary",) * 3, vmem_limit_bytes=VMEM_LIMIT),
        name="diff_attn",
    )(*lams, subln, dq, dk, dv)


def _mix_kernel(x_ref, a_ref, b_ref, woa_ref, wob_ref, fn_ref, wrg_ref, brg_ref, wre_ref, bre_ref,
                xn_ref, hn_ref, gates_ref):
    xn = x_ref[...] + _dot(a_ref[...], woa_ref[...]) + _dot(b_ref[...], wob_ref[...])
    xn_ref[...] = xn
    hn = _rms(xn, fn_ref[...]).astype(BF16)
    hn_ref[...] = hn

    lane = lax.broadcasted_iota(jnp.int32, (xn.shape[0], LANES), 1)
    neg = jnp.float32(-jnp.inf)
    big = jnp.int32(LANES)

    def first_lane_of_max(vals):
        top = jnp.max(vals, axis=-1, keepdims=True)
        return top, jnp.min(jnp.where(vals == top, lane, big), axis=-1, keepdims=True)

    gl = jnp.where(lane < N_GROUPS, _dot(hn, wrg_ref[...]) + brg_ref[...], neg)
    ge = jnp.exp(gl - jnp.max(gl, axis=-1, keepdims=True))
    gp = ge / jnp.sum(ge, axis=-1, keepdims=True)
    p_g, g_idx = first_lane_of_max(gp)

    in_group = (lane < N_EXPERTS) & ((lane & -EXPERTS_PER_GROUP) == g_idx * EXPERTS_PER_GROUP)
    wl = jnp.where(in_group, _dot(hn, wre_ref[...]) + bre_ref[...], neg)
    we = jnp.exp(wl - jnp.max(wl, axis=-1, keepdims=True))
    wp = jnp.where(in_group, we / jnp.sum(we, axis=-1, keepdims=True), -1.0)
    w_a, i_a = first_lane_of_max(wp)
    w_b, i_b = first_lane_of_max(jnp.where(lane == i_a, -1.0, wp))
    tot = w_a + w_b
    gates_ref[...] = (jnp.where(lane == i_a, p_g * (w_a / tot), 0.0)
                      + jnp.where(lane == i_b, p_g * (w_b / tot), 0.0))


def _mix_call(x2, a2, b2, packed, tm):
    T = x2.shape[0]
    row = lambda i: (i, 0)
    full = lambda a: pl.BlockSpec(a.shape, lambda i: (0, 0))
    ws = [packed[k] for k in ("woa", "wob", "ffn_norm", "wrg", "brg", "wre", "bre")]
    return pl.pallas_call(
        _mix_kernel,
        grid=(T // tm,),
        in_specs=[pl.BlockSpec((tm, D_MODEL), row), pl.BlockSpec((tm, D_MLA_PAD), row),
                  pl.BlockSpec((tm, D_DIFF), row)] + [full(w) for w in ws],
        out_specs=[pl.BlockSpec((tm, D_MODEL), row), pl.BlockSpec((tm, D_MODEL), row),
                   pl.BlockSpec((tm, LANES), row)],
        out_shape=[jax.ShapeDtypeStruct((T, D_MODEL), F32), jax.ShapeDtypeStruct((T, D_MODEL), BF16),
                   jax.ShapeDtypeStruct((T, LANES), F32)],
        compiler_params=pltpu.CompilerParams(dimension_semantics=("arbitrary",), vmem_limit_bytes=VMEM_LIMIT),
        name="mix_router",
    )(x2, a2, b2, *ws)


def _moe_kernel(xn_ref, hn_ref, gates_ref, wg_ref, wu_ref, wd_ref, fin_ref, y_ref, acc_ref):
    e = pl.program_id(1)

    @pl.when(e == 0)
    def _():
        acc_ref[...] = xn_ref[...]

    t = hn_ref[...]
    g = _dot(t, wg_ref[...])
    h = (g * jax.nn.sigmoid(g) * _dot(t, wu_ref[...])).astype(BF16)
    lane = lax.broadcasted_iota(jnp.int32, gates_ref.shape, 1)
    gate = jnp.sum(jnp.where(lane == e, gates_ref[...], 0.0), axis=-1, keepdims=True)
    acc_ref[...] += gate * _dot(h, wd_ref[...])

    @pl.when(e == N_EXPERTS - 1)
    def _():
        y_ref[...] = _rms(acc_ref[...], fin_ref[...])


def _moe_call(xn, hn, gates, packed, tm):
    T = xn.shape[0]
    row = lambda i, e: (i, 0)
    return pl.pallas_call(
        _moe_kernel,
        grid=(T // tm, N_EXPERTS),
        in_specs=[pl.BlockSpec((tm, D_MODEL), row), pl.BlockSpec((tm, D_MODEL), row),
                  pl.BlockSpec((tm, LANES), row),
                  pl.BlockSpec((None, D_MODEL, EXPERT_FF), lambda i, e: (e, 0, 0)),
                  pl.BlockSpec((None, D_MODEL, EXPERT_FF), lambda i, e: (e, 0, 0)),
                  pl.BlockSpec((None, EXPERT_FF, D_MODEL), lambda i, e: (e, 0, 0)),
                  pl.BlockSpec((1, D_MODEL), lambda i, e: (0, 0))],
        out_specs=pl.BlockSpec((tm, D_MODEL), row),
        out_shape=jax.ShapeDtypeStruct((T, D_MODEL), F32),
        scratch_shapes=[pltpu.VMEM((tm, D_MODEL), F32)],
        compiler_params=pltpu.CompilerParams(dimension_semantics=("arbitrary", "arbitrary"),
                                             vmem_limit_bytes=VMEM_LIMIT),
        name="moe",
    )(xn, hn, gates, packed["wg"], packed["wu"], packed["wd"], packed["final_norm"])


def _swap_cols(w, block, off, rot):
    K = w.shape[0]
    wb = w.reshape(K, -1, block)
    half = rot // 2
    x1 = wb[:, :, off:off + half]
    x2 = wb[:, :, off + half:off + rot]
    out = jnp.zeros_like(wb)
    out = out.at[:, :, off:off + half].set(-x2).at[:, :, off + half:off + rot].set(x1)
    return out.reshape(w.shape)


def _pad_blocks(w, block, to):
    K = w.shape[0]
    wb = w.reshape(K, -1, block)
    return jnp.pad(wb, ((0, 0), (0, 0), (0, to - block))).reshape(K, -1)


def _pack_weights(attn_norm, w_in, q_norm, w_uq, kv_norm, w_ukv, w_out, ffn_norm, router_group,
                  router_group_bias, router_expert, router_expert_bias, w_gate, w_up, w_down, final_norm):
    l = 0
    wi = w_in[l]
    o_q, o_kv, o_pe = 0, Q_RANK, Q_RANK + KV_RANK
    o_dq = o_pe + MLA_ROPE
    o_dk = o_dq + D_DIFF
    o_dv = o_dk + D_DIFF
    w_pe = wi[:, o_pe:o_dq]
    w_dq = wi[:, o_dq:o_dk]
    w_dk = wi[:, o_dk:o_dv]
    pe_grp = jnp.concatenate([w_pe, _swap_cols(w_pe, MLA_ROPE, 0, MLA_ROPE),
                              jnp.zeros((D_MODEL, LANES - 2 * MLA_ROPE), F32)], axis=1)
    w1 = jnp.concatenate([wi[:, o_q:o_pe], pe_grp,
                          w_dq, _swap_cols(w_dq, DIFF_HEAD_DIM, 0, DIFF_ROT),
                          w_dk, _swap_cols(w_dk, DIFF_HEAD_DIM, 0, DIFF_ROT),
                          wi[:, o_dv:]], axis=1).astype(BF16)

    qk = MLA_NOPE + MLA_ROPE
    wq = jnp.concatenate([_pad_blocks(w_uq[l], qk, LANES),
                          _pad_blocks(_swap_cols(w_uq[l], qk, MLA_NOPE, MLA_ROPE), qk, LANES)], axis=1).astype(BF16)

    wkv3 = w_ukv[l].reshape(KV_RANK, MLA_HEADS, MLA_NOPE + MLA_V)
    padk = lambda a: jnp.pad(a, ((0, 0), (0, 0), (0, LANES - a.shape[-1]))).reshape(KV_RANK, D_MLA_PAD)
    wkv = jnp.concatenate([padk(wkv3[:, :, :MLA_NOPE]), padk(wkv3[:, :, MLA_NOPE:])], axis=1).astype(BF16)

    wb = np.zeros((LANES, MLA_HEADS, LANES), np.float32)
    vone = np.zeros((1, MLA_HEADS, LANES), np.float32)
    for r in range(MLA_ROPE):
        wb[r, :, MLA_NOPE + r] = 1.0
        wb[MLA_ROPE + r, :, MLA_NOPE + r] = 1.0
    vone[0, :, MLA_V] = 1.0

    wo = w_out[l]
    n_mla = MLA_HEADS * MLA_V
    woa = jnp.pad(wo[:n_mla].reshape(MLA_HEADS, MLA_V, D_MODEL),
                  ((0, 0), (0, LANES - MLA_V), (0, 0))).reshape(D_MLA_PAD, D_MODEL).astype(BF16)
    padl = lambda a: jnp.pad(a, ((0, 0), (0, LANES - a.shape[-1])))
    return {
        "attn_norm": attn_norm[l][None], "w1": w1, "q_norm": q_norm[l][None], "wq": wq,
        "kv_norm": kv_norm[l][None], "wkv": wkv,
        "wb": jnp.asarray(wb.reshape(LANES, D_MLA_PAD), BF16), "vone": jnp.asarray(vone.reshape(1, D_MLA_PAD)),
        "woa": woa, "wob": wo[n_mla:].astype(BF16), "ffn_norm": ffn_norm[l][None],
        "wrg": padl(router_group[l]).astype(BF16), "brg": padl(router_group_bias[l][None]),
        "wre": padl(router_expert[l]).astype(BF16), "bre": padl(router_expert_bias[l][None]),
        "wg": w_gate[l].astype(BF16), "wu": w_up[l].astype(BF16), "wd": w_down[l].astype(BF16),
        "final_norm": final_norm[None],
    }


def _cos_sin(seq, rot):
    half = rot // 2
    inv = ROPE_THETA ** (-jnp.arange(half, dtype=F32) / half)
    ang = jnp.arange(seq, dtype=F32)[:, None] * inv[None, :]
    return jnp.cos(ang), jnp.sin(ang)


def _rope_tables(seq):
    z = lambda n: jnp.zeros((seq, n), F32)
    o = lambda n: jnp.ones((seq, n), F32)
    c, s = _cos_sin(seq, MLA_ROPE)
    qs = (MLA_NOPE + MLA_ROPE) ** -0.5 * LOG2E
    tqc = jnp.concatenate([o(MLA_NOPE), c, c, z(LANES - MLA_NOPE - MLA_ROPE)], axis=1) * qs
    tqs = jnp.concatenate([z(MLA_NOPE), s, s, z(LANES - MLA_NOPE - MLA_ROPE)], axis=1) * qs
    tpe = jnp.concatenate([c, c, s, s, z(LANES - 2 * MLA_ROPE)], axis=1)
    c, s = _cos_sin(seq, DIFF_ROT)
    rest = DIFF_HEAD_DIM - DIFF_ROT
    tdc = jnp.concatenate([c, c, o(rest)] * 2, axis=1)
    tds = jnp.concatenate([s, s, z(rest)] * 2, axis=1)
    ds = DIFF_HEAD_DIM ** -0.5 * LOG2E
    return (tqc, tqs, tpe, tdc * ds, tds * ds, tdc, tds)


def _forward(x, packed, lams, subln):
    B, S, D = x.shape
    T = B * S
    x2 = x.reshape(T, D)
    q, k, v, dq, dk, dv = _proj_call(x2, packed, _rope_tables(S), S, tm=512)
    r3 = lambda a: a.reshape(B, S, a.shape[-1])
    a = _mla_call(r3(q), r3(k), r3(v), tq=256)
    b = _diff_call(lams, subln, r3(dq), r3(dk), r3(dv), tq=256)
    xn, hn, gates = _mix_call(x2, a.reshape(T, -1), b.reshape(T, -1), packed, tm=512)
    y = _moe_call(xn, hn, gates, packed, tm=1024)
    return y.reshape(B, S, D)


def kernel(x_prompt, x_sample, attn_norm, w_in, q_norm, w_uq, kv_norm, w_ukv, diff_lq1, diff_lk1, diff_lq2,
           diff_lk2, diff_subln, w_out, ffn_norm, router_group, router_group_bias, router_expert,
           router_expert_bias, w_gate, w_up, w_down, final_norm):
    packed = _pack_weights(attn_norm, w_in, q_norm, w_uq, kv_norm, w_ukv, w_out, ffn_norm, router_group,
                           router_group_bias, router_expert, router_expert_bias, w_gate, w_up, w_down,
                           final_norm)
    lams = (diff_lq1, diff_lk1, diff_lq2, diff_lk2)
    subln = diff_subln
    return (_forward(x_prompt, packed, lams, subln), _forward(x_sample, packed, lams, subln))
```

```python
import functools
import math

import numpy as np
import jax
import jax.numpy as jnp
from jax import lax
from jax.experimental import pallas as pl
from jax.experimental.pallas import tpu as pltpu

D_MODEL = 1024
MLA_HEADS = 8
MLA_NOPE = 64
MLA_ROPE = 32
MLA_V = 64
Q_RANK = 256
KV_RANK = 128
DIFF_HEADS = 4
DIFF_HEAD_DIM = 64
DIFF_ROT = DIFF_HEAD_DIM // 4
ROPE_THETA = 500000.0
N_GROUPS = 4
EXPERTS_PER_GROUP = 8
N_EXPERTS = N_GROUPS * EXPERTS_PER_GROUP
EXPERT_FF = 256
EPS = 1e-6
LAMBDA_INIT = 0.8 - 0.6 * math.exp(-0.3 * 0)

LANES = 128
D_DIFF = DIFF_HEADS * 2 * DIFF_HEAD_DIM
D_MLA_PAD = MLA_HEADS * LANES
LOG2E = math.log2(math.e)
VMEM_LIMIT = 56 * 1024 * 1024

BF16 = jnp.bfloat16
F32 = jnp.float32

_C_Q = 0
_C_KV = _C_Q + Q_RANK
_C_PE = _C_KV + KV_RANK
_C_DQ = _C_PE + LANES
_C_DQS = _C_DQ + D_DIFF
_C_DK = _C_DQS + D_DIFF
_C_DKS = _C_DK + D_DIFF
_C_DV = _C_DKS + D_DIFF
_C_END = _C_DV + D_DIFF


def _rms(x, g):
    return x * lax.rsqrt(jnp.mean(x * x, axis=-1, keepdims=True) + EPS) * g


def _dot(a, b):
    return jnp.dot(a, b, preferred_element_type=F32)


def _dot_nt(a, b):
    return lax.dot_general(a, b, (((1,), (1,)), ((), ())), preferred_element_type=F32)


def _proj_kernel(x_ref, an_ref, w1_ref, qn_ref, wq_ref, kvn_ref, wkv_ref, wb_ref, vone_ref,
                 tqc_ref, tqs_ref, tpe_ref, tdqc_ref, tdqs_ref, tdkc_ref, tdks_ref,
                 q_ref, k_ref, v_ref, dq_ref, dk_ref, dv_ref):
    hn = _rms(x_ref[...], an_ref[...]).astype(BF16)

    def proj(lo, hi):
        return _dot(hn, w1_ref[:, lo:hi])

    cq = _rms(proj(_C_Q, _C_KV), qn_ref[...]).astype(BF16)
    tqc = tqc_ref[...]
    tqs = tqs_ref[...]
    for h in range(MLA_HEADS):
        lo = h * LANES
        qa = _dot(cq, wq_ref[:, lo:lo + LANES])
        qs = _dot(cq, wq_ref[:, D_MLA_PAD + lo:D_MLA_PAD + lo + LANES])
        q_ref[:, lo:lo + LANES] = (qa * tqc + qs * tqs).astype(BF16)

    ckv = _rms(proj(_C_KV, _C_PE), kvn_ref[...]).astype(BF16)
    pe = proj(_C_PE, _C_DQ) * tpe_ref[...]
    pe_hi = pe.astype(BF16)
    pe_lo = (pe - pe_hi.astype(F32)).astype(BF16)
    wb = wb_ref[...]
    k_all = _dot(ckv, wkv_ref[:, 0:D_MLA_PAD]) + _dot(pe_hi, wb) + _dot(pe_lo, wb)
    k_ref[...] = k_all.astype(BF16)
    v_ref[...] = (_dot(ckv, wkv_ref[:, D_MLA_PAD:2 * D_MLA_PAD]) + vone_ref[...]).astype(BF16)

    tdqc, tdqs, tdkc, tdks = tdqc_ref[...], tdqs_ref[...], tdkc_ref[...], tdks_ref[...]
    for h in range(DIFF_HEADS):
        lo = h * LANES
        dq = proj(_C_DQ + lo, _C_DQ + lo + LANES) * tdqc + proj(_C_DQS + lo, _C_DQS + lo + LANES) * tdqs
        dq_ref[:, lo:lo + LANES] = dq.astype(BF16)
        dk = proj(_C_DK + lo, _C_DK + lo + LANES) * tdkc + proj(_C_DKS + lo, _C_DKS + lo + LANES) * tdks
        dk_ref[:, lo:lo + LANES] = dk.astype(BF16)
    dv_ref[...] = proj(_C_DV, _C_END).astype(BF16)


def _proj_call(x2, packed, tables, seq, tm):
    T = x2.shape[0]
    nper = seq // tm
    row = lambda i: (i, 0)
    const = lambda i: (0, 0)
    tab = lambda i: (i % nper, 0)
    full = lambda a: pl.BlockSpec(a.shape, const)
    tspec = pl.BlockSpec((tm, LANES), tab)
    outs = [jax.ShapeDtypeStruct((T, D_MLA_PAD), BF16)] * 3 + [jax.ShapeDtypeStruct((T, D_DIFF), BF16)] * 3
    ospecs = [pl.BlockSpec((tm, D_MLA_PAD), row)] * 3 + [pl.BlockSpec((tm, D_DIFF), row)] * 3
    ws = [packed[k] for k in ("attn_norm", "w1", "q_norm", "wq", "kv_norm", "wkv", "wb", "vone")]
    return pl.pallas_call(
        _proj_kernel,
        grid=(T // tm,),
        in_specs=[pl.BlockSpec((tm, D_MODEL), row)] + [full(w) for w in ws] + [tspec] * 7,
        out_specs=ospecs,
        out_shape=outs,
        compiler_params=pltpu.CompilerParams(dimension_semantics=("arbitrary",), vmem_limit_bytes=VMEM_LIMIT),
        name="proj",
    )(x2, *ws, *tables)


def _mla_kernel(q_ref, k_ref, v_ref, o_ref):
    s = _dot_nt(q_ref[...], k_ref[...])
    m = jnp.max(s, axis=-1, keepdims=True)
    p = jnp.exp2(s - m).astype(BF16)
    oe = _dot(p, v_ref[...])
    o_ref[...] = (oe / oe[:, MLA_V:MLA_V + 1]).astype(BF16)


def _mla_call(q, k, v, tq):
    B, S, _ = q.shape
    qspec = pl.BlockSpec((None, tq, LANES), lambda b, h, i: (b, i, h))
    kspec = pl.BlockSpec((None, S, LANES), lambda b, h, i: (b, 0, h))
    return pl.pallas_call(
        _mla_kernel,
        grid=(B, MLA_HEADS, S // tq),
        in_specs=[qspec, kspec, kspec],
        out_specs=qspec,
        out_shape=jax.ShapeDtypeStruct((B, S, D_MLA_PAD), BF16),
        compiler_params=pltpu.CompilerParams(dimension_semantics=("arbitrary",) * 3, vmem_limit_bytes=VMEM_LIMIT),
        name="mla_attn",
    )(q, k, v)


def _diff_kernel(lq1_ref, lk1_ref, lq2_ref, lk2_ref, g_ref, q_ref, k_ref, v_ref, o_ref):
    lam = (jnp.exp(jnp.sum(lq1_ref[...] * lk1_ref[...], axis=-1, keepdims=True))
           - jnp.exp(jnp.sum(lq2_ref[...] * lk2_ref[...], axis=-1, keepdims=True)) + LAMBDA_INIT)
    q = q_ref[...]
    k = k_ref[...]
    first = lax.broadcasted_iota(jnp.int32, q.shape, 1) < DIFF_HEAD_DIM
    zero = jnp.zeros_like(q)

    def probs(qc):
        s = _dot_nt(qc, k)
        e = jnp.exp2(s - jnp.max(s, axis=-1, keepdims=True))
        return e * (1.0 / jnp.sum(e, axis=-1, keepdims=True))

    a = probs(jnp.where(first, q, zero)) - lam * probs(jnp.where(first, zero, q))
    o = _dot(a.astype(BF16), v_ref[...])
    o_ref[...] = (_rms(o, g_ref[...]) * (1.0 - LAMBDA_INIT)).astype(BF16)


def _diff_call(lams, subln, dq, dk, dv, tq):
    B, S, _ = dq.shape
    small = lambda a: pl.BlockSpec(a.shape, lambda b, h, i: (0, 0))
    qspec = pl.BlockSpec((None, tq, LANES), lambda b, h, i: (b, i, h))
    kspec = pl.BlockSpec((None, S, LANES), lambda b, h, i: (b, 0, h))
    return pl.pallas_call(
        _diff_kernel,
        grid=(B, DIFF_HEADS, S // tq),
        in_specs=[small(a) for a in lams] + [small(subln), qspec, kspec, kspec],
        out_specs=qspec,
        out_shape=jax.ShapeDtypeStruct((B, S, D_DIFF), BF16),
        compiler_params=pltpu.CompilerParams(dimension_semantics=("arbitrary",) * 3, vmem_limit_bytes=VMEM_LIMIT),
        name="diff_attn",
    )(*lams, subln, dq, dk, dv)


def _mix_kernel(x_ref, a_ref, b_ref, woa_ref, wob_ref, fn_ref, wrg_ref, brg_ref, wre_ref, bre_ref,
                xn_ref, hn_ref, gate_ref, route_ref, count_ref, base_ref):
    @pl.when(pl.program_id(0) == 0)
    def _():
        base_ref[...] = jnp.zeros_like(base_ref)

    xn = x_ref[...] + _dot(a_ref[...], woa_ref[...]) + _dot(b_ref[...], wob_ref[...])
    xn_ref[...] = xn
    hn32 = _rms(xn, fn_ref[...])
    hn_ref[...] = hn32
    hn = hn32.astype(BF16)

    tm = xn.shape[0]
    lane = lax.broadcasted_iota(jnp.int32, (tm, LANES), 1)
    neg = jnp.float32(-jnp.inf)
    big = jnp.int32(LANES)

    def first_lane_of_max(vals):
        top = jnp.max(vals, axis=-1, keepdims=True)
        return top, jnp.min(jnp.where(vals == top, lane, big), axis=-1, keepdims=True)

    gl = jnp.where(lane < N_GROUPS, _dot(hn, wrg_ref[...]) + brg_ref[...], neg)
    ge = jnp.exp(gl - jnp.max(gl, axis=-1, keepdims=True))
    gp = ge / jnp.sum(ge, axis=-1, keepdims=True)
    p_g, g_idx = first_lane_of_max(gp)

    in_group = (lane < N_EXPERTS) & ((lane & -EXPERTS_PER_GROUP) == g_idx * EXPERTS_PER_GROUP)
    wl = jnp.where(in_group, _dot(hn, wre_ref[...]) + bre_ref[...], neg)
    we = jnp.exp(wl - jnp.max(wl, axis=-1, keepdims=True))
    wp = jnp.where(in_group, we / jnp.sum(we, axis=-1, keepdims=True), -1.0)
    w_a, i_a = first_lane_of_max(wp)
    w_b, i_b = first_lane_of_max(jnp.where(lane == i_a, -1.0, wp))
    tot = w_a + w_b
    gate_ref[...] = (jnp.where(lane == 0, p_g * (w_a / tot), 0.0)
                     + jnp.where(lane == 1, p_g * (w_b / tot), 0.0))

    picked = (lane == i_a) | (lane == i_b)
    rows = lax.broadcasted_iota(jnp.int32, (tm, tm), 0)
    cols = lax.broadcasted_iota(jnp.int32, (tm, tm), 1)
    earlier = jnp.where(rows > cols, 1.0, 0.0).astype(BF16)
    before = base_ref[...] + _dot(earlier, jnp.where(picked, 1.0, 0.0).astype(BF16))
    rank_a = jnp.sum(jnp.where(lane == i_a, before, 0.0), axis=-1, keepdims=True).astype(jnp.int32)
    rank_b = jnp.sum(jnp.where(lane == i_b, before, 0.0), axis=-1, keepdims=True).astype(jnp.int32)
    zero = jnp.zeros_like(lane)
    route_ref[...] = (jnp.where(lane == 0, i_a, zero) + jnp.where(lane == 1, i_b, zero)
                      + jnp.where(lane == 2, rank_a, zero) + jnp.where(lane == 3, rank_b, zero))
    base_ref[...] += jnp.sum(jnp.where(picked, 1.0, 0.0), axis=0, keepdims=True)
    count_ref[...] = base_ref[...]


def _mix_call(x2, a2, b2, packed, tm):
    T = x2.shape[0]
    row = lambda i: (i, 0)
    full = lambda a: pl.BlockSpec(a.shape, lambda i: (0, 0))
    ws = [packed[k] for k in ("woa", "wob", "ffn_norm", "wrg", "brg", "wre", "bre")]
    return pl.pallas_call(
        _mix_kernel,
        grid=(T // tm,),
        in_specs=[pl.BlockSpec((tm, D_MODEL), row), pl.BlockSpec((tm, D_MLA_PAD), row),
                  pl.BlockSpec((tm, D_DIFF), row)] + [full(w) for w in ws],
        out_specs=[pl.BlockSpec((tm, D_MODEL), row), pl.BlockSpec((tm, D_MODEL), row),
                   pl.BlockSpec((tm, LANES), row), pl.BlockSpec((tm, LANES), row),
                   pl.BlockSpec((1, LANES), lambda i: (0, 0))],
        out_shape=[jax.ShapeDtypeStruct((T, D_MODEL), F32), jax.ShapeDtypeStruct((T, D_MODEL), F32),
                   jax.ShapeDtypeStruct((T, LANES), F32), jax.ShapeDtypeStruct((T, LANES), jnp.int32),
                   jax.ShapeDtypeStruct((1, LANES), F32)],
        scratch_shapes=[pltpu.VMEM((1, LANES), F32)],
        compiler_params=pltpu.CompilerParams(dimension_semantics=("arbitrary",), vmem_limit_bytes=VMEM_LIMIT),
        name="mix_router",
    )(x2, a2, b2, *ws)


def _route_plan(route, count, tile_rows):
    T = route.shape[0]
    P = 2 * T
    nt = P // tile_rows
    i32 = jnp.int32
    eid = jnp.arange(N_EXPERTS, dtype=i32)
    counts = count[0, :N_EXPERTS].astype(i32)
    seg_end = jnp.sum(jnp.where(eid[None, :] <= eid[:, None], counts[None, :], 0), axis=1)
    seg_start = seg_end - counts
    picks = route[:, 0:2]
    dest = jnp.sum(jnp.where(picks[:, :, None] == eid, seg_start, 0), axis=-1) + route[:, 2:4]

    tile_start = jnp.arange(nt, dtype=i32) * tile_rows
    slot_of_tile = jnp.arange(nt, dtype=i32) + jnp.sum(seg_start[None, :] < tile_start[:, None], axis=1).astype(i32)
    slot_of_seg = eid + jnp.minimum(seg_start // tile_rows + 1, nt)
    vals = jnp.concatenate([tile_start, seg_start])
    slots = jnp.concatenate([slot_of_tile, slot_of_seg])
    n_items = nt + N_EXPERTS
    bounds = jnp.sum(jnp.where(slots[None, :] == jnp.arange(n_items, dtype=i32)[:, None], vals[None, :], 0), axis=1)
    nxt = jnp.concatenate([bounds[1:], jnp.full((1,), P, i32)])
    tile = jnp.minimum(bounds // tile_rows, nt - 1)
    lo = bounds - tile * tile_rows
    hi = nxt - tile * tile_rows
    expert = jnp.minimum(jnp.sum(seg_end[None, :] <= bounds[:, None], axis=1).astype(i32), N_EXPERTS - 1)
    return dest.astype(i32), (tile, expert, lo, hi)


def _dispatch_kernel(dest_ref, hn_hbm, xs_hbm, sem, *, tm):
    first_tok = pl.program_id(0) * tm

    def row_copy(p):
        src = hn_hbm.at[pl.ds(first_tok + (p >> 1), 1)]
        return pltpu.make_async_copy(src, xs_hbm.at[pl.ds(dest_ref[0, p], 1)], sem)

    def issue(p, carry):
        row_copy(p).start()
        return carry

    def drain(p, carry):
        row_copy(p).wait()
        return carry

    lax.fori_loop(0, 2 * tm, issue, 0, unroll=8)
    lax.fori_loop(0, 2 * tm, drain, 0, unroll=8)


def _dispatch_call(dest, hn, tm):
    T = hn.shape[0]
    dest3 = dest.reshape(T // tm, 1, 2 * tm)
    return pl.pallas_call(
        functools.partial(_dispatch_kernel, tm=tm),
        grid=(T // tm,),
        in_specs=[pl.BlockSpec((None, 1, 2 * tm), lambda i: (i, 0, 0), memory_space=pltpu.SMEM),
                  pl.BlockSpec(memory_space=pl.ANY)],
        out_specs=pl.BlockSpec(memory_space=pl.ANY),
        out_shape=jax.ShapeDtypeStruct((2 * T, D_MODEL), F32),
        scratch_shapes=[pltpu.SemaphoreType.DMA],
        compiler_params=pltpu.CompilerParams(dimension_semantics=("arbitrary",)),
        name="moe_dispatch",
    )(dest3, hn)


def _expert_kernel(tile_ref, exp_ref, lo_ref, hi_ref, xs_ref, wg_ref, wu_ref, wd_ref, ys_ref):
    w = pl.program_id(0)
    lo = lo_ref[w]
    hi = hi_ref[w]
    first_visit = (w == 0) | (tile_ref[w] != tile_ref[jnp.maximum(w - 1, 0)])

    @pl.when(first_visit)
    def _():
        ys_ref[...] = jnp.zeros_like(ys_ref)

    @pl.when(hi > lo)
    def _():
        x = xs_ref[...].astype(BF16)
        g = _dot(x, wg_ref[...])
        h = (g * jax.nn.sigmoid(g) * _dot(x, wu_ref[...])).astype(BF16)
        y = _dot(h, wd_ref[...])
        row = lax.broadcasted_iota(jnp.int32, (y.shape[0], 1), 0)
        ys_ref[...] = jnp.where((row >= lo) & (row < hi), y, ys_ref[...])


def _expert_call(plan, xs, packed, tm):
    P = xs.shape[0]
    n_items = plan[0].shape[0]
    by_tile = lambda w, tile, exp, lo, hi: (tile[w], 0)
    by_expert = lambda w, tile, exp, lo, hi: (exp[w], 0, 0)
    grid_spec = pltpu.PrefetchScalarGridSpec(
        num_scalar_prefetch=4,
        grid=(n_items,),
        in_specs=[pl.BlockSpec((tm, D_MODEL), by_tile),
                  pl.BlockSpec((None, D_MODEL, EXPERT_FF), by_expert),
                  pl.BlockSpec((None, D_MODEL, EXPERT_FF), by_expert),
                  pl.BlockSpec((None, EXPERT_FF, D_MODEL), by_expert)],
        out_specs=pl.BlockSpec((tm, D_MODEL), by_tile),
    )
    return pl.pallas_call(
        _expert_kernel,
        grid_spec=grid_spec,
        out_shape=jax.ShapeDtypeStruct((P, D_MODEL), F32),
        compiler_params=pltpu.CompilerParams(dimension_semantics=("arbitrary",), vmem_limit_bytes=VMEM_LIMIT),
        name="moe_experts",
    )(*plan, xs, packed["wg"], packed["wu"], packed["wd"])


def _combine_kernel(dest_ref, ys_hbm, xn_ref, gate_ref, fin_ref, y_ref, buf, sem, *, tm):
    def row_copy(p):
        return pltpu.make_async_copy(ys_hbm.at[pl.ds(dest_ref[0, p], 1)],
                                     buf.at[p & 1, pl.ds(p >> 1, 1)], sem)

    def issue(p, carry):
        row_copy(p).start()
        return carry

    def drain(p, carry):
        row_copy(p).wait()
        return carry

    lax.fori_loop(0, 2 * tm, issue, 0, unroll=8)
    lax.fori_loop(0, 2 * tm, drain, 0, unroll=8)
    gate = gate_ref[...]
    y = xn_ref[...] + gate[:, 0:1] * buf[0] + gate[:, 1:2] * buf[1]
    y_ref[...] = _rms(y, fin_ref[...])


def _combine_call(dest, ys, xn, gate, packed, tm):
    T = xn.shape[0]
    dest3 = dest.reshape(T // tm, 1, 2 * tm)
    row = lambda i: (i, 0)
    return pl.pallas_call(
        functools.partial(_combine_kernel, tm=tm),
        grid=(T // tm,),
        in_specs=[pl.BlockSpec((None, 1, 2 * tm), lambda i: (i, 0, 0), memory_space=pltpu.SMEM),
                  pl.BlockSpec(memory_space=pl.ANY),
                  pl.BlockSpec((tm, D_MODEL), row), pl.BlockSpec((tm, LANES), row),
                  pl.BlockSpec((1, D_MODEL), lambda i: (0, 0))],
        out_specs=pl.BlockSpec((tm, D_MODEL), row),
        out_shape=jax.ShapeDtypeStruct((T, D_MODEL), F32),
        scratch_shapes=[pltpu.VMEM((2, tm, D_MODEL), F32), pltpu.SemaphoreType.DMA],
        compiler_params=pltpu.CompilerParams(dimension_semantics=("arbitrary",), vmem_limit_bytes=VMEM_LIMIT),
        name="moe_combine",
    )(dest3, ys, xn, gate, packed["final_norm"])


def _swap_cols(w, block, off, rot):
    K = w.shape[0]
    wb = w.reshape(K, -1, block)
    half = rot // 2
    x1 = wb[:, :, off:off + half]
    x2 = wb[:, :, off + half:off + rot]
    out = jnp.zeros_like(wb)
    out = out.at[:, :, off:off + half].set(-x2).at[:, :, off + half:off + rot].set(x1)
    return out.reshape(w.shape)


def _pad_blocks(w, block, to):
    K = w.shape[0]
    wb = w.reshape(K, -1, block)
    return jnp.pad(wb, ((0, 0), (0, 0), (0, to - block))).reshape(K, -1)


def _pack_weights(attn_norm, w_in, q_norm, w_uq, kv_norm, w_ukv, w_out, ffn_norm, router_group,
                  router_group_bias, router_expert, router_expert_bias, w_gate, w_up, w_down, final_norm):
    l = 0
    wi = w_in[l]
    o_q, o_kv, o_pe = 0, Q_RANK, Q_RANK + KV_RANK
    o_dq = o_pe + MLA_ROPE
    o_dk = o_dq + D_DIFF
    o_dv = o_dk + D_DIFF
    w_pe = wi[:, o_pe:o_dq]
    w_dq = wi[:, o_dq:o_dk]
    w_dk = wi[:, o_dk:o_dv]
    pe_grp = jnp.concatenate([w_pe, _swap_cols(w_pe, MLA_ROPE, 0, MLA_ROPE),
                              jnp.zeros((D_MODEL, LANES - 2 * MLA_ROPE), F32)], axis=1)
    w1 = jnp.concatenate([wi[:, o_q:o_pe], pe_grp,
                          w_dq, _swap_cols(w_dq, DIFF_HEAD_DIM, 0, DIFF_ROT),
                          w_dk, _swap_cols(w_dk, DIFF_HEAD_DIM, 0, DIFF_ROT),
                          wi[:, o_dv:]], axis=1).astype(BF16)

    qk = MLA_NOPE + MLA_ROPE
    wq = jnp.concatenate([_pad_blocks(w_uq[l], qk, LANES),
                          _pad_blocks(_swap_cols(w_uq[l], qk, MLA_NOPE, MLA_ROPE), qk, LANES)], axis=1).astype(BF16)

    wkv3 = w_ukv[l].reshape(KV_RANK, MLA_HEADS, MLA_NOPE + MLA_V)
    padk = lambda a: jnp.pad(a, ((0, 0), (0, 0), (0, LANES - a.shape[-1]))).reshape(KV_RANK, D_MLA_PAD)
    wkv = jnp.concatenate([padk(wkv3[:, :, :MLA_NOPE]), padk(wkv3[:, :, MLA_NOPE:])], axis=1).astype(BF16)

    wb = np.zeros((LANES, MLA_HEADS, LANES), np.float32)
    vone = np.zeros((1, MLA_HEADS, LANES), np.float32)
    for r in range(MLA_ROPE):
        wb[r, :, MLA_NOPE + r] = 1.0
        wb[MLA_ROPE + r, :, MLA_NOPE + r] = 1.0
    vone[0, :, MLA_V] = 1.0

    wo = w_out[l]
    n_mla = MLA_HEADS * MLA_V
    woa = jnp.pad(wo[:n_mla].reshape(MLA_HEADS, MLA_V, D_MODEL),
                  ((0, 0), (0, LANES - MLA_V), (0, 0))).reshape(D_MLA_PAD, D_MODEL).astype(BF16)
    padl = lambda a: jnp.pad(a, ((0, 0), (0, LANES - a.shape[-1])))
    return {
        "attn_norm": attn_norm[l][None], "w1": w1, "q_norm": q_norm[l][None], "wq": wq,
        "kv_norm": kv_norm[l][None], "wkv": wkv,
        "wb": jnp.asarray(wb.reshape(LANES, D_MLA_PAD), BF16), "vone": jnp.asarray(vone.reshape(1, D_MLA_PAD)),
        "woa": woa, "wob": wo[n_mla:].astype(BF16), "ffn_norm": ffn_norm[l][None],
        "wrg": padl(router_group[l]).astype(BF16), "brg": padl(router_group_bias[l][None]),
        "wre": padl(router_expert[l]).astype(BF16), "bre": padl(router_expert_bias[l][None]),
        "wg": w_gate[l].astype(BF16), "wu": w_up[l].astype(BF16), "wd": w_down[l].astype(BF16),
        "final_norm": final_norm[None],
    }


def _cos_sin(seq, rot):
    half = rot // 2
    inv = ROPE_THETA ** (-jnp.arange(half, dtype=F32) / half)
    ang = jnp.arange(seq, dtype=F32)[:, None] * inv[None, :]
    return jnp.cos(ang), jnp.sin(ang)


def _rope_tables(seq):
    z = lambda n: jnp.zeros((seq, n), F32)
    o = lambda n: jnp.ones((seq, n), F32)
    c, s = _cos_sin(seq, MLA_ROPE)
    qs = (MLA_NOPE + MLA_ROPE) ** -0.5 * LOG2E
    tqc = jnp.concatenate([o(MLA_NOPE), c, c, z(LANES - MLA_NOPE - MLA_ROPE)], axis=1) * qs
    tqs = jnp.concatenate([z(MLA_NOPE), s, s, z(LANES - MLA_NOPE - MLA_ROPE)], axis=1) * qs
    tpe = jnp.concatenate([c, c, s, s, z(LANES - 2 * MLA_ROPE)], axis=1)
    c, s = _cos_sin(seq, DIFF_ROT)
    rest = DIFF_HEAD_DIM - DIFF_ROT
    tdc = jnp.concatenate([c, c, o(rest)] * 2, axis=1)
    tds = jnp.concatenate([s, s, z(rest)] * 2, axis=1)
    ds = DIFF_HEAD_DIM ** -0.5 * LOG2E
    return (tqc, tqs, tpe, tdc * ds, tds * ds, tdc, tds)


def _forward(x, packed, lams, subln):
    B, S, D = x.shape
    T = B * S
    x2 = x.reshape(T, D)
    q, k, v, dq, dk, dv = _proj_call(x2, packed, _rope_tables(S), S, tm=512)
    r3 = lambda a: a.reshape(B, S, a.shape[-1])
    a = _mla_call(r3(q), r3(k), r3(v), tq=256)
    b = _diff_call(lams, subln, r3(dq), r3(dk), r3(dv), tq=256)
    xn, hn, gate, route, count = _mix_call(x2, a.reshape(T, -1), b.reshape(T, -1), packed, tm=512)
    dest, plan = _route_plan(route, count, tile_rows=256)
    xs = _dispatch_call(dest, hn, tm=512)
    ys = _expert_call(plan, xs, packed, tm=256)
    y = _combine_call(dest, ys, xn, gate, packed, tm=512)
    return y.reshape(B, S, D)


def kernel(x_prompt, x_sample, attn_norm, w_in, q_norm, w_uq, kv_norm, w_ukv, diff_lq1, diff_lk1, diff_lq2,
           diff_lk2, diff_subln, w_out, ffn_norm, router_group, router_group_bias, router_expert,
           router_expert_bias, w_gate, w_up, w_down, final_norm):
    packed = _pack_weights(attn_norm, w_in, q_norm, w_uq, kv_norm, w_ukv, w_out, ffn_norm, router_group,
                           router_group_bias, router_expert, router_expert_bias, w_gate, w_up, w_down,
                           final_norm)
    lams = (diff_lq1, diff_lk1, diff_lq2, diff_lk2)
    subln = diff_subln
    return (_forward(x_prompt, packed, lams, subln), _forward(x_sample, packed, lams, subln))
```

```python
import functools
import math

import numpy as np
import jax
import jax.numpy as jnp
from jax import lax
from jax.experimental import pallas as pl
from jax.experimental.pallas import tpu as pltpu

D_MODEL = 1024
MLA_HEADS = 8
MLA_NOPE = 64
MLA_ROPE = 32
MLA_V = 64
Q_RANK = 256
KV_RANK = 128
DIFF_HEADS = 4
DIFF_HEAD_DIM = 64
DIFF_ROT = DIFF_HEAD_DIM // 4
ROPE_THETA = 500000.0
N_GROUPS = 4
EXPERTS_PER_GROUP = 8
N_EXPERTS = N_GROUPS * EXPERTS_PER_GROUP
EXPERT_FF = 256
EPS = 1e-6
LAMBDA_INIT = 0.8 - 0.6 * math.exp(-0.3 * 0)

LANES = 128
D_DIFF = DIFF_HEADS * 2 * DIFF_HEAD_DIM
D_MLA_PAD = MLA_HEADS * LANES
LOG2E = math.log2(math.e)
VMEM_LIMIT = 56 * 1024 * 1024

BF16 = jnp.bfloat16
F32 = jnp.float32

_C_Q = 0
_C_KV = _C_Q + Q_RANK
_C_PE = _C_KV + KV_RANK
_C_DQ = _C_PE + LANES
_C_DQS = _C_DQ + D_DIFF
_C_DK = _C_DQS + D_DIFF
_C_DKS = _C_DK + D_DIFF
_C_END = _C_DKS + D_DIFF


def _rms(x, g):
    return x * lax.rsqrt(jnp.mean(x * x, axis=-1, keepdims=True) + EPS) * g


def _dot(a, b):
    return jnp.dot(a, b, preferred_element_type=F32)


def _dot_nt(a, b):
    return lax.dot_general(a, b, (((1,), (1,)), ((), ())), preferred_element_type=F32)


def _proj_kernel(x_ref, an_ref, w1_ref, qn_ref, wq_ref, kvn_ref, wk_ref, wvt_ref, wdvt_ref,
                 tqc_ref, tqs_ref, tpe_ref, tdqc_ref, tdqs_ref, tdkc_ref, tdks_ref,
                 q_ref, k_ref, vt_ref, dq_ref, dk_ref, dvt_ref):
    hn = _rms(x_ref[...], an_ref[...]).astype(BF16)

    def proj(lo, hi):
        return _dot(hn, w1_ref[:, lo:hi])

    cq = _rms(proj(_C_Q, _C_KV), qn_ref[...]).astype(BF16)
    tqc = tqc_ref[...]
    tqs = tqs_ref[...]
    qa = _dot(cq, wq_ref[:, 0:D_MLA_PAD])
    qs = _dot(cq, wq_ref[:, D_MLA_PAD:2 * D_MLA_PAD])
    for h in range(MLA_HEADS):
        blk = slice(h * LANES, (h + 1) * LANES)
        q_ref[:, blk] = (qa[:, blk] * tqc + qs[:, blk] * tqs).astype(BF16)

    kv_pe = proj(_C_KV, _C_DQ)
    ckv = _rms(kv_pe[:, 0:KV_RANK], kvn_ref[...]).astype(BF16)
    pe = kv_pe[:, KV_RANK:] * tpe_ref[...]
    pe_hi = pe.astype(BF16)
    pe_lo = (pe - pe_hi.astype(F32)).astype(BF16)
    k_all = _dot(jnp.concatenate([ckv, pe_hi, pe_lo], axis=1), wk_ref[...])
    k_ref[...] = k_all.astype(BF16)
    vt = _dot_nt(wvt_ref[...], ckv)
    ones_row = (lax.broadcasted_iota(jnp.int32, vt.shape, 0) & (LANES - 1)) == MLA_V
    vt_ref[...] = jnp.where(ones_row, 1.0, vt).astype(BF16)

    tdqc, tdqs, tdkc, tdks = tdqc_ref[...], tdqs_ref[...], tdkc_ref[...], tdks_ref[...]
    dq, dqs = proj(_C_DQ, _C_DQS), proj(_C_DQS, _C_DK)
    dk, dks = proj(_C_DK, _C_DKS), proj(_C_DKS, _C_END)
    for h in range(DIFF_HEADS):
        blk = slice(h * LANES, (h + 1) * LANES)
        dq_ref[:, blk] = (dq[:, blk] * tdqc + dqs[:, blk] * tdqs).astype(BF16)
        dk_ref[:, blk] = (dk[:, blk] * tdkc + dks[:, blk] * tdks).astype(BF16)
    dvt_ref[...] = _dot_nt(wdvt_ref[...], hn).astype(BF16)


def _proj_call(x2, packed, tables, seq, tm):
    T = x2.shape[0]
    nper = seq // tm
    B = T // seq
    row = lambda i: (i, 0)
    const = lambda i: (0, 0)
    tab = lambda i: (i % nper, 0)
    col = lambda i: (i // nper, 0, i % nper)
    full = lambda a: pl.BlockSpec(a.shape, const)
    tspec = pl.BlockSpec((tm, LANES), tab)
    tok = lambda d: (jax.ShapeDtypeStruct((T, d), BF16), pl.BlockSpec((tm, d), row))
    tr = lambda d: (jax.ShapeDtypeStruct((B, d, seq), BF16), pl.BlockSpec((None, d, tm), col))
    outs, ospecs = zip(tok(D_MLA_PAD), tok(D_MLA_PAD), tr(D_MLA_PAD), tok(D_DIFF), tok(D_DIFF), tr(D_DIFF))
    ws = [packed[k] for k in ("attn_norm", "w1", "q_norm", "wq", "kv_norm", "wk", "wvt", "wdvt")]
    return pl.pallas_call(
        _proj_kernel,
        grid=(T // tm,),
        in_specs=[pl.BlockSpec((tm, D_MODEL), row)] + [full(w) for w in ws] + [tspec] * 7,
        out_specs=ospecs,
        out_shape=outs,
        compiler_params=pltpu.CompilerParams(dimension_semantics=("arbitrary",), vmem_limit_bytes=VMEM_LIMIT),
        name="proj",
    )(x2, *ws, *tables)


def _block_index_maps(n_heads, n_qblocks, n_blocks):
    per_batch = n_heads * n_qblocks

    def split(blk):
        b = blk // per_batch
        r = blk % per_batch
        return b, r // n_qblocks, r % n_qblocks

    def rows(blk):
        b, h, i = split(blk)
        return b, i, h

    def keys(blk):
        b, h, _ = split(blk)
        return b, 0, h

    def values_t(blk):
        b, h, _ = split(blk)
        return b, h, 0

    cur = lambda t: jnp.minimum(t, n_blocks - 1)
    prev = lambda t: jnp.maximum(t - 1, 0)
    return (lambda t: rows(cur(t)), lambda t: keys(cur(t)), lambda t: rows(prev(t)), lambda t: values_t(prev(t)))


def _pipelined_steps(step, zero_scratch):
    t = pl.program_id(0)

    @pl.when(t == 0)
    def _():
        zero_scratch()

    @pl.when(t % 2 == 0)
    def _():
        step(1, 0)

    @pl.when(t % 2 == 1)
    def _():
        step(0, 1)


def _mla_kernel(q_ref, k_ref, vt_ref, o_ref, s0_ref, s1_ref, m0_ref, m1_ref):
    s_refs, m_refs = (s0_ref, s1_ref), (m0_ref, m1_ref)

    def step(src, dst):
        pt = jnp.exp2(s_refs[src][...] - m_refs[src][...]).astype(BF16)
        ot = _dot(vt_ref[...], pt)
        o_ref[...] = (ot / ot[MLA_V:MLA_V + 1, :]).T.astype(BF16)
        st = _dot_nt(k_ref[...], q_ref[...])
        s_refs[dst][...] = st
        m_refs[dst][...] = jnp.max(st, axis=0, keepdims=True)

    def zero_scratch():
        s1_ref[...] = jnp.zeros_like(s1_ref)
        m1_ref[...] = jnp.zeros_like(m1_ref)

    _pipelined_steps(step, zero_scratch)


def _mla_call(q, k, vt, tq):
    B, S, _ = q.shape
    n_blocks = B * MLA_HEADS * (S // tq)
    q_map, k_map, o_map, vt_map = _block_index_maps(MLA_HEADS, S // tq, n_blocks)
    return pl.pallas_call(
        _mla_kernel,
        grid=(n_blocks + 1,),
        in_specs=[pl.BlockSpec((None, tq, LANES), q_map), pl.BlockSpec((None, S, LANES), k_map),
                  pl.BlockSpec((None, LANES, S), vt_map)],
        out_specs=pl.BlockSpec((None, tq, LANES), o_map),
        out_shape=jax.ShapeDtypeStruct((B, S, D_MLA_PAD), BF16),
        scratch_shapes=[pltpu.VMEM((S, tq), F32)] * 2 + [pltpu.VMEM((1, tq), F32)] * 2,
        compiler_params=pltpu.CompilerParams(dimension_semantics=("arbitrary",), vmem_limit_bytes=VMEM_LIMIT),
        name="mla_attn",
    )(q, k, vt)


def _diff_kernel(lq1_ref, lk1_ref, lq2_ref, lk2_ref, g_ref, q_ref, k_ref, vt_ref, o_ref,
                 sa0_ref, sa1_ref, sb0_ref, sb1_ref, ma0_ref, ma1_ref, mb0_ref, mb1_ref):
    sa_refs, sb_refs = (sa0_ref, sa1_ref), (sb0_ref, sb1_ref)
    ma_refs, mb_refs = (ma0_ref, ma1_ref), (mb0_ref, mb1_ref)

    def step(src, dst):
        lam = (jnp.exp(jnp.sum(lq1_ref[...] * lk1_ref[...], axis=-1, keepdims=True))
               - jnp.exp(jnp.sum(lq2_ref[...] * lk2_ref[...], axis=-1, keepdims=True)) + LAMBDA_INIT)
        ea = jnp.exp2(sa_refs[src][...] - ma_refs[src][...])
        eb = jnp.exp2(sb_refs[src][...] - mb_refs[src][...])
        ra = 1.0 / jnp.sum(ea, axis=0, keepdims=True)
        rb = lam / jnp.sum(eb, axis=0, keepdims=True)
        vt = vt_ref[...]
        ot = _dot(vt, ea.astype(BF16)) * ra - _dot(vt, eb.astype(BF16)) * rb
        o_ref[...] = (_rms(ot.T, g_ref[...]) * (1.0 - LAMBDA_INIT)).astype(BF16)

        q = q_ref[...]
        k = k_ref[...]
        first = lax.broadcasted_iota(jnp.int32, q.shape, 1) < DIFF_HEAD_DIM
        zero = jnp.zeros_like(q)
        for s_refs, m_refs, qc in ((sa_refs, ma_refs, jnp.where(first, q, zero)),
                                   (sb_refs, mb_refs, jnp.where(first, zero, q))):
            st = _dot_nt(k, qc)
            s_refs[dst][...] = st
            m_refs[dst][...] = jnp.max(st, axis=0, keepdims=True)

    def zero_scratch():
        for ref in (sa1_ref, sb1_ref, ma1_ref, mb1_ref):
            ref[...] = jnp.zeros_like(ref)

    _pipelined_steps(step, zero_scratch)


def _diff_call(lams, subln, dq, dk, dvt, tq):
    B, S, _ = dq.shape
    n_blocks = B * DIFF_HEADS * (S // tq)
    q_map, k_map, o_map, vt_map = _block_index_maps(DIFF_HEADS, S // tq, n_blocks)
    small = lambda a: pl.BlockSpec(a.shape, lambda t: (0, 0))
    return pl.pallas_call(
        _diff_kernel,
        grid=(n_blocks + 1,),
        in_specs=[small(a) for a in lams] + [small(subln), pl.BlockSpec((None, tq, LANES), q_map),
                                             pl.BlockSpec((None, S, LANES), k_map),
                                             pl.BlockSpec((None, LANES, S), vt_map)],
        out_specs=pl.BlockSpec((None, tq, LANES), o_map),
        out_shape=jax.ShapeDtypeStruct((B, S, D_DIFF), BF16),
        scratch_shapes=[pltpu.VMEM((S, tq), F32)] * 4 + [pltpu.VMEM((1, tq), F32)] * 4,
        compiler_params=pltpu.CompilerParams(dimension_semantics=("arbitrary",), vmem_limit_bytes=VMEM_LIMIT),
        name="diff_attn",
    )(*lams, subln, dq, dk, dvt)


def _mix_kernel(x_ref, a_ref, b_ref, woa_ref, wob_ref, fn_ref, wrg_ref, brg_ref, wre_ref, bre_ref,
                xn_ref, hn_ref, gate_ref, route_ref, count_ref, base_ref):
    @pl.when(pl.program_id(0) == 0)
    def _():
        base_ref[...] = jnp.zeros_like(base_ref)

    xn = x_ref[...] + _dot(a_ref[...], woa_ref[...]) + _dot(b_ref[...], wob_ref[...])
    xn_ref[...] = xn
    hn32 = _rms(xn, fn_ref[...])
    hn_ref[...] = hn32
    hn = hn32.astype(BF16)

    tm = xn.shape[0]
    lane = lax.broadcasted_iota(jnp.int32, (tm, LANES), 1)
    neg = jnp.float32(-jnp.inf)
    big = jnp.int32(LANES)

    def first_lane_of_max(vals):
        top = jnp.max(vals, axis=-1, keepdims=True)
        return top, jnp.min(jnp.where(vals == top, lane, big), axis=-1, keepdims=True)

    gl = jnp.where(lane < N_GROUPS, _dot(hn, wrg_ref[...]) + brg_ref[...], neg)
    ge = jnp.exp(gl - jnp.max(gl, axis=-1, keepdims=True))
    gp = ge / jnp.sum(ge, axis=-1, keepdims=True)
    p_g, g_idx = first_lane_of_max(gp)

    in_group = (lane < N_EXPERTS) & ((lane & -EXPERTS_PER_GROUP) == g_idx * EXPERTS_PER_GROUP)
    wl = jnp.where(in_group, _dot(hn, wre_ref[...]) + bre_ref[...], neg)
    we = jnp.exp(wl - jnp.max(wl, axis=-1, keepdims=True))
    wp = jnp.where(in_group, we / jnp.sum(we, axis=-1, keepdims=True), -1.0)
    w_a, i_a = first_lane_of_max(wp)
    w_b, i_b = first_lane_of_max(jnp.where(lane == i_a, -1.0, wp))
    tot = w_a + w_b
    gate_ref[...] = (jnp.where(lane == 0, p_g * (w_a / tot), 0.0)
                     + jnp.where(lane == 1, p_g * (w_b / tot), 0.0))

    picked = (lane == i_a) | (lane == i_b)
    rows = lax.broadcasted_iota(jnp.int32, (tm, tm), 0)
    cols = lax.broadcasted_iota(jnp.int32, (tm, tm), 1)
    earlier = jnp.where(rows > cols, 1.0, 0.0).astype(BF16)
    before = base_ref[...] + _dot(earlier, jnp.where(picked, 1.0, 0.0).astype(BF16))
    rank_a = jnp.sum(jnp.where(lane == i_a, before, 0.0), axis=-1, keepdims=True).astype(jnp.int32)
    rank_b = jnp.sum(jnp.where(lane == i_b, before, 0.0), axis=-1, keepdims=True).astype(jnp.int32)
    zero = jnp.zeros_like(lane)
    route_ref[...] = (jnp.where(lane == 0, i_a, zero) + jnp.where(lane == 1, i_b, zero)
                      + jnp.where(lane == 2, rank_a, zero) + jnp.where(lane == 3, rank_b, zero))
    base_ref[...] += jnp.sum(jnp.where(picked, 1.0, 0.0), axis=0, keepdims=True)
    count_ref[...] = base_ref[...]


def _mix_call(x2, a2, b2, packed, tm):
    T = x2.shape[0]
    row = lambda i: (i, 0)
    full = lambda a: pl.BlockSpec(a.shape, lambda i: (0, 0))
    ws = [packed[k] for k in ("woa", "wob", "ffn_norm", "wrg", "brg", "wre", "bre")]
    return pl.pallas_call(
        _mix_kernel,
        grid=(T // tm,),
        in_specs=[pl.BlockSpec((tm, D_MODEL), row), pl.BlockSpec((tm, D_MLA_PAD), row),
                  pl.BlockSpec((tm, D_DIFF), row)] + [full(w) for w in ws],
        out_specs=[pl.BlockSpec((tm, D_MODEL), row), pl.BlockSpec((tm, D_MODEL), row),
                   pl.BlockSpec((tm, LANES), row), pl.BlockSpec((tm, LANES), row),
                   pl.BlockSpec((1, LANES), lambda i: (0, 0))],
        out_shape=[jax.ShapeDtypeStruct((T, D_MODEL), F32), jax.ShapeDtypeStruct((T, D_MODEL), F32),
                   jax.ShapeDtypeStruct((T, LANES), F32), jax.ShapeDtypeStruct((T, LANES), jnp.int32),
                   jax.ShapeDtypeStruct((1, LANES), F32)],
        scratch_shapes=[pltpu.VMEM((1, LANES), F32)],
        compiler_params=pltpu.CompilerParams(dimension_semantics=("arbitrary",), vmem_limit_bytes=VMEM_LIMIT),
        name="mix_router",
    )(x2, a2, b2, *ws)


def _route_plan(route, count, tile_rows):
    T = route.shape[0]
    P = 2 * T
    nt = P // tile_rows
    i32 = jnp.int32
    eid = jnp.arange(N_EXPERTS, dtype=i32)
    counts = count[0, :N_EXPERTS].astype(i32)
    seg_end = jnp.sum(jnp.where(eid[None, :] <= eid[:, None], counts[None, :], 0), axis=1)
    seg_start = seg_end - counts
    picks = route[:, 0:2]
    dest = jnp.sum(jnp.where(picks[:, :, None] == eid, seg_start, 0), axis=-1) + route[:, 2:4]

    tile_start = jnp.arange(nt, dtype=i32) * tile_rows
    slot_of_tile = jnp.arange(nt, dtype=i32) + jnp.sum(seg_start[None, :] < tile_start[:, None], axis=1).astype(i32)
    slot_of_seg = eid + jnp.minimum(seg_start // tile_rows + 1, nt)
    vals = jnp.concatenate([tile_start, seg_start])
    slots = jnp.concatenate([slot_of_tile, slot_of_seg])
    n_items = nt + N_EXPERTS
    bounds = jnp.sum(jnp.where(slots[None, :] == jnp.arange(n_items, dtype=i32)[:, None], vals[None, :], 0), axis=1)
    nxt = jnp.concatenate([bounds[1:], jnp.full((1,), P, i32)])
    tile = jnp.minimum(bounds // tile_rows, nt - 1)
    lo = bounds - tile * tile_rows
    hi = nxt - tile * tile_rows
    expert = jnp.minimum(jnp.sum(seg_end[None, :] <= bounds[:, None], axis=1).astype(i32), N_EXPERTS - 1)
    return dest.astype(i32), (tile, expert, lo, hi)


def _dispatch_kernel(dest_ref, hn_ref, xs_hbm, sem, *, tm):
    def row_copy(p):
        return pltpu.make_async_copy(hn_ref.at[pl.ds(p >> 1, 1)], xs_hbm.at[pl.ds(dest_ref[0, p], 1)], sem)

    def issue(p, carry):
        row_copy(p).start()
        return carry

    lax.fori_loop(0, 2 * tm, issue, 0, unroll=8)
    for _ in range(2):
        pltpu.make_async_copy(hn_ref, xs_hbm.at[pl.ds(0, tm)], sem).wait()


def _dispatch_call(dest, hn, tm):
    T = hn.shape[0]
    dest3 = dest.reshape(T // tm, 1, 2 * tm)
    return pl.pallas_call(
        functools.partial(_dispatch_kernel, tm=tm),
        grid=(T // tm,),
        in_specs=[pl.BlockSpec((None, 1, 2 * tm), lambda i: (i, 0, 0), memory_space=pltpu.SMEM),
                  pl.BlockSpec((tm, D_MODEL), lambda i: (i, 0))],
        out_specs=pl.BlockSpec(memory_space=pl.ANY),
        out_shape=jax.ShapeDtypeStruct((2 * T, D_MODEL), F32),
        scratch_shapes=[pltpu.SemaphoreType.DMA],
        compiler_params=pltpu.CompilerParams(dimension_semantics=("arbitrary",)),
        name="moe_dispatch",
    )(dest3, hn)


def _expert_kernel(tile_ref, exp_ref, lo_ref, hi_ref, xs_ref, wg_ref, wu_ref, wd_ref, ys_ref):
    w = pl.program_id(0)
    lo = lo_ref[w]
    hi = hi_ref[w]
    first_visit = (w == 0) | (tile_ref[w] != tile_ref[jnp.maximum(w - 1, 0)])

    @pl.when(first_visit)
    def _():
        ys_ref[...] = jnp.zeros_like(ys_ref)

    @pl.when(hi > lo)
    def _():
        x = xs_ref[...].astype(BF16)
        g = _dot(x, wg_ref[...])
        h = (g * jax.nn.sigmoid(g) * _dot(x, wu_ref[...])).astype(BF16)
        y = _dot(h, wd_ref[...])
        row = lax.broadcasted_iota(jnp.int32, (y.shape[0], 1), 0)
        ys_ref[...] = jnp.where((row >= lo) & (row < hi), y, ys_ref[...])


def _expert_call(plan, xs, packed, tm):
    P = xs.shape[0]
    n_items = plan[0].shape[0]
    by_tile = lambda w, tile, exp, lo, hi: (tile[w], 0)
    by_expert = lambda w, tile, exp, lo, hi: (exp[w], 0, 0)
    grid_spec = pltpu.PrefetchScalarGridSpec(
        num_scalar_prefetch=4,
        grid=(n_items,),
        in_specs=[pl.BlockSpec((tm, D_MODEL), by_tile),
                  pl.BlockSpec((None, D_MODEL, EXPERT_FF), by_expert),
                  pl.BlockSpec((None, D_MODEL, EXPERT_FF), by_expert),
                  pl.BlockSpec((None, EXPERT_FF, D_MODEL), by_expert)],
        out_specs=pl.BlockSpec((tm, D_MODEL), by_tile),
    )
    return pl.pallas_call(
        _expert_kernel,
        grid_spec=grid_spec,
        out_shape=jax.ShapeDtypeStruct((P, D_MODEL), F32),
        compiler_params=pltpu.CompilerParams(dimension_semantics=("arbitrary",), vmem_limit_bytes=VMEM_LIMIT),
        name="moe_experts",
    )(*plan, xs, packed["wg"], packed["wu"], packed["wd"])


def _combine_kernel(dest_ref, ys_hbm, xn_ref, gate_ref, fin_ref, y_ref, buf, sem, *, tm):
    def row_copy(p):
        return pltpu.make_async_copy(ys_hbm.at[pl.ds(dest_ref[0, p], 1)],
                                     buf.at[p & 1, pl.ds(p >> 1, 1)], sem)

    def issue(p, carry):
        row_copy(p).start()
        return carry

    lax.fori_loop(0, 2 * tm, issue, 0, unroll=8)
    for pick in range(2):
        pltpu.make_async_copy(ys_hbm.at[pl.ds(0, tm)], buf.at[pick], sem).wait()
    gate = gate_ref[...]
    y = xn_ref[...] + gate[:, 0:1] * buf[0] + gate[:, 1:2] * buf[1]
    y_ref[...] = _rms(y, fin_ref[...])


def _combine_call(dest, ys, xn, gate, packed, tm):
    T = xn.shape[0]
    dest3 = dest.reshape(T // tm, 1, 2 * tm)
    row = lambda i: (i, 0)
    return pl.pallas_call(
        functools.partial(_combine_kernel, tm=tm),
        grid=(T // tm,),
        in_specs=[pl.BlockSpec((None, 1, 2 * tm), lambda i: (i, 0, 0), memory_space=pltpu.SMEM),
                  pl.BlockSpec(memory_space=pl.ANY),
                  pl.BlockSpec((tm, D_MODEL), row), pl.BlockSpec((tm, LANES), row),
                  pl.BlockSpec((1, D_MODEL), lambda i: (0, 0))],
        out_specs=pl.BlockSpec((tm, D_MODEL), row),
        out_shape=jax.ShapeDtypeStruct((T, D_MODEL), F32),
        scratch_shapes=[pltpu.VMEM((2, tm, D_MODEL), F32), pltpu.SemaphoreType.DMA],
        compiler_params=pltpu.CompilerParams(dimension_semantics=("arbitrary",), vmem_limit_bytes=VMEM_LIMIT),
        name="moe_combine",
    )(dest3, ys, xn, gate, packed["final_norm"])


def _swap_cols(w, block, off, rot):
    K = w.shape[0]
    wb = w.reshape(K, -1, block)
    half = rot // 2
    x1 = wb[:, :, off:off + half]
    x2 = wb[:, :, off + half:off + rot]
    out = jnp.zeros_like(wb)
    out = out.at[:, :, off:off + half].set(-x2).at[:, :, off + half:off + rot].set(x1)
    return out.reshape(w.shape)


def _pad_blocks(w, block, to):
    K = w.shape[0]
    wb = w.reshape(K, -1, block)
    return jnp.pad(wb, ((0, 0), (0, 0), (0, to - block))).reshape(K, -1)


def _pack_weights(attn_norm, w_in, q_norm, w_uq, kv_norm, w_ukv, w_out, ffn_norm, router_group,
                  router_group_bias, router_expert, router_expert_bias, w_gate, w_up, w_down, final_norm):
    l = 0
    wi = w_in[l]
    o_q, o_kv, o_pe = 0, Q_RANK, Q_RANK + KV_RANK
    o_dq = o_pe + MLA_ROPE
    o_dk = o_dq + D_DIFF
    o_dv = o_dk + D_DIFF
    w_pe = wi[:, o_pe:o_dq]
    w_dq = wi[:, o_dq:o_dk]
    w_dk = wi[:, o_dk:o_dv]
    pe_grp = jnp.concatenate([w_pe, _swap_cols(w_pe, MLA_ROPE, 0, MLA_ROPE),
                              jnp.zeros((D_MODEL, LANES - 2 * MLA_ROPE), F32)], axis=1)
    w1 = jnp.concatenate([wi[:, o_q:o_pe], pe_grp,
                          w_dq, _swap_cols(w_dq, DIFF_HEAD_DIM, 0, DIFF_ROT),
                          w_dk, _swap_cols(w_dk, DIFF_HEAD_DIM, 0, DIFF_ROT)], axis=1).astype(BF16)
    wdvt = wi[:, o_dv:].T.astype(BF16)

    qk = MLA_NOPE + MLA_ROPE
    wq = jnp.concatenate([_pad_blocks(w_uq[l], qk, LANES),
                          _pad_blocks(_swap_cols(w_uq[l], qk, MLA_NOPE, MLA_ROPE), qk, LANES)], axis=1).astype(BF16)

    wkv3 = w_ukv[l].reshape(KV_RANK, MLA_HEADS, MLA_NOPE + MLA_V)
    padk = lambda a: jnp.pad(a, ((0, 0), (0, 0), (0, LANES - a.shape[-1]))).reshape(KV_RANK, D_MLA_PAD)
    wvt = padk(wkv3[:, :, MLA_NOPE:]).T.astype(BF16)
    place = np.zeros((LANES, MLA_HEADS, LANES), np.float32)
    for r in range(MLA_ROPE):
        place[r, :, MLA_NOPE + r] = 1.0
        place[MLA_ROPE + r, :, MLA_NOPE + r] = 1.0
    place = jnp.asarray(place.reshape(LANES, D_MLA_PAD))
    wk = jnp.concatenate([padk(wkv3[:, :, :MLA_NOPE]), place, place], axis=0).astype(BF16)

    wo = w_out[l]
    n_mla = MLA_HEADS * MLA_V
    woa = jnp.pad(wo[:n_mla].reshape(MLA_HEADS, MLA_V, D_MODEL),
                  ((0, 0), (0, LANES - MLA_V), (0, 0))).reshape(D_MLA_PAD, D_MODEL).astype(BF16)
    padl = lambda a: jnp.pad(a, ((0, 0), (0, LANES - a.shape[-1])))
    return {
        "attn_norm": attn_norm[l][None], "w1": w1, "q_norm": q_norm[l][None], "wq": wq,
        "kv_norm": kv_norm[l][None], "wk": wk, "wvt": wvt, "wdvt": wdvt,
        "woa": woa, "wob": wo[n_mla:].astype(BF16), "ffn_norm": ffn_norm[l][None],
        "wrg": padl(router_group[l]).astype(BF16), "brg": padl(router_group_bias[l][None]),
        "wre": padl(router_expert[l]).astype(BF16), "bre": padl(router_expert_bias[l][None]),
        "wg": w_gate[l].astype(BF16), "wu": w_up[l].astype(BF16), "wd": w_down[l].astype(BF16),
        "final_norm": final_norm[None],
    }


def _cos_sin(seq, rot):
    half = rot // 2
    inv = ROPE_THETA ** (-jnp.arange(half, dtype=F32) / half)
    ang = jnp.arange(seq, dtype=F32)[:, None] * inv[None, :]
    return jnp.cos(ang), jnp.sin(ang)


def _rope_tables(seq):
    z = lambda n: jnp.zeros((seq, n), F32)
    o = lambda n: jnp.ones((seq, n), F32)
    c, s = _cos_sin(seq, MLA_ROPE)
    qs = (MLA_NOPE + MLA_ROPE) ** -0.5 * LOG2E
    tqc = jnp.concatenate([o(MLA_NOPE), c, c, z(LANES - MLA_NOPE - MLA_ROPE)], axis=1) * qs
    tqs = jnp.concatenate([z(MLA_NOPE), s, s, z(LANES - MLA_NOPE - MLA_ROPE)], axis=1) * qs
    tpe = jnp.concatenate([c, c, s, s, z(LANES - 2 * MLA_ROPE)], axis=1)
    c, s = _cos_sin(seq, DIFF_ROT)
    rest = DIFF_HEAD_DIM - DIFF_ROT
    tdc = jnp.concatenate([c, c, o(rest)] * 2, axis=1)
    tds = jnp.concatenate([s, s, z(rest)] * 2, axis=1)
    ds = DIFF_HEAD_DIM ** -0.5 * LOG2E
    return (tqc, tqs, tpe, tdc * ds, tds * ds, tdc, tds)


def _forward(x, packed, lams, subln):
    B, S, D = x.shape
    T = B * S
    x2 = x.reshape(T, D)
    q, k, vt, dq, dk, dvt = _proj_call(x2, packed, _rope_tables(S), S, tm=512)
    r3 = lambda a: a.reshape(B, S, a.shape[-1])
    a = _mla_call(r3(q), r3(k), vt, tq=512)
    b = _diff_call(lams, subln, r3(dq), r3(dk), dvt, tq=512)
    xn, hn, gate, route, count = _mix_call(x2, a.reshape(T, -1), b.reshape(T, -1), packed, tm=512)
    dest, plan = _route_plan(route, count, tile_rows=256)
    xs = _dispatch_call(dest, hn, tm=512)
    ys = _expert_call(plan, xs, packed, tm=256)
    y = _combine_call(dest, ys, xn, gate, packed, tm=512)
    return y.reshape(B, S, D)


def kernel(x_prompt, x_sample, attn_norm, w_in, q_norm, w_uq, kv_norm, w_ukv, diff_lq1, diff_lk1, diff_lq2,
           diff_lk2, diff_subln, w_out, ffn_norm, router_group, router_group_bias, router_expert,
           router_expert_bias, w_gate, w_up, w_down, final_norm):
    packed = _pack_weights(attn_norm, w_in, q_norm, w_uq, kv_norm, w_ukv, w_out, ffn_norm, router_group,
                           router_group_bias, router_expert, router_expert_bias, w_gate, w_up, w_down,
                           final_norm)
    lams = (diff_lq1, diff_lk1, diff_lq2, diff_lk2)
    subln = diff_subln
    return (_forward(x_prompt, packed, lams, subln), _forward(x_sample, packed, lams, subln))
```

```python
import functools
import math

import numpy as np
import jax
import jax.numpy as jnp
from jax import lax
from jax.experimental import pallas as pl
from jax.experimental.pallas import tpu as pltpu

D_MODEL = 1024
MLA_HEADS = 8
MLA_NOPE = 64
MLA_ROPE = 32
MLA_V = 64
Q_RANK = 256
KV_RANK = 128
DIFF_HEADS = 4
DIFF_HEAD_DIM = 64
DIFF_ROT = DIFF_HEAD_DIM // 4
ROPE_THETA = 500000.0
N_GROUPS = 4
EXPERTS_PER_GROUP = 8
N_EXPERTS = N_GROUPS * EXPERTS_PER_GROUP
EXPERT_FF = 256
EPS = 1e-6
LAMBDA_INIT = 0.8 - 0.6 * math.exp(-0.3 * 0)

LANES = 128
D_DIFF = DIFF_HEADS * 2 * DIFF_HEAD_DIM
D_MLA_PAD = MLA_HEADS * LANES
LOG2E = math.log2(math.e)
VMEM_LIMIT = 56 * 1024 * 1024

BF16 = jnp.bfloat16
F32 = jnp.float32

_C_Q = 0
_C_KV = _C_Q + Q_RANK
_C_PE = _C_KV + KV_RANK
_C_DQ = _C_PE + LANES
_C_DQS = _C_DQ + D_DIFF
_C_DK = _C_DQS + D_DIFF
_C_DKS = _C_DK + D_DIFF
_C_END = _C_DKS + D_DIFF


def _rms(x, g):
    return x * lax.rsqrt(jnp.mean(x * x, axis=-1, keepdims=True) + EPS) * g


def _dot(a, b):
    return jnp.dot(a, b, preferred_element_type=F32)


def _dot_nt(a, b):
    return lax.dot_general(a, b, (((1,), (1,)), ((), ())), preferred_element_type=F32)


ROW_TILES = D_MODEL // LANES
assert ROW_TILES == 8


def _token_tile(ref, t):
    return ref.at[pl.ds(pl.multiple_of(t * ROW_TILES, ROW_TILES), ROW_TILES)]


def _store_row_tiles(ref, x):
    for j in range(ROW_TILES):
        ref[pl.ds(j, x.shape[0], stride=ROW_TILES), :] = x[:, j * LANES:(j + 1) * LANES]


def _load_row_tiles(ref):
    rows = ref.shape[0] // ROW_TILES
    return jnp.concatenate([ref[pl.ds(j, rows, stride=ROW_TILES), :] for j in range(ROW_TILES)], axis=1)


def _proj_kernel(x_ref, an_ref, w1_ref, qn_ref, wq_ref, kvn_ref, wk_ref, wvt_ref, wdvt_ref,
                 tqc_ref, tqs_ref, tpe_ref, tdqc_ref, tdqs_ref, tdkc_ref, tdks_ref,
                 q_ref, k_ref, vt_ref, dq_ref, dk_ref, dvt_ref):
    hn = _rms(x_ref[...], an_ref[...]).astype(BF16)

    def proj(lo, hi):
        return _dot(hn, w1_ref[:, lo:hi])

    cq = _rms(proj(_C_Q, _C_KV), qn_ref[...]).astype(BF16)
    tqc = tqc_ref[...]
    tqs = tqs_ref[...]
    qa = _dot(cq, wq_ref[:, 0:D_MLA_PAD])
    qs = _dot(cq, wq_ref[:, D_MLA_PAD:2 * D_MLA_PAD])
    for h in range(MLA_HEADS):
        blk = slice(h * LANES, (h + 1) * LANES)
        q_ref[:, blk] = (qa[:, blk] * tqc + qs[:, blk] * tqs).astype(BF16)

    kv_pe = proj(_C_KV, _C_DQ)
    ckv = _rms(kv_pe[:, 0:KV_RANK], kvn_ref[...]).astype(BF16)
    pe = kv_pe[:, KV_RANK:] * tpe_ref[...]
    pe_hi = pe.astype(BF16)
    pe_lo = (pe - pe_hi.astype(F32)).astype(BF16)
    k_all = _dot(jnp.concatenate([ckv, pe_hi, pe_lo], axis=1), wk_ref[...])
    k_ref[...] = k_all.astype(BF16)
    vt = _dot_nt(wvt_ref[...], ckv)
    ones_row = (lax.broadcasted_iota(jnp.int32, vt.shape, 0) & (LANES - 1)) == MLA_V
    vt_ref[...] = jnp.where(ones_row, 1.0, vt).astype(BF16)

    tdqc, tdqs, tdkc, tdks = tdqc_ref[...], tdqs_ref[...], tdkc_ref[...], tdks_ref[...]
    dq, dqs = proj(_C_DQ, _C_DQS), proj(_C_DQS, _C_DK)
    dk, dks = proj(_C_DK, _C_DKS), proj(_C_DKS, _C_END)
    for h in range(DIFF_HEADS):
        blk = slice(h * LANES, (h + 1) * LANES)
        dq_ref[:, blk] = (dq[:, blk] * tdqc + dqs[:, blk] * tdqs).astype(BF16)
        dk_ref[:, blk] = (dk[:, blk] * tdkc + dks[:, blk] * tdks).astype(BF16)
    dvt_ref[...] = _dot_nt(wdvt_ref[...], hn).astype(BF16)


def _proj_call(x2, packed, tables, seq, tm):
    T = x2.shape[0]
    nper = seq // tm
    B = T // seq
    row = lambda i: (i, 0)
    const = lambda i: (0, 0)
    tab = lambda i: (i % nper, 0)
    col = lambda i: (i // nper, 0, i % nper)
    full = lambda a: pl.BlockSpec(a.shape, const)
    tspec = pl.BlockSpec((tm, LANES), tab)
    tok = lambda d: (jax.ShapeDtypeStruct((T, d), BF16), pl.BlockSpec((tm, d), row))
    tr = lambda d: (jax.ShapeDtypeStruct((B, d, seq), BF16), pl.BlockSpec((None, d, tm), col))
    outs, ospecs = zip(tok(D_MLA_PAD), tok(D_MLA_PAD), tr(D_MLA_PAD), tok(D_DIFF), tok(D_DIFF), tr(D_DIFF))
    ws = [packed[k] for k in ("attn_norm", "w1", "q_norm", "wq", "kv_norm", "wk", "wvt", "wdvt")]
    return pl.pallas_call(
        _proj_kernel,
        grid=(T // tm,),
        in_specs=[pl.BlockSpec((tm, D_MODEL), row)] + [full(w) for w in ws] + [tspec] * 7,
        out_specs=ospecs,
        out_shape=outs,
        compiler_params=pltpu.CompilerParams(dimension_semantics=("arbitrary",), vmem_limit_bytes=VMEM_LIMIT),
        name="proj",
    )(x2, *ws, *tables)


def _block_index_maps(n_heads, n_qblocks, n_blocks):
    per_batch = n_heads * n_qblocks

    def split(blk):
        b = blk // per_batch
        r = blk % per_batch
        return b, r // n_qblocks, r % n_qblocks

    def rows(blk):
        b, h, i = split(blk)
        return b, i, h

    def keys(blk):
        b, h, _ = split(blk)
        return b, 0, h

    def values_t(blk):
        b, h, _ = split(blk)
        return b, h, 0

    cur = lambda t: jnp.minimum(t, n_blocks - 1)
    prev = lambda t: jnp.maximum(t - 1, 0)
    return (lambda t: rows(cur(t)), lambda t: keys(cur(t)), lambda t: rows(prev(t)), lambda t: values_t(prev(t)))


def _pipelined_steps(step, zero_scratch):
    t = pl.program_id(0)

    @pl.when(t == 0)
    def _():
        zero_scratch()

    @pl.when(t % 2 == 0)
    def _():
        step(1, 0)

    @pl.when(t % 2 == 1)
    def _():
        step(0, 1)


def _mla_kernel(q_ref, k_ref, vt_ref, o_ref, s0_ref, s1_ref, m0_ref, m1_ref):
    s_refs, m_refs = (s0_ref, s1_ref), (m0_ref, m1_ref)

    def step(src, dst):
        pt = jnp.exp2(s_refs[src][...] - m_refs[src][...]).astype(BF16)
        ot = _dot(vt_ref[...], pt)
        o_ref[...] = (ot / ot[MLA_V:MLA_V + 1, :]).T.astype(BF16)
        st = _dot_nt(k_ref[...], q_ref[...])
        s_refs[dst][...] = st
        m_refs[dst][...] = jnp.max(st, axis=0, keepdims=True)

    def zero_scratch():
        s1_ref[...] = jnp.zeros_like(s1_ref)
        m1_ref[...] = jnp.zeros_like(m1_ref)

    _pipelined_steps(step, zero_scratch)


def _mla_call(q, k, vt, tq):
    B, S, _ = q.shape
    n_blocks = B * MLA_HEADS * (S // tq)
    q_map, k_map, o_map, vt_map = _block_index_maps(MLA_HEADS, S // tq, n_blocks)
    return pl.pallas_call(
        _mla_kernel,
        grid=(n_blocks + 1,),
        in_specs=[pl.BlockSpec((None, tq, LANES), q_map), pl.BlockSpec((None, S, LANES), k_map),
                  pl.BlockSpec((None, LANES, S), vt_map)],
        out_specs=pl.BlockSpec((None, tq, LANES), o_map),
        out_shape=jax.ShapeDtypeStruct((B, S, D_MLA_PAD), BF16),
        scratch_shapes=[pltpu.VMEM((S, tq), F32)] * 2 + [pltpu.VMEM((1, tq), F32)] * 2,
        compiler_params=pltpu.CompilerParams(dimension_semantics=("arbitrary",), vmem_limit_bytes=VMEM_LIMIT),
        name="mla_attn",
    )(q, k, vt)


def _diff_kernel(lq1_ref, lk1_ref, lq2_ref, lk2_ref, g_ref, q_ref, k_ref, vt_ref, o_ref,
                 sa0_ref, sa1_ref, sb0_ref, sb1_ref, ma0_ref, ma1_ref, mb0_ref, mb1_ref):
    sa_refs, sb_refs = (sa0_ref, sa1_ref), (sb0_ref, sb1_ref)
    ma_refs, mb_refs = (ma0_ref, ma1_ref), (mb0_ref, mb1_ref)

    def step(src, dst):
        lam = (jnp.exp(jnp.sum(lq1_ref[...] * lk1_ref[...], axis=-1, keepdims=True))
               - jnp.exp(jnp.sum(lq2_ref[...] * lk2_ref[...], axis=-1, keepdims=True)) + LAMBDA_INIT)
        ea = jnp.exp2(sa_refs[src][...] - ma_refs[src][...])
        eb = jnp.exp2(sb_refs[src][...] - mb_refs[src][...])
        ra = 1.0 / jnp.sum(ea, axis=0, keepdims=True)
        rb = lam / jnp.sum(eb, axis=0, keepdims=True)
        vt = vt_ref[...]
        ot = _dot(vt, ea.astype(BF16)) * ra - _dot(vt, eb.astype(BF16)) * rb
        o_ref[...] = (_rms(ot.T, g_ref[...]) * (1.0 - LAMBDA_INIT)).astype(BF16)

        q = q_ref[...]
        k = k_ref[...]
        first = lax.broadcasted_iota(jnp.int32, q.shape, 1) < DIFF_HEAD_DIM
        zero = jnp.zeros_like(q)
        for s_refs, m_refs, qc in ((sa_refs, ma_refs, jnp.where(first, q, zero)),
                                   (sb_refs, mb_refs, jnp.where(first, zero, q))):
            st = _dot_nt(k, qc)
            s_refs[dst][...] = st
            m_refs[dst][...] = jnp.max(st, axis=0, keepdims=True)

    def zero_scratch():
        for ref in (sa1_ref, sb1_ref, ma1_ref, mb1_ref):
            ref[...] = jnp.zeros_like(ref)

    _pipelined_steps(step, zero_scratch)


def _diff_call(lams, subln, dq, dk, dvt, tq):
    B, S, _ = dq.shape
    n_blocks = B * DIFF_HEADS * (S // tq)
    q_map, k_map, o_map, vt_map = _block_index_maps(DIFF_HEADS, S // tq, n_blocks)
    small = lambda a: pl.BlockSpec(a.shape, lambda t: (0, 0))
    return pl.pallas_call(
        _diff_kernel,
        grid=(n_blocks + 1,),
        in_specs=[small(a) for a in lams] + [small(subln), pl.BlockSpec((None, tq, LANES), q_map),
                                             pl.BlockSpec((None, S, LANES), k_map),
                                             pl.BlockSpec((None, LANES, S), vt_map)],
        out_specs=pl.BlockSpec((None, tq, LANES), o_map),
        out_shape=jax.ShapeDtypeStruct((B, S, D_DIFF), BF16),
        scratch_shapes=[pltpu.VMEM((S, tq), F32)] * 4 + [pltpu.VMEM((1, tq), F32)] * 4,
        compiler_params=pltpu.CompilerParams(dimension_semantics=("arbitrary",), vmem_limit_bytes=VMEM_LIMIT),
        name="diff_attn",
    )(*lams, subln, dq, dk, dvt)


def _mix_kernel(x_ref, a_ref, b_ref, woa_ref, wob_ref, fn_ref, wrg_ref, brg_ref, wre_ref, bre_ref,
                xn_ref, hn_ref, gate_ref, route_ref, count_ref, base_ref):
    @pl.when(pl.program_id(0) == 0)
    def _():
        base_ref[...] = jnp.zeros_like(base_ref)

    xn = x_ref[...] + _dot(a_ref[...], woa_ref[...]) + _dot(b_ref[...], wob_ref[...])
    xn_ref[...] = xn
    hn32 = _rms(xn, fn_ref[...])
    _store_row_tiles(hn_ref, hn32)
    hn = hn32.astype(BF16)

    tm = xn.shape[0]
    lane = lax.broadcasted_iota(jnp.int32, (tm, LANES), 1)
    neg = jnp.float32(-jnp.inf)
    big = jnp.int32(LANES)

    def first_lane_of_max(vals):
        top = jnp.max(vals, axis=-1, keepdims=True)
        return top, jnp.min(jnp.where(vals == top, lane, big), axis=-1, keepdims=True)

    gl = jnp.where(lane < N_GROUPS, _dot(hn, wrg_ref[...]) + brg_ref[...], neg)
    ge = jnp.exp(gl - jnp.max(gl, axis=-1, keepdims=True))
    gp = ge / jnp.sum(ge, axis=-1, keepdims=True)
    p_g, g_idx = first_lane_of_max(gp)

    in_group = (lane < N_EXPERTS) & ((lane & -EXPERTS_PER_GROUP) == g_idx * EXPERTS_PER_GROUP)
    wl = jnp.where(in_group, _dot(hn, wre_ref[...]) + bre_ref[...], neg)
    we = jnp.exp(wl - jnp.max(wl, axis=-1, keepdims=True))
    wp = jnp.where(in_group, we / jnp.sum(we, axis=-1, keepdims=True), -1.0)
    w_a, i_a = first_lane_of_max(wp)
    w_b, i_b = first_lane_of_max(jnp.where(lane == i_a, -1.0, wp))
    tot = w_a + w_b
    gate_ref[...] = (jnp.where(lane == 0, p_g * (w_a / tot), 0.0)
                     + jnp.where(lane == 1, p_g * (w_b / tot), 0.0))

    picked = (lane == i_a) | (lane == i_b)
    rows = lax.broadcasted_iota(jnp.int32, (tm, tm), 0)
    cols = lax.broadcasted_iota(jnp.int32, (tm, tm), 1)
    earlier = jnp.where(rows > cols, 1.0, 0.0).astype(BF16)
    before = base_ref[...] + _dot(earlier, jnp.where(picked, 1.0, 0.0).astype(BF16))
    rank_a = jnp.sum(jnp.where(lane == i_a, before, 0.0), axis=-1, keepdims=True).astype(jnp.int32)
    rank_b = jnp.sum(jnp.where(lane == i_b, before, 0.0), axis=-1, keepdims=True).astype(jnp.int32)
    zero = jnp.zeros_like(lane)
    route_ref[...] = (jnp.where(lane == 0, i_a, zero) + jnp.where(lane == 1, i_b, zero)
                      + jnp.where(lane == 2, rank_a, zero) + jnp.where(lane == 3, rank_b, zero))
    base_ref[...] += jnp.sum(jnp.where(picked, 1.0, 0.0), axis=0, keepdims=True)
    count_ref[...] = base_ref[...]


def _mix_call(x2, a2, b2, packed, tm):
    T = x2.shape[0]
    row = lambda i: (i, 0)
    full = lambda a: pl.BlockSpec(a.shape, lambda i: (0, 0))
    ws = [packed[k] for k in ("woa", "wob", "ffn_norm", "wrg", "brg", "wre", "bre")]
    return pl.pallas_call(
        _mix_kernel,
        grid=(T // tm,),
        in_specs=[pl.BlockSpec((tm, D_MODEL), row), pl.BlockSpec((tm, D_MLA_PAD), row),
                  pl.BlockSpec((tm, D_DIFF), row)] + [full(w) for w in ws],
        out_specs=[pl.BlockSpec((tm, D_MODEL), row), pl.BlockSpec((tm * ROW_TILES, LANES), row),
                   pl.BlockSpec((tm, LANES), row), pl.BlockSpec((tm, LANES), row),
                   pl.BlockSpec((1, LANES), lambda i: (0, 0))],
        out_shape=[jax.ShapeDtypeStruct((T, D_MODEL), F32), jax.ShapeDtypeStruct((T * ROW_TILES, LANES), F32),
                   jax.ShapeDtypeStruct((T, LANES), F32), jax.ShapeDtypeStruct((T, LANES), jnp.int32),
                   jax.ShapeDtypeStruct((1, LANES), F32)],
        scratch_shapes=[pltpu.VMEM((1, LANES), F32)],
        compiler_params=pltpu.CompilerParams(dimension_semantics=("arbitrary",), vmem_limit_bytes=VMEM_LIMIT),
        name="mix_router",
    )(x2, a2, b2, *ws)


def _route_plan(route, count, tile_rows):
    T = route.shape[0]
    P = 2 * T
    nt = P // tile_rows
    i32 = jnp.int32
    eid = jnp.arange(N_EXPERTS, dtype=i32)
    counts = count[0, :N_EXPERTS].astype(i32)
    seg_end = jnp.sum(jnp.where(eid[None, :] <= eid[:, None], counts[None, :], 0), axis=1)
    seg_start = seg_end - counts
    picks = route[:, 0:2]
    dest = jnp.sum(jnp.where(picks[:, :, None] == eid, seg_start, 0), axis=-1) + route[:, 2:4]

    tile_start = jnp.arange(nt, dtype=i32) * tile_rows
    slot_of_tile = jnp.arange(nt, dtype=i32) + jnp.sum(seg_start[None, :] < tile_start[:, None], axis=1).astype(i32)
    slot_of_seg = eid + jnp.minimum(seg_start // tile_rows + 1, nt)
    vals = jnp.concatenate([tile_start, seg_start])
    slots = jnp.concatenate([slot_of_tile, slot_of_seg])
    n_items = nt + N_EXPERTS
    bounds = jnp.sum(jnp.where(slots[None, :] == jnp.arange(n_items, dtype=i32)[:, None], vals[None, :], 0), axis=1)
    nxt = jnp.concatenate([bounds[1:], jnp.full((1,), P, i32)])
    tile = jnp.minimum(bounds // tile_rows, nt - 1)
    lo = bounds - tile * tile_rows
    hi = nxt - tile * tile_rows
    expert = jnp.minimum(jnp.sum(seg_end[None, :] <= bounds[:, None], axis=1).astype(i32), N_EXPERTS - 1)
    return dest.astype(i32), (tile, expert, lo, hi)


def _dispatch_kernel(dest_ref, hn_ref, xs_hbm, sem, *, tm):
    def issue(r, carry):
        for pick in range(2):
            pltpu.make_async_copy(_token_tile(hn_ref, r), _token_tile(xs_hbm, dest_ref[0, 2 * r + pick]),
                                  sem).start()
        return carry

    lax.fori_loop(0, tm, issue, 0, unroll=4)
    for _ in range(2):
        pltpu.make_async_copy(hn_ref, xs_hbm.at[pl.ds(0, tm * ROW_TILES)], sem).wait()


def _dispatch_call(dest, hn, tm):
    T = hn.shape[0] // ROW_TILES
    dest3 = dest.reshape(T // tm, 1, 2 * tm)
    return pl.pallas_call(
        functools.partial(_dispatch_kernel, tm=tm),
        grid=(T // tm,),
        in_specs=[pl.BlockSpec((None, 1, 2 * tm), lambda i: (i, 0, 0), memory_space=pltpu.SMEM),
                  pl.BlockSpec((tm * ROW_TILES, LANES), lambda i: (i, 0))],
        out_specs=pl.BlockSpec(memory_space=pl.ANY),
        out_shape=jax.ShapeDtypeStruct((2 * T * ROW_TILES, LANES), F32),
        scratch_shapes=[pltpu.SemaphoreType.DMA],
        compiler_params=pltpu.CompilerParams(dimension_semantics=("arbitrary",)),
        name="moe_dispatch",
    )(dest3, hn)


def _expert_kernel(tile_ref, exp_ref, lo_ref, hi_ref, xs_ref, wg_ref, wu_ref, wd_ref, ys_ref):
    w = pl.program_id(0)
    lo = lo_ref[w]
    hi = hi_ref[w]
    first_visit = (w == 0) | (tile_ref[w] != tile_ref[jnp.maximum(w - 1, 0)])

    @pl.when(first_visit)
    def _():
        ys_ref[...] = jnp.zeros_like(ys_ref)

    @pl.when(hi > lo)
    def _():
        x = _load_row_tiles(xs_ref).astype(BF16)
        g = _dot(x, wg_ref[...])
        h = (g * jax.nn.sigmoid(g) * _dot(x, wu_ref[...])).astype(BF16)
        y = _dot(h, wd_ref[...])
        row = lax.broadcasted_iota(jnp.int32, (y.shape[0], 1), 0)
        _store_row_tiles(ys_ref, jnp.where((row >= lo) & (row < hi), y, _load_row_tiles(ys_ref)))


def _expert_call(plan, xs, packed, tm):
    n_items = plan[0].shape[0]
    by_tile = lambda w, tile, exp, lo, hi: (tile[w], 0)
    by_expert = lambda w, tile, exp, lo, hi: (exp[w], 0, 0)
    grid_spec = pltpu.PrefetchScalarGridSpec(
        num_scalar_prefetch=4,
        grid=(n_items,),
        in_specs=[pl.BlockSpec((tm * ROW_TILES, LANES), by_tile),
                  pl.BlockSpec((None, D_MODEL, EXPERT_FF), by_expert),
                  pl.BlockSpec((None, D_MODEL, EXPERT_FF), by_expert),
                  pl.BlockSpec((None, EXPERT_FF, D_MODEL), by_expert)],
        out_specs=pl.BlockSpec((tm * ROW_TILES, LANES), by_tile),
    )
    return pl.pallas_call(
        _expert_kernel,
        grid_spec=grid_spec,
        out_shape=jax.ShapeDtypeStruct(xs.shape, F32),
        compiler_params=pltpu.CompilerParams(dimension_semantics=("arbitrary",), vmem_limit_bytes=VMEM_LIMIT),
        name="moe_experts",
    )(*plan, xs, packed["wg"], packed["wu"], packed["wd"])


def _combine_kernel(dest_ref, ys_hbm, xn_ref, gate_ref, fin_ref, y_ref, buf, sem, *, tm):
    def issue(r, carry):
        for pick in range(2):
            pltpu.make_async_copy(_token_tile(ys_hbm, dest_ref[0, 2 * r + pick]),
                                  _token_tile(buf.at[pick], r), sem).start()
        return carry

    lax.fori_loop(0, tm, issue, 0, unroll=4)
    for pick in range(2):
        pltpu.make_async_copy(ys_hbm.at[pl.ds(0, tm * ROW_TILES)], buf.at[pick], sem).wait()
    gate = gate_ref[...]
    y = xn_ref[...] + gate[:, 0:1] * _load_row_tiles(buf.at[0]) + gate[:, 1:2] * _load_row_tiles(buf.at[1])
    y_ref[...] = _rms(y, fin_ref[...])


def _combine_call(dest, ys, xn, gate, packed, tm):
    T = xn.shape[0]
    dest3 = dest.reshape(T // tm, 1, 2 * tm)
    row = lambda i: (i, 0)
    return pl.pallas_call(
        functools.partial(_combine_kernel, tm=tm),
        grid=(T // tm,),
        in_specs=[pl.BlockSpec((None, 1, 2 * tm), lambda i: (i, 0, 0), memory_space=pltpu.SMEM),
                  pl.BlockSpec(memory_space=pl.ANY),
                  pl.BlockSpec((tm, D_MODEL), row), pl.BlockSpec((tm, LANES), row),
                  pl.BlockSpec((1, D_MODEL), lambda i: (0, 0))],
        out_specs=pl.BlockSpec((tm, D_MODEL), row),
        out_shape=jax.ShapeDtypeStruct((T, D_MODEL), F32),
        scratch_shapes=[pltpu.VMEM((2, tm * ROW_TILES, LANES), F32), pltpu.SemaphoreType.DMA],
        compiler_params=pltpu.CompilerParams(dimension_semantics=("arbitrary",), vmem_limit_bytes=VMEM_LIMIT),
        name="moe_combine",
    )(dest3, ys, xn, gate, packed["final_norm"])


def _swap_cols(w, block, off, rot):
    K = w.shape[0]
    wb = w.reshape(K, -1, block)
    half = rot // 2
    x1 = wb[:, :, off:off + half]
    x2 = wb[:, :, off + half:off + rot]
    out = jnp.zeros_like(wb)
    out = out.at[:, :, off:off + half].set(-x2).at[:, :, off + half:off + rot].set(x1)
    return out.reshape(w.shape)


def _pad_blocks(w, block, to):
    K = w.shape[0]
    wb = w.reshape(K, -1, block)
    return jnp.pad(wb, ((0, 0), (0, 0), (0, to - block))).reshape(K, -1)


def _pack_weights(attn_norm, w_in, q_norm, w_uq, kv_norm, w_ukv, w_out, ffn_norm, router_group,
                  router_group_bias, router_expert, router_expert_bias, w_gate, w_up, w_down, final_norm):
    l = 0
    wi = w_in[l]
    o_q, o_kv, o_pe = 0, Q_RANK, Q_RANK + KV_RANK
    o_dq = o_pe + MLA_ROPE
    o_dk = o_dq + D_DIFF
    o_dv = o_dk + D_DIFF
    w_pe = wi[:, o_pe:o_dq]
    w_dq = wi[:, o_dq:o_dk]
    w_dk = wi[:, o_dk:o_dv]
    pe_grp = jnp.concatenate([w_pe, _swap_cols(w_pe, MLA_ROPE, 0, MLA_ROPE),
                              jnp.zeros((D_MODEL, LANES - 2 * MLA_ROPE), F32)], axis=1)
    w1 = jnp.concatenate([wi[:, o_q:o_pe], pe_grp,
                          w_dq, _swap_cols(w_dq, DIFF_HEAD_DIM, 0, DIFF_ROT),
                          w_dk, _swap_cols(w_dk, DIFF_HEAD_DIM, 0, DIFF_ROT)], axis=1).astype(BF16)
    wdvt = wi[:, o_dv:].T.astype(BF16)

    qk = MLA_NOPE + MLA_ROPE
    wq = jnp.concatenate([_pad_blocks(w_uq[l], qk, LANES),
                          _pad_blocks(_swap_cols(w_uq[l], qk, MLA_NOPE, MLA_ROPE), qk, LANES)], axis=1).astype(BF16)

    wkv3 = w_ukv[l].reshape(KV_RANK, MLA_HEADS, MLA_NOPE + MLA_V)
    padk = lambda a: jnp.pad(a, ((0, 0), (0, 0), (0, LANES - a.shape[-1]))).reshape(KV_RANK, D_MLA_PAD)
    wvt = padk(wkv3[:, :, MLA_NOPE:]).T.astype(BF16)
    place = np.zeros((LANES, MLA_HEADS, LANES), np.float32)
    for r in range(MLA_ROPE):
        place[r, :, MLA_NOPE + r] = 1.0
        place[MLA_ROPE + r, :, MLA_NOPE + r] = 1.0
    place = jnp.asarray(place.reshape(LANES, D_MLA_PAD))
    wk = jnp.concatenate([padk(wkv3[:, :, :MLA_NOPE]), place, place], axis=0).astype(BF16)

    wo = w_out[l]
    n_mla = MLA_HEADS * MLA_V
    woa = jnp.pad(wo[:n_mla].reshape(MLA_HEADS, MLA_V, D_MODEL),
                  ((0, 0), (0, LANES - MLA_V), (0, 0))).reshape(D_MLA_PAD, D_MODEL).astype(BF16)
    padl = lambda a: jnp.pad(a, ((0, 0), (0, LANES - a.shape[-1])))
    return {
        "attn_norm": attn_norm[l][None], "w1": w1, "q_norm": q_norm[l][None], "wq": wq,
        "kv_norm": kv_norm[l][None], "wk": wk, "wvt": wvt, "wdvt": wdvt,
        "woa": woa, "wob": wo[n_mla:].astype(BF16), "ffn_norm": ffn_norm[l][None],
        "wrg": padl(router_group[l]).astype(BF16), "brg": padl(router_group_bias[l][None]),
        "wre": padl(router_expert[l]).astype(BF16), "bre": padl(router_expert_bias[l][None]),
        "wg": w_gate[l].astype(BF16), "wu": w_up[l].astype(BF16), "wd": w_down[l].astype(BF16),
        "final_norm": final_norm[None],
    }


def _cos_sin(seq, rot):
    half = rot // 2
    inv = ROPE_THETA ** (-jnp.arange(half, dtype=F32) / half)
    ang = jnp.arange(seq, dtype=F32)[:, None] * inv[None, :]
    return jnp.cos(ang), jnp.sin(ang)


def _rope_tables(seq):
    z = lambda n: jnp.zeros((seq, n), F32)
    o = lambda n: jnp.ones((seq, n), F32)
    c, s = _cos_sin(seq, MLA_ROPE)
    qs = (MLA_NOPE + MLA_ROPE) ** -0.5 * LOG2E
    tqc = jnp.concatenate([o(MLA_NOPE), c, c, z(LANES - MLA_NOPE - MLA_ROPE)], axis=1) * qs
    tqs = jnp.concatenate([z(MLA_NOPE), s, s, z(LANES - MLA_NOPE - MLA_ROPE)], axis=1) * qs
    tpe = jnp.concatenate([c, c, s, s, z(LANES - 2 * MLA_ROPE)], axis=1)
    c, s = _cos_sin(seq, DIFF_ROT)
    rest = DIFF_HEAD_DIM - DIFF_ROT
    tdc = jnp.concatenate([c, c, o(rest)] * 2, axis=1)
    tds = jnp.concatenate([s, s, z(rest)] * 2, axis=1)
    ds = DIFF_HEAD_DIM ** -0.5 * LOG2E
    return (tqc, tqs, tpe, tdc * ds, tds * ds, tdc, tds)


def _forward(x, packed, lams, subln):
    B, S, D = x.shape
    T = B * S
    x2 = x.reshape(T, D)
    q, k, vt, dq, dk, dvt = _proj_call(x2, packed, _rope_tables(S), S, tm=512)
    r3 = lambda a: a.reshape(B, S, a.shape[-1])
    a = _mla_call(r3(q), r3(k), vt, tq=512)
    b = _diff_call(lams, subln, r3(dq), r3(dk), dvt, tq=512)
    xn, hn, gate, route, count = _mix_call(x2, a.reshape(T, -1), b.reshape(T, -1), packed, tm=512)
    dest, plan = _route_plan(route, count, tile_rows=256)
    xs = _dispatch_call(dest, hn, tm=512)
    ys = _expert_call(plan, xs, packed, tm=256)
    y = _combine_call(dest, ys, xn, gate, packed, tm=512)
    return y.reshape(B, S, D)


def kernel(x_prompt, x_sample, attn_norm, w_in, q_norm, w_uq, kv_norm, w_ukv, diff_lq1, diff_lk1, diff_lq2,
           diff_lk2, diff_subln, w_out, ffn_norm, router_group, router_group_bias, router_expert,
           router_expert_bias, w_gate, w_up, w_down, final_norm):
    packed = _pack_weights(attn_norm, w_in, q_norm, w_uq, kv_norm, w_ukv, w_out, ffn_norm, router_group,
                           router_group_bias, router_expert, router_expert_bias, w_gate, w_up, w_down,
                           final_norm)
    lams = (diff_lq1, diff_lk1, diff_lq2, diff_lk2)
    subln = diff_subln
    return (_forward(x_prompt, packed, lams, subln), _forward(x_sample, packed, lams, subln))
```

```python
import functools
import math

import numpy as np
import jax
import jax.numpy as jnp
from jax import lax
from jax.experimental import pallas as pl
from jax.experimental.pallas import tpu as pltpu

D_MODEL = 1024
MLA_HEADS = 8
MLA_NOPE = 64
MLA_ROPE = 32
MLA_V = 64
Q_RANK = 256
KV_RANK = 128
DIFF_HEADS = 4
DIFF_HEAD_DIM = 64
DIFF_ROT = DIFF_HEAD_DIM // 4
ROPE_THETA = 500000.0
N_GROUPS = 4
EXPERTS_PER_GROUP = 8
N_EXPERTS = N_GROUPS * EXPERTS_PER_GROUP
EXPERT_FF = 256
EPS = 1e-6
LAMBDA_INIT = 0.8 - 0.6 * math.exp(-0.3 * 0)

LANES = 128
D_DIFF = DIFF_HEADS * 2 * DIFF_HEAD_DIM
D_MLA_PAD = MLA_HEADS * LANES
LOG2E = math.log2(math.e)
VMEM_LIMIT = 56 * 1024 * 1024

BF16 = jnp.bfloat16
F32 = jnp.float32

_C_Q = 0
_C_KV = _C_Q + Q_RANK
_C_PE = _C_KV + KV_RANK
_C_DQ = _C_PE + LANES
_C_DQS = _C_DQ + D_DIFF
_C_DK = _C_DQS + D_DIFF
_C_DKS = _C_DK + D_DIFF
_C_END = _C_DKS + D_DIFF


def _rms(x, g):
    return x * lax.rsqrt(jnp.mean(x * x, axis=-1, keepdims=True) + EPS) * g


def _dot(a, b):
    return jnp.dot(a, b, preferred_element_type=F32)


def _dot_nt(a, b):
    return lax.dot_general(a, b, (((1,), (1,)), ((), ())), preferred_element_type=F32)


ROW_TILES = D_MODEL // LANES
assert ROW_TILES == 8


def _token_tile(ref, t):
    return ref.at[pl.ds(pl.multiple_of(t * ROW_TILES, ROW_TILES), ROW_TILES)]


def _store_row_tiles(ref, x):
    for j in range(ROW_TILES):
        ref[pl.ds(j, x.shape[0], stride=ROW_TILES), :] = x[:, j * LANES:(j + 1) * LANES]


def _load_row_tiles(ref):
    rows = ref.shape[0] // ROW_TILES
    return jnp.concatenate([ref[pl.ds(j, rows, stride=ROW_TILES), :] for j in range(ROW_TILES)], axis=1)


def _proj_kernel(x_ref, an_ref, w1_ref, qn_ref, wq_ref, kvn_ref, wk_ref, wvt_ref, wdvt_ref,
                 tqc_ref, tqs_ref, tpe_ref, tdqc_ref, tdqs_ref, tdkc_ref, tdks_ref,
                 q_ref, k_ref, vt_ref, dq_ref, dk_ref, dvt_ref):
    hn = _rms(x_ref[...], an_ref[...]).astype(BF16)

    def proj(lo, hi):
        return _dot(hn, w1_ref[:, lo:hi])

    cq = _rms(proj(_C_Q, _C_KV), qn_ref[...]).astype(BF16)
    tqc = tqc_ref[...]
    tqs = tqs_ref[...]
    qa = _dot(cq, wq_ref[:, 0:D_MLA_PAD])
    qs = _dot(cq, wq_ref[:, D_MLA_PAD:2 * D_MLA_PAD])
    for h in range(MLA_HEADS):
        blk = slice(h * LANES, (h + 1) * LANES)
        q_ref[:, blk] = (qa[:, blk] * tqc + qs[:, blk] * tqs).astype(BF16)

    kv_pe = proj(_C_KV, _C_DQ)
    ckv = _rms(kv_pe[:, 0:KV_RANK], kvn_ref[...]).astype(BF16)
    pe = kv_pe[:, KV_RANK:] * tpe_ref[...]
    pe_hi = pe.astype(BF16)
    pe_lo = (pe - pe_hi.astype(F32)).astype(BF16)
    k_all = _dot(jnp.concatenate([ckv, pe_hi, pe_lo], axis=1), wk_ref[...])
    k_ref[...] = k_all.astype(BF16)
    vt = _dot_nt(wvt_ref[...], ckv)
    ones_row = (lax.broadcasted_iota(jnp.int32, vt.shape, 0) & (LANES - 1)) == MLA_V
    vt_ref[...] = jnp.where(ones_row, 1.0, vt).astype(BF16)

    tdqc, tdqs, tdkc, tdks = tdqc_ref[...], tdqs_ref[...], tdkc_ref[...], tdks_ref[...]
    dq, dqs = proj(_C_DQ, _C_DQS), proj(_C_DQS, _C_DK)
    dk, dks = proj(_C_DK, _C_DKS), proj(_C_DKS, _C_END)
    for h in range(DIFF_HEADS):
        blk = slice(h * LANES, (h + 1) * LANES)
        dq_ref[:, blk] = (dq[:, blk] * tdqc + dqs[:, blk] * tdqs).astype(BF16)
        dk_ref[:, blk] = (dk[:, blk] * tdkc + dks[:, blk] * tdks).astype(BF16)
    dvt_ref[...] = _dot_nt(wdvt_ref[...], hn).astype(BF16)


def _proj_call(x2, packed, tables, seq, tm):
    T = x2.shape[0]
    nper = seq // tm
    B = T // seq
    row = lambda i: (i, 0)
    const = lambda i: (0, 0)
    tab = lambda i: (i % nper, 0)
    col = lambda i: (i // nper, 0, i % nper)
    full = lambda a: pl.BlockSpec(a.shape, const)
    tspec = pl.BlockSpec((tm, LANES), tab)
    tok = lambda d: (jax.ShapeDtypeStruct((T, d), BF16), pl.BlockSpec((tm, d), row))
    tr = lambda d: (jax.ShapeDtypeStruct((B, d, seq), BF16), pl.BlockSpec((None, d, tm), col))
    outs, ospecs = zip(tok(D_MLA_PAD), tok(D_MLA_PAD), tr(D_MLA_PAD), tok(D_DIFF), tok(D_DIFF), tr(D_DIFF))
    ws = [packed[k] for k in ("attn_norm", "w1", "q_norm", "wq", "kv_norm", "wk", "wvt", "wdvt")]
    return pl.pallas_call(
        _proj_kernel,
        grid=(T // tm,),
        in_specs=[pl.BlockSpec((tm, D_MODEL), row)] + [full(w) for w in ws] + [tspec] * 7,
        out_specs=ospecs,
        out_shape=outs,
        compiler_params=pltpu.CompilerParams(dimension_semantics=("arbitrary",), vmem_limit_bytes=VMEM_LIMIT),
        name="proj",
    )(x2, *ws, *tables)


def _block_index_maps(n_heads, n_qblocks, n_blocks):
    per_batch = n_heads * n_qblocks

    def split(blk):
        b = blk // per_batch
        r = blk % per_batch
        return b, r // n_qblocks, r % n_qblocks

    def rows(blk):
        b, h, i = split(blk)
        return b, i, h

    def keys(blk):
        b, h, _ = split(blk)
        return b, 0, h

    def values_t(blk):
        b, h, _ = split(blk)
        return b, h, 0

    cur = lambda t: jnp.minimum(t, n_blocks - 1)
    prev = lambda t: jnp.maximum(t - 1, 0)
    return (lambda t: rows(cur(t)), lambda t: keys(cur(t)), lambda t: rows(prev(t)), lambda t: values_t(prev(t)))


KEY_CHUNK = 256


def _key_chunk(c):
    return slice(c * KEY_CHUNK, (c + 1) * KEY_CHUNK)


def _running_max(m, st):
    mc = jnp.max(st, axis=0, keepdims=True)
    return mc if m is None else jnp.maximum(m, mc)


def _accumulate(acc, part):
    return part if acc is None else acc + part


def _pipelined_steps(step, zero_scratch):
    t = pl.program_id(0)

    @pl.when(t == 0)
    def _():
        zero_scratch()

    @pl.when(t % 2 == 0)
    def _():
        step(1, 0)

    @pl.when(t % 2 == 1)
    def _():
        step(0, 1)


def _mla_kernel(q_ref, k_ref, vt_ref, o_ref, s0_ref, s1_ref, m0_ref, m1_ref):
    s_refs, m_refs = (s0_ref, s1_ref), (m0_ref, m1_ref)

    def step(src, dst):
        q = q_ref[...]
        m_src = m_refs[src][...]
        scores = lambda c: _dot_nt(k_ref[_key_chunk(c), :], q)
        n_chunks = k_ref.shape[0] // KEY_CHUNK
        ahead, ot, m_run = scores(0), None, None
        for c in range(n_chunks):
            keys = _key_chunk(c)
            st, ahead = ahead, (scores(c + 1) if c + 1 < n_chunks else None)
            s_refs[dst][keys, :] = st
            m_run = _running_max(m_run, st)
            pt = jnp.exp2(s_refs[src][keys, :] - m_src).astype(BF16)
            ot = _accumulate(ot, _dot(vt_ref[:, keys], pt))
        m_refs[dst][...] = m_run
        o_ref[...] = (ot / ot[MLA_V:MLA_V + 1, :]).T.astype(BF16)

    def zero_scratch():
        s1_ref[...] = jnp.zeros_like(s1_ref)
        m1_ref[...] = jnp.zeros_like(m1_ref)

    _pipelined_steps(step, zero_scratch)


def _mla_call(q, k, vt, tq):
    B, S, _ = q.shape
    n_blocks = B * MLA_HEADS * (S // tq)
    q_map, k_map, o_map, vt_map = _block_index_maps(MLA_HEADS, S // tq, n_blocks)
    return pl.pallas_call(
        _mla_kernel,
        grid=(n_blocks + 1,),
        in_specs=[pl.BlockSpec((None, tq, LANES), q_map), pl.BlockSpec((None, S, LANES), k_map),
                  pl.BlockSpec((None, LANES, S), vt_map)],
        out_specs=pl.BlockSpec((None, tq, LANES), o_map),
        out_shape=jax.ShapeDtypeStruct((B, S, D_MLA_PAD), BF16),
        scratch_shapes=[pltpu.VMEM((S, tq), F32)] * 2 + [pltpu.VMEM((1, tq), F32)] * 2,
        compiler_params=pltpu.CompilerParams(dimension_semantics=("arbitrary",), vmem_limit_bytes=VMEM_LIMIT),
        name="mla_attn",
    )(q, k, vt)


def _diff_kernel(lq1_ref, lk1_ref, lq2_ref, lk2_ref, g_ref, q_ref, k_ref, vt_ref, o_ref,
                 sa0_ref, sa1_ref, sb0_ref, sb1_ref, ma0_ref, ma1_ref, mb0_ref, mb1_ref):
    sa_refs, sb_refs = (sa0_ref, sa1_ref), (sb0_ref, sb1_ref)
    ma_refs, mb_refs = (ma0_ref, ma1_ref), (mb0_ref, mb1_ref)

    def step(src, dst):
        lam = (jnp.exp(jnp.sum(lq1_ref[...] * lk1_ref[...], axis=-1, keepdims=True))
               - jnp.exp(jnp.sum(lq2_ref[...] * lk2_ref[...], axis=-1, keepdims=True)) + LAMBDA_INIT)
        q = q_ref[...]
        first = lax.broadcasted_iota(jnp.int32, q.shape, 1) < DIFF_HEAD_DIM
        zero = jnp.zeros_like(q)
        qs = (jnp.where(first, q, zero), jnp.where(first, zero, q))
        s_maps, m_maps = (sa_refs, sb_refs), (ma_refs, mb_refs)
        m_src = [m_refs[src][...] for m_refs in m_maps]
        scores = lambda c: [_dot_nt(k_ref[_key_chunk(c), :], qc) for qc in qs]
        n_chunks = k_ref.shape[0] // KEY_CHUNK
        ahead = scores(0)
        m_run, tot, ot = [None, None], [None, None], [None, None]
        for c in range(n_chunks):
            keys = _key_chunk(c)
            now, ahead = ahead, (scores(c + 1) if c + 1 < n_chunks else None)
            vt = vt_ref[:, keys]
            for i in range(2):
                s_maps[i][dst][keys, :] = now[i]
                m_run[i] = _running_max(m_run[i], now[i])
                e = jnp.exp2(s_maps[i][src][keys, :] - m_src[i])
                tot[i] = _accumulate(tot[i], jnp.sum(e, axis=0, keepdims=True))
                ot[i] = _accumulate(ot[i], _dot(vt, e.astype(BF16)))
        for i in range(2):
            m_maps[i][dst][...] = m_run[i]
        o = ot[0] * (1.0 / tot[0]) - ot[1] * (lam / tot[1])
        o_ref[...] = (_rms(o.T, g_ref[...]) * (1.0 - LAMBDA_INIT)).astype(BF16)

    def zero_scratch():
        for ref in (sa1_ref, sb1_ref, ma1_ref, mb1_ref):
            ref[...] = jnp.zeros_like(ref)

    _pipelined_steps(step, zero_scratch)


def _diff_call(lams, subln, dq, dk, dvt, tq):
    B, S, _ = dq.shape
    n_blocks = B * DIFF_HEADS * (S // tq)
    q_map, k_map, o_map, vt_map = _block_index_maps(DIFF_HEADS, S // tq, n_blocks)
    small = lambda a: pl.BlockSpec(a.shape, lambda t: (0, 0))
    return pl.pallas_call(
        _diff_kernel,
        grid=(n_blocks + 1,),
        in_specs=[small(a) for a in lams] + [small(subln), pl.BlockSpec((None, tq, LANES), q_map),
                                             pl.BlockSpec((None, S, LANES), k_map),
                                             pl.BlockSpec((None, LANES, S), vt_map)],
        out_specs=pl.BlockSpec((None, tq, LANES), o_map),
        out_shape=jax.ShapeDtypeStruct((B, S, D_DIFF), BF16),
        scratch_shapes=[pltpu.VMEM((S, tq), F32)] * 4 + [pltpu.VMEM((1, tq), F32)] * 4,
        compiler_params=pltpu.CompilerParams(dimension_semantics=("arbitrary",), vmem_limit_bytes=VMEM_LIMIT),
        name="diff_attn",
    )(*lams, subln, dq, dk, dvt)


ROUTE_PARTS = 1


def _mix_kernel(x_ref, a_ref, b_ref, woa_ref, wob_ref, fn_ref, wrg_ref, brg_ref, wre_ref, bre_ref,
                xn_ref, hn_ref, gate_ref, route_ref, count_ref, base_ref):
    @pl.when(pl.program_id(0) == 0)
    def _():
        base_ref[...] = jnp.zeros_like(base_ref)

    xn = x_ref[...] + _dot(a_ref[...], woa_ref[...]) + _dot(b_ref[...], wob_ref[...])
    xn_ref[...] = xn
    hn32 = _rms(xn, fn_ref[...])
    _store_row_tiles(hn_ref, hn32)
    hn = hn32.astype(BF16)

    tm = xn.shape[0]
    group_logits = _dot(hn, wrg_ref[...]) + brg_ref[...]
    expert_logits = _dot(hn, wre_ref[...]) + bre_ref[...]

    part_rows = tm // ROUTE_PARTS
    lane = lax.broadcasted_iota(jnp.int32, (part_rows, LANES), 1)
    neg = jnp.float32(-jnp.inf)
    big = jnp.int32(LANES)

    def first_lane_of_max(vals):
        top = jnp.max(vals, axis=-1, keepdims=True)
        return top, jnp.min(jnp.where(vals == top, lane, big), axis=-1, keepdims=True)

    picks = []
    for part in range(ROUTE_PARTS):
        rows = slice(part * part_rows, (part + 1) * part_rows)
        gl = jnp.where(lane < N_GROUPS, group_logits[rows], neg)
        ge = jnp.exp(gl - jnp.max(gl, axis=-1, keepdims=True))
        gp = ge / jnp.sum(ge, axis=-1, keepdims=True)
        p_g, g_idx = first_lane_of_max(gp)

        in_group = (lane < N_EXPERTS) & ((lane & -EXPERTS_PER_GROUP) == g_idx * EXPERTS_PER_GROUP)
        wl = jnp.where(in_group, expert_logits[rows], neg)
        we = jnp.exp(wl - jnp.max(wl, axis=-1, keepdims=True))
        wp = jnp.where(in_group, we / jnp.sum(we, axis=-1, keepdims=True), -1.0)
        w_a, i_a = first_lane_of_max(wp)
        w_b, i_b = first_lane_of_max(jnp.where(lane == i_a, -1.0, wp))
        tot = w_a + w_b
        gate_ref[rows, :] = (jnp.where(lane == 0, p_g * (w_a / tot), 0.0)
                             + jnp.where(lane == 1, p_g * (w_b / tot), 0.0))
        picks.append((i_a, i_b))

    picked = jnp.concatenate([jnp.where((lane == i_a) | (lane == i_b), 1.0, 0.0) for i_a, i_b in picks], axis=0)
    r_id = lax.broadcasted_iota(jnp.int32, (tm, tm), 0)
    c_id = lax.broadcasted_iota(jnp.int32, (tm, tm), 1)
    earlier = jnp.where(r_id > c_id, 1.0, 0.0).astype(BF16)
    before = base_ref[...] + _dot(earlier, picked.astype(BF16))
    zero = jnp.zeros_like(lane)
    for part, (i_a, i_b) in enumerate(picks):
        rows = slice(part * part_rows, (part + 1) * part_rows)
        rank_a = jnp.sum(jnp.where(lane == i_a, before[rows], 0.0), axis=-1, keepdims=True).astype(jnp.int32)
        rank_b = jnp.sum(jnp.where(lane == i_b, before[rows], 0.0), axis=-1, keepdims=True).astype(jnp.int32)
        route_ref[rows, :] = (jnp.where(lane == 0, i_a, zero) + jnp.where(lane == 1, i_b, zero)
                              + jnp.where(lane == 2, rank_a, zero) + jnp.where(lane == 3, rank_b, zero))
    base_ref[...] += jnp.sum(picked, axis=0, keepdims=True)
    count_ref[...] = base_ref[...]


def _mix_call(x2, a2, b2, packed, tm):
    T = x2.shape[0]
    row = lambda i: (i, 0)
    full = lambda a: pl.BlockSpec(a.shape, lambda i: (0, 0))
    ws = [packed[k] for k in ("woa", "wob", "ffn_norm", "wrg", "brg", "wre", "bre")]
    return pl.pallas_call(
        _mix_kernel,
        grid=(T // tm,),
        in_specs=[pl.BlockSpec((tm, D_MODEL), row), pl.BlockSpec((tm, D_MLA_PAD), row),
                  pl.BlockSpec((tm, D_DIFF), row)] + [full(w) for w in ws],
        out_specs=[pl.BlockSpec((tm, D_MODEL), row), pl.BlockSpec((tm * ROW_TILES, LANES), row),
                   pl.BlockSpec((tm, LANES), row), pl.BlockSpec((tm, LANES), row),
                   pl.BlockSpec((1, LANES), lambda i: (0, 0))],
        out_shape=[jax.ShapeDtypeStruct((T, D_MODEL), F32), jax.ShapeDtypeStruct((T * ROW_TILES, LANES), F32),
                   jax.ShapeDtypeStruct((T, LANES), F32), jax.ShapeDtypeStruct((T, LANES), jnp.int32),
                   jax.ShapeDtypeStruct((1, LANES), F32)],
        scratch_shapes=[pltpu.VMEM((1, LANES), F32)],
        compiler_params=pltpu.CompilerParams(dimension_semantics=("arbitrary",), vmem_limit_bytes=VMEM_LIMIT),
        name="mix_router",
    )(x2, a2, b2, *ws)


def _route_plan(route, count, tile_rows):
    T = route.shape[0]
    P = 2 * T
    nt = P // tile_rows
    i32 = jnp.int32
    eid = jnp.arange(N_EXPERTS, dtype=i32)
    counts = count[0, :N_EXPERTS].astype(i32)
    seg_end = jnp.sum(jnp.where(eid[None, :] <= eid[:, None], counts[None, :], 0), axis=1)
    seg_start = seg_end - counts
    picks = route[:, 0:2]
    dest = jnp.sum(jnp.where(picks[:, :, None] == eid, seg_start, 0), axis=-1) + route[:, 2:4]

    tile_start = jnp.arange(nt, dtype=i32) * tile_rows
    slot_of_tile = jnp.arange(nt, dtype=i32) + jnp.sum(seg_start[None, :] < tile_start[:, None], axis=1).astype(i32)
    slot_of_seg = eid + jnp.minimum(seg_start // tile_rows + 1, nt)
    vals = jnp.concatenate([tile_start, seg_start])
    slots = jnp.concatenate([slot_of_tile, slot_of_seg])
    n_items = nt + N_EXPERTS
    bounds = jnp.sum(jnp.where(slots[None, :] == jnp.arange(n_items, dtype=i32)[:, None], vals[None, :], 0), axis=1)
    nxt = jnp.concatenate([bounds[1:], jnp.full((1,), P, i32)])
    tile = jnp.minimum(bounds // tile_rows, nt - 1)
    lo = bounds - tile * tile_rows
    hi = nxt - tile * tile_rows
    expert = jnp.minimum(jnp.sum(seg_end[None, :] <= bounds[:, None], axis=1).astype(i32), N_EXPERTS - 1)
    return dest.astype(i32), (tile, expert, lo, hi)


def _dispatch_kernel(dest_ref, hn_ref, xs_hbm, sem, *, tm):
    def issue(r, carry):
        for pick in range(2):
            pltpu.make_async_copy(_token_tile(hn_ref, r), _token_tile(xs_hbm, dest_ref[0, 2 * r + pick]),
                                  sem).start()
        return carry

    lax.fori_loop(0, tm, issue, 0, unroll=4)
    for _ in range(2):
        pltpu.make_async_copy(hn_ref, xs_hbm.at[pl.ds(0, tm * ROW_TILES)], sem).wait()


def _dispatch_call(dest, hn, tm):
    T = hn.shape[0] // ROW_TILES
    dest3 = dest.reshape(T // tm, 1, 2 * tm)
    return pl.pallas_call(
        functools.partial(_dispatch_kernel, tm=tm),
        grid=(T // tm,),
        in_specs=[pl.BlockSpec((None, 1, 2 * tm), lambda i: (i, 0, 0), memory_space=pltpu.SMEM),
                  pl.BlockSpec((tm * ROW_TILES, LANES), lambda i: (i, 0))],
        out_specs=pl.BlockSpec(memory_space=pl.ANY),
        out_shape=jax.ShapeDtypeStruct((2 * T * ROW_TILES, LANES), F32),
        scratch_shapes=[pltpu.SemaphoreType.DMA],
        compiler_params=pltpu.CompilerParams(dimension_semantics=("arbitrary",)),
        name="moe_dispatch",
    )(dest3, hn)


def _expert_kernel(tile_ref, exp_ref, lo_ref, hi_ref, xs_ref, wg_ref, wu_ref, wd_ref, ys_ref):
    w = pl.program_id(0)
    lo = lo_ref[w]
    hi = hi_ref[w]
    first_visit = (w == 0) | (tile_ref[w] != tile_ref[jnp.maximum(w - 1, 0)])

    @pl.when(first_visit)
    def _():
        ys_ref[...] = jnp.zeros_like(ys_ref)

    @pl.when(hi > lo)
    def _():
        x = _load_row_tiles(xs_ref).astype(BF16)
        g = _dot(x, wg_ref[...])
        h = (g * jax.nn.sigmoid(g) * _dot(x, wu_ref[...])).astype(BF16)
        y = _dot(h, wd_ref[...])
        row = lax.broadcasted_iota(jnp.int32, (y.shape[0], 1), 0)
        _store_row_tiles(ys_ref, jnp.where((row >= lo) & (row < hi), y, _load_row_tiles(ys_ref)))


def _expert_call(plan, xs, packed, tm):
    n_items = plan[0].shape[0]
    by_tile = lambda w, tile, exp, lo, hi: (tile[w], 0)
    by_expert = lambda w, tile, exp, lo, hi: (exp[w], 0, 0)
    grid_spec = pltpu.PrefetchScalarGridSpec(
        num_scalar_prefetch=4,
        grid=(n_items,),
        in_specs=[pl.BlockSpec((tm * ROW_TILES, LANES), by_tile),
                  pl.BlockSpec((None, D_MODEL, EXPERT_FF), by_expert),
                  pl.BlockSpec((None, D_MODEL, EXPERT_FF), by_expert),
                  pl.BlockSpec((None, EXPERT_FF, D_MODEL), by_expert)],
        out_specs=pl.BlockSpec((tm * ROW_TILES, LANES), by_tile),
    )
    return pl.pallas_call(
        _expert_kernel,
        grid_spec=grid_spec,
        out_shape=jax.ShapeDtypeStruct(xs.shape, F32),
        compiler_params=pltpu.CompilerParams(dimension_semantics=("arbitrary",), vmem_limit_bytes=VMEM_LIMIT),
        name="moe_experts",
    )(*plan, xs, packed["wg"], packed["wu"], packed["wd"])


def _combine_kernel(dest_ref, dest_next_ref, ys_hbm, xn_ref, gate_ref, fin_ref, y_ref, buf, sems, *, tm):
    i = pl.program_id(0)

    def start_gather(dests, slot):
        def issue(r, carry):
            for pick in range(2):
                pltpu.make_async_copy(_token_tile(ys_hbm, dests[0, 2 * r + pick]),
                                      _token_tile(buf.at[slot, pick], r), sems.at[slot]).start()
            return carry
        lax.fori_loop(0, tm, issue, 0, unroll=4)

    def finish(slot):
        for pick in range(2):
            pltpu.make_async_copy(ys_hbm.at[pl.ds(0, tm * ROW_TILES)], buf.at[slot, pick], sems.at[slot]).wait()
        gate = gate_ref[...]
        y = (xn_ref[...] + gate[:, 0:1] * _load_row_tiles(buf.at[slot, 0])
             + gate[:, 1:2] * _load_row_tiles(buf.at[slot, 1]))
        y_ref[...] = _rms(y, fin_ref[...])

    @pl.when(i == 0)
    def _():
        start_gather(dest_ref, 0)

    for slot in range(2):
        @pl.when(i % 2 == slot)
        def _(slot=slot):
            @pl.when(i + 1 < pl.num_programs(0))
            def _():
                start_gather(dest_next_ref, 1 - slot)
            finish(slot)


def _combine_call(dest, ys, xn, gate, packed, tm):
    T = xn.shape[0]
    n_tiles = T // tm
    dest3 = dest.reshape(n_tiles, 1, 2 * tm)
    row = lambda i: (i, 0)
    dest_spec = lambda index: pl.BlockSpec((None, 1, 2 * tm), index, memory_space=pltpu.SMEM)
    return pl.pallas_call(
        functools.partial(_combine_kernel, tm=tm),
        grid=(n_tiles,),
        in_specs=[dest_spec(lambda i: (i, 0, 0)), dest_spec(lambda i: (jnp.minimum(i + 1, n_tiles - 1), 0, 0)),
                  pl.BlockSpec(memory_space=pl.ANY),
                  pl.BlockSpec((tm, D_MODEL), row), pl.BlockSpec((tm, LANES), row),
                  pl.BlockSpec((1, D_MODEL), lambda i: (0, 0))],
        out_specs=pl.BlockSpec((tm, D_MODEL), row),
        out_shape=jax.ShapeDtypeStruct((T, D_MODEL), F32),
        scratch_shapes=[pltpu.VMEM((2, 2, tm * ROW_TILES, LANES), F32), pltpu.SemaphoreType.DMA((2,))],
        compiler_params=pltpu.CompilerParams(dimension_semantics=("arbitrary",), vmem_limit_bytes=VMEM_LIMIT),
        name="moe_combine",
    )(dest3, dest3, ys, xn, gate, packed["final_norm"])


def _swap_cols(w, block, off, rot):
    K = w.shape[0]
    wb = w.reshape(K, -1, block)
    half = rot // 2
    x1 = wb[:, :, off:off + half]
    x2 = wb[:, :, off + half:off + rot]
    out = jnp.zeros_like(wb)
    out = out.at[:, :, off:off + half].set(-x2).at[:, :, off + half:off + rot].set(x1)
    return out.reshape(w.shape)


def _pad_blocks(w, block, to):
    K = w.shape[0]
    wb = w.reshape(K, -1, block)
    return jnp.pad(wb, ((0, 0), (0, 0), (0, to - block))).reshape(K, -1)


def _pack_weights(attn_norm, w_in, q_norm, w_uq, kv_norm, w_ukv, w_out, ffn_norm, router_group,
                  router_group_bias, router_expert, router_expert_bias, w_gate, w_up, w_down, final_norm):
    l = 0
    wi = w_in[l]
    o_q, o_kv, o_pe = 0, Q_RANK, Q_RANK + KV_RANK
    o_dq = o_pe + MLA_ROPE
    o_dk = o_dq + D_DIFF
    o_dv = o_dk + D_DIFF
    w_pe = wi[:, o_pe:o_dq]
    w_dq = wi[:, o_dq:o_dk]
    w_dk = wi[:, o_dk:o_dv]
    pe_grp = jnp.concatenate([w_pe, _swap_cols(w_pe, MLA_ROPE, 0, MLA_ROPE),
                              jnp.zeros((D_MODEL, LANES - 2 * MLA_ROPE), F32)], axis=1)
    w1 = jnp.concatenate([wi[:, o_q:o_pe], pe_grp,
                          w_dq, _swap_cols(w_dq, DIFF_HEAD_DIM, 0, DIFF_ROT),
                          w_dk, _swap_cols(w_dk, DIFF_HEAD_DIM, 0, DIFF_ROT)], axis=1).astype(BF16)
    wdvt = wi[:, o_dv:].T.astype(BF16)

    qk = MLA_NOPE + MLA_ROPE
    wq = jnp.concatenate([_pad_blocks(w_uq[l], qk, LANES),
                          _pad_blocks(_swap_cols(w_uq[l], qk, MLA_NOPE, MLA_ROPE), qk, LANES)], axis=1).astype(BF16)

    wkv3 = w_ukv[l].reshape(KV_RANK, MLA_HEADS, MLA_NOPE + MLA_V)
    padk = lambda a: jnp.pad(a, ((0, 0), (0, 0), (0, LANES - a.shape[-1]))).reshape(KV_RANK, D_MLA_PAD)
    wvt = padk(wkv3[:, :, MLA_NOPE:]).T.astype(BF16)
    place = np.zeros((LANES, MLA_HEADS, LANES), np.float32)
    for r in range(MLA_ROPE):
        place[r, :, MLA_NOPE + r] = 1.0
        place[MLA_ROPE + r, :, MLA_NOPE + r] = 1.0
    place = jnp.asarray(place.reshape(LANES, D_MLA_PAD))
    wk = jnp.concatenate([padk(wkv3[:, :, :MLA_NOPE]), place, place], axis=0).astype(BF16)

    wo = w_out[l]
    n_mla = MLA_HEADS * MLA_V
    woa = jnp.pad(wo[:n_mla].reshape(MLA_HEADS, MLA_V, D_MODEL),
                  ((0, 0), (0, LANES - MLA_V), (0, 0))).reshape(D_MLA_PAD, D_MODEL).astype(BF16)
    padl = lambda a: jnp.pad(a, ((0, 0), (0, LANES - a.shape[-1])))
    return {
        "attn_norm": attn_norm[l][None], "w1": w1, "q_norm": q_norm[l][None], "wq": wq,
        "kv_norm": kv_norm[l][None], "wk": wk, "wvt": wvt, "wdvt": wdvt,
        "woa": woa, "wob": wo[n_mla:].astype(BF16), "ffn_norm": ffn_norm[l][None],
        "wrg": padl(router_group[l]).astype(BF16), "brg": padl(router_group_bias[l][None]),
        "wre": padl(router_expert[l]).astype(BF16), "bre": padl(router_expert_bias[l][None]),
        "wg": w_gate[l].astype(BF16), "wu": w_up[l].astype(BF16), "wd": w_down[l].astype(BF16),
        "final_norm": final_norm[None],
    }


def _cos_sin(seq, rot):
    half = rot // 2
    inv = ROPE_THETA ** (-jnp.arange(half, dtype=F32) / half)
    ang = jnp.arange(seq, dtype=F32)[:, None] * inv[None, :]
    return jnp.cos(ang), jnp.sin(ang)


def _rope_tables(seq):
    z = lambda n: jnp.zeros((seq, n), F32)
    o = lambda n: jnp.ones((seq, n), F32)
    c, s = _cos_sin(seq, MLA_ROPE)
    qs = (MLA_NOPE + MLA_ROPE) ** -0.5 * LOG2E
    tqc = jnp.concatenate([o(MLA_NOPE), c, c, z(LANES - MLA_NOPE - MLA_ROPE)], axis=1) * qs
    tqs = jnp.concatenate([z(MLA_NOPE), s, s, z(LANES - MLA_NOPE - MLA_ROPE)], axis=1) * qs
    tpe = jnp.concatenate([c, c, s, s, z(LANES - 2 * MLA_ROPE)], axis=1)
    c, s = _cos_sin(seq, DIFF_ROT)
    rest = DIFF_HEAD_DIM - DIFF_ROT
    tdc = jnp.concatenate([c, c, o(rest)] * 2, axis=1)
    tds = jnp.concatenate([s, s, z(rest)] * 2, axis=1)
    ds = DIFF_HEAD_DIM ** -0.5 * LOG2E
    return (tqc, tqs, tpe, tdc * ds, tds * ds, tdc, tds)


def _forward(x, packed, lams, subln):
    B, S, D = x.shape
    T = B * S
    x2 = x.reshape(T, D)
    q, k, vt, dq, dk, dvt = _proj_call(x2, packed, _rope_tables(S), S, tm=512)
    r3 = lambda a: a.reshape(B, S, a.shape[-1])
    a = _mla_call(r3(q), r3(k), vt, tq=512)
    b = _diff_call(lams, subln, r3(dq), r3(dk), dvt, tq=512)
    xn, hn, gate, route, count = _mix_call(x2, a.reshape(T, -1), b.reshape(T, -1), packed, tm=512)
    dest, plan = _route_plan(route, count, tile_rows=256)
    xs = _dispatch_call(dest, hn, tm=512)
    ys = _expert_call(plan, xs, packed, tm=256)
    y = _combine_call(dest, ys, xn, gate, packed, tm=512)
    return y.reshape(B, S, D)


def kernel(x_prompt, x_sample, attn_norm, w_in, q_norm, w_uq, kv_norm, w_ukv, diff_lq1, diff_lk1, diff_lq2,
           diff_lk2, diff_subln, w_out, ffn_norm, router_group, router_group_bias, router_expert,
           router_expert_bias, w_gate, w_up, w_down, final_norm):
    packed = _pack_weights(attn_norm, w_in, q_norm, w_uq, kv_norm, w_ukv, w_out, ffn_norm, router_group,
                           router_group_bias, router_expert, router_expert_bias, w_gate, w_up, w_down,
                           final_norm)
    lams = (diff_lq1, diff_lk1, diff_lq2, diff_lk2)
    subln = diff_subln
    return (_forward(x_prompt, packed, lams, subln), _forward(x_sample, packed, lams, subln))
```

```python
import functools
import math

import numpy as np
import jax
import jax.numpy as jnp
from jax import lax
from jax.experimental import pallas as pl
from jax.experimental.pallas import tpu as pltpu

D_MODEL = 1024
MLA_HEADS = 8
MLA_NOPE = 64
MLA_ROPE = 32
MLA_V = 64
Q_RANK = 256
KV_RANK = 128
DIFF_HEADS = 4
DIFF_HEAD_DIM = 64
DIFF_ROT = DIFF_HEAD_DIM // 4
ROPE_THETA = 500000.0
N_GROUPS = 4
EXPERTS_PER_GROUP = 8
N_EXPERTS = N_GROUPS * EXPERTS_PER_GROUP
EXPERT_FF = 256
EPS = 1e-6
LAMBDA_INIT = 0.8 - 0.6 * math.exp(-0.3 * 0)

LANES = 128
D_DIFF = DIFF_HEADS * 2 * DIFF_HEAD_DIM
D_MLA_PAD = MLA_HEADS * LANES
LOG2E = math.log2(math.e)
VMEM_LIMIT = 56 * 1024 * 1024

BF16 = jnp.bfloat16
F32 = jnp.float32

_C_Q = 0
_C_KV = _C_Q + Q_RANK
_C_PE = _C_KV + KV_RANK
_C_DQ = _C_PE + LANES
_C_DK = _C_DQ + D_DIFF
_C_END = _C_DK + D_DIFF


def _rms(x, g):
    return x * lax.rsqrt(jnp.mean(x * x, axis=-1, keepdims=True) + EPS) * g


def _dot(a, b):
    return jnp.dot(a, b, preferred_element_type=F32)


def _dot_nt(a, b):
    return lax.dot_general(a, b, (((1,), (1,)), ((), ())), preferred_element_type=F32)


ROW_TILES = D_MODEL // LANES
assert ROW_TILES == 8


def _token_tile(ref, t):
    return ref.at[pl.ds(pl.multiple_of(t * ROW_TILES, ROW_TILES), ROW_TILES)]


def _store_row_tiles(ref, x):
    for j in range(ROW_TILES):
        ref[pl.ds(j, x.shape[0], stride=ROW_TILES), :] = x[:, j * LANES:(j + 1) * LANES]


def _load_row_tiles(ref):
    rows = ref.shape[0] // ROW_TILES
    return jnp.concatenate([ref[pl.ds(j, rows, stride=ROW_TILES), :] for j in range(ROW_TILES)], axis=1)


def _rotary(x, table, half):
    return (x * table[:, 0:LANES] + pltpu.roll(x, half, 1) * table[:, LANES:2 * LANES]
            + pltpu.roll(x, LANES - half, 1) * table[:, 2 * LANES:3 * LANES])


def _proj_kernel(x_ref, an_ref, w1_ref, qn_ref, wq_ref, kvn_ref, wk_ref, wvt_ref, wdvt_ref,
                 tq_ref, tpe_ref, tdq_ref, tdk_ref,
                 q_ref, k_ref, vt_ref, dq_ref, dk_ref, dvt_ref):
    hn = _rms(x_ref[...], an_ref[...]).astype(BF16)

    def proj(lo, hi):
        return _dot(hn, w1_ref[:, lo:hi])

    cq = _rms(proj(_C_Q, _C_KV), qn_ref[...]).astype(BF16)
    tq = tq_ref[...]
    qa = _dot(cq, wq_ref[...])
    for h in range(MLA_HEADS):
        blk = slice(h * LANES, (h + 1) * LANES)
        q_ref[:, blk] = _rotary(qa[:, blk], tq, MLA_ROPE // 2).astype(BF16)

    kv_pe = proj(_C_KV, _C_DQ)
    ckv = _rms(kv_pe[:, 0:KV_RANK], kvn_ref[...]).astype(BF16)
    pe = _rotary(kv_pe[:, KV_RANK:], tpe_ref[...], MLA_ROPE // 2)
    pe_hi = pe.astype(BF16)
    pe_lo = (pe - pe_hi.astype(F32)).astype(BF16)
    k_all = _dot(jnp.concatenate([ckv, pe_hi, pe_lo], axis=1), wk_ref[...])
    k_ref[...] = k_all.astype(BF16)
    vt = _dot_nt(wvt_ref[...], ckv)
    ones_row = (lax.broadcasted_iota(jnp.int32, vt.shape, 0) & (LANES - 1)) == MLA_V
    vt_ref[...] = jnp.where(ones_row, 1.0, vt).astype(BF16)

    tdq, tdk = tdq_ref[...], tdk_ref[...]
    dq, dk = proj(_C_DQ, _C_DK), proj(_C_DK, _C_END)
    for h in range(DIFF_HEADS):
        blk = slice(h * LANES, (h + 1) * LANES)
        dq_ref[:, blk] = _rotary(dq[:, blk], tdq, DIFF_ROT // 2).astype(BF16)
        dk_ref[:, blk] = _rotary(dk[:, blk], tdk, DIFF_ROT // 2).astype(BF16)
    dvt_ref[...] = _dot_nt(wdvt_ref[...], hn).astype(BF16)


def _proj_call(x2, packed, tables, seq, tm):
    T = x2.shape[0]
    nper = seq // tm
    B = T // seq
    row = lambda i: (i, 0)
    const = lambda i: (0, 0)
    tab = lambda i: (i % nper, 0)
    col = lambda i: (i // nper, 0, i % nper)
    full = lambda a: pl.BlockSpec(a.shape, const)
    tspec = pl.BlockSpec((tm, 3 * LANES), tab)
    tok = lambda d: (jax.ShapeDtypeStruct((T, d), BF16), pl.BlockSpec((tm, d), row))
    tr = lambda d: (jax.ShapeDtypeStruct((B, d, seq), BF16), pl.BlockSpec((None, d, tm), col))
    outs, ospecs = zip(tok(D_MLA_PAD), tok(D_MLA_PAD), tr(D_MLA_PAD), tok(D_DIFF), tok(D_DIFF), tr(D_DIFF))
    ws = [packed[k] for k in ("attn_norm", "w1", "q_norm", "wq", "kv_norm", "wk", "wvt", "wdvt")]
    return pl.pallas_call(
        _proj_kernel,
        grid=(T // tm,),
        in_specs=[pl.BlockSpec((tm, D_MODEL), row)] + [full(w) for w in ws] + [tspec] * len(tables),
        out_specs=ospecs,
        out_shape=outs,
        compiler_params=pltpu.CompilerParams(dimension_semantics=("arbitrary",), vmem_limit_bytes=VMEM_LIMIT),
        name="proj",
    )(x2, *ws, *tables)


def _block_index_maps(n_heads, n_qblocks, n_blocks):
    per_batch = n_heads * n_qblocks

    def split(blk):
        b = blk // per_batch
        r = blk % per_batch
        return b, r // n_qblocks, r % n_qblocks

    def rows(blk):
        b, h, i = split(blk)
        return b, i, h

    def keys(blk):
        b, h, _ = split(blk)
        return b, 0, h

    def values_t(blk):
        b, h, _ = split(blk)
        return b, h, 0

    cur = lambda t: jnp.minimum(t, n_blocks - 1)
    prev = lambda t: jnp.maximum(t - 1, 0)
    return (lambda t: rows(cur(t)), lambda t: keys(cur(t)), lambda t: rows(prev(t)), lambda t: values_t(prev(t)))


KEY_CHUNK = 256


def _key_chunk(c):
    return slice(c * KEY_CHUNK, (c + 1) * KEY_CHUNK)


def _running_max(m, st):
    mc = jnp.max(st, axis=0, keepdims=True)
    return mc if m is None else jnp.maximum(m, mc)


def _accumulate(acc, part):
    return part if acc is None else acc + part


def _pipelined_steps(step, zero_scratch):
    t = pl.program_id(0)

    @pl.when(t == 0)
    def _():
        zero_scratch()

    @pl.when(t % 2 == 0)
    def _():
        step(1, 0)

    @pl.when(t % 2 == 1)
    def _():
        step(0, 1)


def _mla_kernel(q_ref, k_ref, vt_ref, o_ref, s0_ref, s1_ref, m0_ref, m1_ref):
    s_refs, m_refs = (s0_ref, s1_ref), (m0_ref, m1_ref)

    def step(src, dst):
        q = q_ref[...]
        m_src = m_refs[src][...]
        scores = lambda c: _dot_nt(k_ref[_key_chunk(c), :], q)
        n_chunks = k_ref.shape[0] // KEY_CHUNK
        ahead, ot, m_run = scores(0), None, None
        for c in range(n_chunks):
            keys = _key_chunk(c)
            st, ahead = ahead, (scores(c + 1) if c + 1 < n_chunks else None)
            s_refs[dst][keys, :] = st
            m_run = _running_max(m_run, st)
            pt = jnp.exp2(s_refs[src][keys, :] - m_src).astype(BF16)
            ot = _accumulate(ot, _dot(vt_ref[:, keys], pt))
        m_refs[dst][...] = m_run
        o_ref[...] = (ot / ot[MLA_V:MLA_V + 1, :]).T.astype(BF16)

    def zero_scratch():
        s1_ref[...] = jnp.zeros_like(s1_ref)
        m1_ref[...] = jnp.zeros_like(m1_ref)

    _pipelined_steps(step, zero_scratch)


def _mla_call(q, k, vt, tq):
    B, S, _ = q.shape
    n_blocks = B * MLA_HEADS * (S // tq)
    q_map, k_map, o_map, vt_map = _block_index_maps(MLA_HEADS, S // tq, n_blocks)
    return pl.pallas_call(
        _mla_kernel,
        grid=(n_blocks + 1,),
        in_specs=[pl.BlockSpec((None, tq, LANES), q_map), pl.BlockSpec((None, S, LANES), k_map),
                  pl.BlockSpec((None, LANES, S), vt_map)],
        out_specs=pl.BlockSpec((None, tq, LANES), o_map),
        out_shape=jax.ShapeDtypeStruct((B, S, D_MLA_PAD), BF16),
        scratch_shapes=[pltpu.VMEM((S, tq), F32)] * 2 + [pltpu.VMEM((1, tq), F32)] * 2,
        compiler_params=pltpu.CompilerParams(dimension_semantics=("arbitrary",), vmem_limit_bytes=VMEM_LIMIT),
        name="mla_attn",
    )(q, k, vt)


def _diff_kernel(lq1_ref, lk1_ref, lq2_ref, lk2_ref, g_ref, q_ref, k_ref, vt_ref, o_ref,
                 sa0_ref, sa1_ref, sb0_ref, sb1_ref, ma0_ref, ma1_ref, mb0_ref, mb1_ref):
    sa_refs, sb_refs = (sa0_ref, sa1_ref), (sb0_ref, sb1_ref)
    ma_refs, mb_refs = (ma0_ref, ma1_ref), (mb0_ref, mb1_ref)

    def step(src, dst):
        lam = (jnp.exp(jnp.sum(lq1_ref[...] * lk1_ref[...], axis=-1, keepdims=True))
               - jnp.exp(jnp.sum(lq2_ref[...] * lk2_ref[...], axis=-1, keepdims=True)) + LAMBDA_INIT)
        q = q_ref[...]
        first = lax.broadcasted_iota(jnp.int32, q.shape, 1) < DIFF_HEAD_DIM
        zero = jnp.zeros_like(q)
        qs = (jnp.where(first, q, zero), jnp.where(first, zero, q))
        s_maps, m_maps = (sa_refs, sb_refs), (ma_refs, mb_refs)
        m_src = [m_refs[src][...] for m_refs in m_maps]
        scores = lambda c: [_dot_nt(k_ref[_key_chunk(c), :], qc) for qc in qs]
        n_chunks = k_ref.shape[0] // KEY_CHUNK
        ahead = scores(0)
        m_run, tot, ot = [None, None], [None, None], [None, None]
        for c in range(n_chunks):
            keys = _key_chunk(c)
            now, ahead = ahead, (scores(c + 1) if c + 1 < n_chunks else None)
            vt = vt_ref[:, keys]
            for i in range(2):
                s_maps[i][dst][keys, :] = now[i]
                m_run[i] = _running_max(m_run[i], now[i])
                e = jnp.exp2(s_maps[i][src][keys, :] - m_src[i])
                tot[i] = _accumulate(tot[i], jnp.sum(e, axis=0, keepdims=True))
                ot[i] = _accumulate(ot[i], _dot(vt, e.astype(BF16)))
        for i in range(2):
            m_maps[i][dst][...] = m_run[i]
        o = ot[0] * (1.0 / tot[0]) - ot[1] * (lam / tot[1])
        o_ref[...] = (_rms(o.T, g_ref[...]) * (1.0 - LAMBDA_INIT)).astype(BF16)

    def zero_scratch():
        for ref in (sa1_ref, sb1_ref, ma1_ref, mb1_ref):
            ref[...] = jnp.zeros_like(ref)

    _pipelined_steps(step, zero_scratch)


def _diff_call(lams, subln, dq, dk, dvt, tq):
    B, S, _ = dq.shape
    n_blocks = B * DIFF_HEADS * (S // tq)
    q_map, k_map, o_map, vt_map = _block_index_maps(DIFF_HEADS, S // tq, n_blocks)
    small = lambda a: pl.BlockSpec(a.shape, lambda t: (0, 0))
    return pl.pallas_call(
        _diff_kernel,
        grid=(n_blocks + 1,),
        in_specs=[small(a) for a in lams] + [small(subln), pl.BlockSpec((None, tq, LANES), q_map),
                                             pl.BlockSpec((None, S, LANES), k_map),
                                             pl.BlockSpec((None, LANES, S), vt_map)],
        out_specs=pl.BlockSpec((None, tq, LANES), o_map),
        out_shape=jax.ShapeDtypeStruct((B, S, D_DIFF), BF16),
        scratch_shapes=[pltpu.VMEM((S, tq), F32)] * 4 + [pltpu.VMEM((1, tq), F32)] * 4,
        compiler_params=pltpu.CompilerParams(dimension_semantics=("arbitrary",), vmem_limit_bytes=VMEM_LIMIT),
        name="diff_attn",
    )(*lams, subln, dq, dk, dvt)


def _mix_kernel(x_ref, a_ref, b_ref, woa_ref, wob_ref, fn_ref, wrg_ref, brg_ref, wre_ref, bre_ref, earlier_ref,
                xn_ref, hn_ref, gate_ref, route_ref, count_ref, base_ref, lg0_ref, lg1_ref, le0_ref, le1_ref):
    t = pl.program_id(0)
    group_refs, expert_refs = (lg0_ref, lg1_ref), (le0_ref, le1_ref)

    def step(src, dst):
        xn = x_ref[...] + _dot(a_ref[...], woa_ref[...]) + _dot(b_ref[...], wob_ref[...])
        xn_ref[...] = xn
        hn32 = _rms(xn, fn_ref[...])
        _store_row_tiles(hn_ref, hn32)
        hn = hn32.astype(BF16)
        group_refs[dst][...] = _dot(hn, wrg_ref[...]) + brg_ref[...]
        expert_refs[dst][...] = _dot(hn, wre_ref[...]) + bre_ref[...]

        lane = lax.broadcasted_iota(jnp.int32, (xn.shape[0], LANES), 1)
        neg = jnp.float32(-jnp.inf)
        big = jnp.int32(LANES)

        def first_lane_of_max(vals):
            top = jnp.max(vals, axis=-1, keepdims=True)
            return top, jnp.min(jnp.where(vals == top, lane, big), axis=-1, keepdims=True)

        gl = jnp.where(lane < N_GROUPS, group_refs[src][...], neg)
        ge = jnp.exp(gl - jnp.max(gl, axis=-1, keepdims=True))
        gp = ge / jnp.sum(ge, axis=-1, keepdims=True)
        p_g, g_idx = first_lane_of_max(gp)

        in_group = (lane < N_EXPERTS) & ((lane & -EXPERTS_PER_GROUP) == g_idx * EXPERTS_PER_GROUP)
        wl = jnp.where(in_group, expert_refs[src][...], neg)
        we = jnp.exp(wl - jnp.max(wl, axis=-1, keepdims=True))
        wp = jnp.where(in_group, we / jnp.sum(we, axis=-1, keepdims=True), -1.0)
        w_a, i_a = first_lane_of_max(wp)
        w_b, i_b = first_lane_of_max(jnp.where(lane == i_a, -1.0, wp))
        tot = w_a + w_b
        gate_ref[...] = (jnp.where(lane == 0, p_g * (w_a / tot), 0.0)
                         + jnp.where(lane == 1, p_g * (w_b / tot), 0.0))

        picked = jnp.where((lane == i_a) | (lane == i_b), 1.0, 0.0)
        before = base_ref[...] + _dot(earlier_ref[...], picked.astype(BF16))
        rank_a = jnp.sum(jnp.where(lane == i_a, before, 0.0), axis=-1, keepdims=True).astype(jnp.int32)
        rank_b = jnp.sum(jnp.where(lane == i_b, before, 0.0), axis=-1, keepdims=True).astype(jnp.int32)
        zero = jnp.zeros_like(lane)
        route_ref[...] = (jnp.where(lane == 0, i_a, zero) + jnp.where(lane == 1, i_b, zero)
                          + jnp.where(lane == 2, rank_a, zero) + jnp.where(lane == 3, rank_b, zero))
        base_ref[...] += jnp.where(t > 0, jnp.sum(picked, axis=0, keepdims=True), 0.0)
        count_ref[...] = base_ref[...]

    def zero_scratch():
        for ref in (base_ref, lg1_ref, le1_ref):
            ref[...] = jnp.zeros_like(ref)

    _pipelined_steps(step, zero_scratch)


def _mix_call(x2, a2, b2, packed, tm):
    T = x2.shape[0]
    n_tiles = T // tm
    cur = lambda t: (jnp.minimum(t, n_tiles - 1), 0)
    prev = lambda t: (jnp.maximum(t - 1, 0), 0)
    full = lambda a: pl.BlockSpec(a.shape, lambda t: (0, 0))
    ws = [packed[k] for k in ("woa", "wob", "ffn_norm", "wrg", "brg", "wre", "bre")]
    ws.append(jnp.asarray(np.tril(np.ones((tm, tm), np.float32), -1), BF16))
    return pl.pallas_call(
        _mix_kernel,
        grid=(n_tiles + 1,),
        in_specs=[pl.BlockSpec((tm, D_MODEL), cur), pl.BlockSpec((tm, D_MLA_PAD), cur),
                  pl.BlockSpec((tm, D_DIFF), cur)] + [full(w) for w in ws],
        out_specs=[pl.BlockSpec((tm, D_MODEL), cur), pl.BlockSpec((tm * ROW_TILES, LANES), cur),
                   pl.BlockSpec((tm, LANES), prev), pl.BlockSpec((tm, LANES), prev),
                   pl.BlockSpec((1, LANES), lambda t: (0, 0))],
        out_shape=[jax.ShapeDtypeStruct((T, D_MODEL), F32), jax.ShapeDtypeStruct((T * ROW_TILES, LANES), F32),
                   jax.ShapeDtypeStruct((T, LANES), F32), jax.ShapeDtypeStruct((T, LANES), jnp.int32),
                   jax.ShapeDtypeStruct((1, LANES), F32)],
        scratch_shapes=[pltpu.VMEM((1, LANES), F32)] + [pltpu.VMEM((tm, LANES), F32)] * 4,
        compiler_params=pltpu.CompilerParams(dimension_semantics=("arbitrary",), vmem_limit_bytes=VMEM_LIMIT),
        name="mix_router",
    )(x2, a2, b2, *ws)


def _route_plan(route, count, tile_rows):
    T = route.shape[0]
    P = 2 * T
    nt = P // tile_rows
    i32 = jnp.int32
    eid = jnp.arange(N_EXPERTS, dtype=i32)
    counts = count[0, :N_EXPERTS].astype(i32)
    seg_end = jnp.sum(jnp.where(eid[None, :] <= eid[:, None], counts[None, :], 0), axis=1)
    seg_start = seg_end - counts
    picks = route[:, 0:2]
    dest = jnp.sum(jnp.where(picks[:, :, None] == eid, seg_start, 0), axis=-1) + route[:, 2:4]

    tile_start = jnp.arange(nt, dtype=i32) * tile_rows
    slot_of_tile = jnp.arange(nt, dtype=i32) + jnp.sum(seg_start[None, :] < tile_start[:, None], axis=1).astype(i32)
    slot_of_seg = eid + jnp.minimum(seg_start // tile_rows + 1, nt)
    vals = jnp.concatenate([tile_start, seg_start])
    slots = jnp.concatenate([slot_of_tile, slot_of_seg])
    n_items = nt + N_EXPERTS
    bounds = jnp.sum(jnp.where(slots[None, :] == jnp.arange(n_items, dtype=i32)[:, None], vals[None, :], 0), axis=1)
    nxt = jnp.concatenate([bounds[1:], jnp.full((1,), P, i32)])
    tile = jnp.minimum(bounds // tile_rows, nt - 1)
    lo = bounds - tile * tile_rows
    hi = nxt - tile * tile_rows
    expert = jnp.minimum(jnp.sum(seg_end[None, :] <= bounds[:, None], axis=1).astype(i32), N_EXPERTS - 1)
    return dest.astype(i32), (tile, expert, lo, hi)


def _dispatch_kernel(dest_ref, hn_ref, xs_hbm, sem, *, tm):
    def issue(r, carry):
        for pick in range(2):
            pltpu.make_async_copy(_token_tile(hn_ref, r), _token_tile(xs_hbm, dest_ref[0, 2 * r + pick]),
                                  sem).start()
        return carry

    lax.fori_loop(0, tm, issue, 0, unroll=4)
    for _ in range(2):
        pltpu.make_async_copy(hn_ref, xs_hbm.at[pl.ds(0, tm * ROW_TILES)], sem).wait()


def _dispatch_call(dest, hn, tm):
    T = hn.shape[0] // ROW_TILES
    dest3 = dest.reshape(T // tm, 1, 2 * tm)
    return pl.pallas_call(
        functools.partial(_dispatch_kernel, tm=tm),
        grid=(T // tm,),
        in_specs=[pl.BlockSpec((None, 1, 2 * tm), lambda i: (i, 0, 0), memory_space=pltpu.SMEM),
                  pl.BlockSpec((tm * ROW_TILES, LANES), lambda i: (i, 0))],
        out_specs=pl.BlockSpec(memory_space=pl.ANY),
        out_shape=jax.ShapeDtypeStruct((2 * T * ROW_TILES, LANES), F32),
        scratch_shapes=[pltpu.SemaphoreType.DMA],
        compiler_params=pltpu.CompilerParams(dimension_semantics=("arbitrary",)),
        name="moe_dispatch",
    )(dest3, hn)


def _expert_kernel(tile_ref, exp_ref, lo_ref, hi_ref, xs_ref, wg_ref, wu_ref, wd_ref, ys_ref):
    w = pl.program_id(0)
    lo = lo_ref[w]
    hi = hi_ref[w]
    first_visit = (w == 0) | (tile_ref[w] != tile_ref[jnp.maximum(w - 1, 0)])

    @pl.when(first_visit)
    def _():
        ys_ref[...] = jnp.zeros_like(ys_ref)

    @pl.when(hi > lo)
    def _():
        x = _load_row_tiles(xs_ref).astype(BF16)
        g = _dot(x, wg_ref[...])
        h = (g * jax.nn.sigmoid(g) * _dot(x, wu_ref[...])).astype(BF16)
        y = _dot(h, wd_ref[...])
        row = lax.broadcasted_iota(jnp.int32, (y.shape[0], 1), 0)
        _store_row_tiles(ys_ref, jnp.where((row >= lo) & (row < hi), y, _load_row_tiles(ys_ref)))


def _expert_call(plan, xs, packed, tm):
    n_items = plan[0].shape[0]
    by_tile = lambda w, tile, exp, lo, hi: (tile[w], 0)
    by_expert = lambda w, tile, exp, lo, hi: (exp[w], 0, 0)
    grid_spec = pltpu.PrefetchScalarGridSpec(
        num_scalar_prefetch=4,
        grid=(n_items,),
        in_specs=[pl.BlockSpec((tm * ROW_TILES, LANES), by_tile),
                  pl.BlockSpec((None, D_MODEL, EXPERT_FF), by_expert),
                  pl.BlockSpec((None, D_MODEL, EXPERT_FF), by_expert),
                  pl.BlockSpec((None, EXPERT_FF, D_MODEL), by_expert)],
        out_specs=pl.BlockSpec((tm * ROW_TILES, LANES), by_tile),
    )
    return pl.pallas_call(
        _expert_kernel,
        grid_spec=grid_spec,
        out_shape=jax.ShapeDtypeStruct(xs.shape, F32),
        compiler_params=pltpu.CompilerParams(dimension_semantics=("arbitrary",), vmem_limit_bytes=VMEM_LIMIT),
        name="moe_experts",
    )(*plan, xs, packed["wg"], packed["wu"], packed["wd"])


def _combine_kernel(dest_ref, dest_next_ref, ys_hbm, xn_ref, gate_ref, fin_ref, y_ref, buf, sems, *, tm):
    i = pl.program_id(0)

    def start_gather(dests, slot):
        def issue(r, carry):
            for pick in range(2):
                pltpu.make_async_copy(_token_tile(ys_hbm, dests[0, 2 * r + pick]),
                                      _token_tile(buf.at[slot, pick], r), sems.at[slot]).start()
            return carry
        lax.fori_loop(0, tm, issue, 0, unroll=4)

    def finish(slot):
        for pick in range(2):
            pltpu.make_async_copy(ys_hbm.at[pl.ds(0, tm * ROW_TILES)], buf.at[slot, pick], sems.at[slot]).wait()
        gate = gate_ref[...]
        y = (xn_ref[...] + gate[:, 0:1] * _load_row_tiles(buf.at[slot, 0])
             + gate[:, 1:2] * _load_row_tiles(buf.at[slot, 1]))
        y_ref[...] = _rms(y, fin_ref[...])

    @pl.when(i == 0)
    def _():
        start_gather(dest_ref, 0)

    for slot in range(2):
        @pl.when(i % 2 == slot)
        def _(slot=slot):
            @pl.when(i + 1 < pl.num_programs(0))
            def _():
                start_gather(dest_next_ref, 1 - slot)
            finish(slot)


def _combine_call(dest, ys, xn, gate, packed, tm):
    T = xn.shape[0]
    n_tiles = T // tm
    dest3 = dest.reshape(n_tiles, 1, 2 * tm)
    row = lambda i: (i, 0)
    dest_spec = lambda index: pl.BlockSpec((None, 1, 2 * tm), index, memory_space=pltpu.SMEM)
    return pl.pallas_call(
        functools.partial(_combine_kernel, tm=tm),
        grid=(n_tiles,),
        in_specs=[dest_spec(lambda i: (i, 0, 0)), dest_spec(lambda i: (jnp.minimum(i + 1, n_tiles - 1), 0, 0)),
                  pl.BlockSpec(memory_space=pl.ANY),
                  pl.BlockSpec((tm, D_MODEL), row), pl.BlockSpec((tm, LANES), row),
                  pl.BlockSpec((1, D_MODEL), lambda i: (0, 0))],
        out_specs=pl.BlockSpec((tm, D_MODEL), row),
        out_shape=jax.ShapeDtypeStruct((T, D_MODEL), F32),
        scratch_shapes=[pltpu.VMEM((2, 2, tm * ROW_TILES, LANES), F32), pltpu.SemaphoreType.DMA((2,))],
        compiler_params=pltpu.CompilerParams(dimension_semantics=("arbitrary",), vmem_limit_bytes=VMEM_LIMIT),
        name="moe_combine",
    )(dest3, dest3, ys, xn, gate, packed["final_norm"])


def _pad_blocks(w, block, to):
    K = w.shape[0]
    wb = w.reshape(K, -1, block)
    return jnp.pad(wb, ((0, 0), (0, 0), (0, to - block))).reshape(K, -1)


def _pack_weights(attn_norm, w_in, q_norm, w_uq, kv_norm, w_ukv, w_out, ffn_norm, router_group,
                  router_group_bias, router_expert, router_expert_bias, w_gate, w_up, w_down, final_norm):
    l = 0
    wi = w_in[l]
    o_q, o_kv, o_pe = 0, Q_RANK, Q_RANK + KV_RANK
    o_dq = o_pe + MLA_ROPE
    o_dk = o_dq + D_DIFF
    o_dv = o_dk + D_DIFF
    w1 = jnp.concatenate([wi[:, o_q:o_dq], jnp.zeros((D_MODEL, LANES - MLA_ROPE), F32), wi[:, o_dq:o_dv]],
                         axis=1).astype(BF16)
    wdvt = wi[:, o_dv:].T.astype(BF16)
    wq = _pad_blocks(w_uq[l], MLA_NOPE + MLA_ROPE, LANES).astype(BF16)

    wkv3 = w_ukv[l].reshape(KV_RANK, MLA_HEADS, MLA_NOPE + MLA_V)
    padk = lambda a: jnp.pad(a, ((0, 0), (0, 0), (0, LANES - a.shape[-1]))).reshape(KV_RANK, D_MLA_PAD)
    wvt = padk(wkv3[:, :, MLA_NOPE:]).T.astype(BF16)
    place = np.zeros((LANES, MLA_HEADS, LANES), np.float32)
    for r in range(MLA_ROPE):
        place[r, :, MLA_NOPE + r] = 1.0
    place = jnp.asarray(place.reshape(LANES, D_MLA_PAD))
    wk = jnp.concatenate([padk(wkv3[:, :, :MLA_NOPE]), place, place], axis=0).astype(BF16)

    wo = w_out[l]
    n_mla = MLA_HEADS * MLA_V
    woa = jnp.pad(wo[:n_mla].reshape(MLA_HEADS, MLA_V, D_MODEL),
                  ((0, 0), (0, LANES - MLA_V), (0, 0))).reshape(D_MLA_PAD, D_MODEL).astype(BF16)
    padl = lambda a: jnp.pad(a, ((0, 0), (0, LANES - a.shape[-1])))
    return {
        "attn_norm": attn_norm[l][None], "w1": w1, "q_norm": q_norm[l][None], "wq": wq,
        "kv_norm": kv_norm[l][None], "wk": wk, "wvt": wvt, "wdvt": wdvt,
        "woa": woa, "wob": wo[n_mla:].astype(BF16), "ffn_norm": ffn_norm[l][None],
        "wrg": padl(router_group[l]).astype(BF16), "brg": padl(router_group_bias[l][None]),
        "wre": padl(router_expert[l]).astype(BF16), "bre": padl(router_expert_bias[l][None]),
        "wg": w_gate[l].astype(BF16), "wu": w_up[l].astype(BF16), "wd": w_down[l].astype(BF16),
        "final_norm": final_norm[None],
    }


def _cos_sin(seq, rot):
    half = rot // 2
    inv = ROPE_THETA ** (-jnp.arange(half, dtype=F32) / half)
    ang = jnp.arange(seq, dtype=F32)[:, None] * inv[None, :]
    return jnp.cos(ang), jnp.sin(ang)


def _rope_tables(seq):
    z = lambda n: jnp.zeros((seq, n), F32)
    o = lambda n: jnp.ones((seq, n), F32)

    def table(blocks, scale=1.0):
        cos_t = jnp.concatenate([p for b, c, s, a in blocks for p in (o(b), c, c, o(a))], axis=1)
        up = jnp.concatenate([p for b, c, s, a in blocks for p in (z(b), z(s.shape[1]), s, z(a))], axis=1)
        down = jnp.concatenate([p for b, c, s, a in blocks for p in (z(b), -s, z(s.shape[1]), z(a))], axis=1)
        return jnp.concatenate([cos_t, up, down], axis=1) * scale

    c, s = _cos_sin(seq, MLA_ROPE)
    tq = table([(MLA_NOPE, c, s, LANES - MLA_NOPE - MLA_ROPE)], (MLA_NOPE + MLA_ROPE) ** -0.5 * LOG2E)
    tpe = table([(0, c, s, LANES - MLA_ROPE)])
    c, s = _cos_sin(seq, DIFF_ROT)
    head = (0, c, s, DIFF_HEAD_DIM - DIFF_ROT)
    tdk = table([head, head])
    return (tq, tpe, tdk * (DIFF_HEAD_DIM ** -0.5 * LOG2E), tdk)


def _forward(x, packed, tables, lams, subln):
    B, S, D = x.shape
    T = B * S
    x2 = x.reshape(T, D)
    q, k, vt, dq, dk, dvt = _proj_call(x2, packed, tables, S, tm=512)
    r3 = lambda a: a.reshape(B, S, a.shape[-1])
    a = _mla_call(r3(q), r3(k), vt, tq=512)
    b = _diff_call(lams, subln, r3(dq), r3(dk), dvt, tq=512)
    xn, hn, gate, route, count = _mix_call(x2, a.reshape(T, -1), b.reshape(T, -1), packed, tm=512)
    dest, plan = _route_plan(route, count, tile_rows=256)
    xs = _dispatch_call(dest, hn, tm=512)
    ys = _expert_call(plan, xs, packed, tm=256)
    y = _combine_call(dest, ys, xn, gate, packed, tm=512)
    return y.reshape(B, S, D)


def kernel(x_prompt, x_sample, attn_norm, w_in, q_norm, w_uq, kv_norm, w_ukv, diff_lq1, diff_lk1, diff_lq2,
           diff_lk2, diff_subln, w_out, ffn_norm, router_group, router_group_bias, router_expert,
           router_expert_bias, w_gate, w_up, w_down, final_norm):
    packed = _pack_weights(attn_norm, w_in, q_norm, w_uq, kv_norm, w_ukv, w_out, ffn_norm, router_group,
                           router_group_bias, router_expert, router_expert_bias, w_gate, w_up, w_down,
                           final_norm)
    lams = (diff_lq1, diff_lk1, diff_lq2, diff_lk2)
    subln = diff_subln
    tables = _rope_tables(max(x_prompt.shape[1], x_sample.shape[1]))
    return (_forward(x_prompt, packed, tables, lams, subln), _forward(x_sample, packed, tables, lams, subln))
```

```python
import functools
import math

import numpy as np
import jax
import jax.numpy as jnp
from jax import lax
from jax.experimental import pallas as pl
from jax.experimental.pallas import tpu as pltpu

D_MODEL = 1024
MLA_HEADS = 8
MLA_NOPE = 64
MLA_ROPE = 32
MLA_V = 64
Q_RANK = 256
KV_RANK = 128
DIFF_HEADS = 4
DIFF_HEAD_DIM = 64
DIFF_ROT = DIFF_HEAD_DIM // 4
ROPE_THETA = 500000.0
N_GROUPS = 4
EXPERTS_PER_GROUP = 8
N_EXPERTS = N_GROUPS * EXPERTS_PER_GROUP
EXPERT_FF = 256
EPS = 1e-6
LAMBDA_INIT = 0.8 - 0.6 * math.exp(-0.3 * 0)

LANES = 128
D_DIFF = DIFF_HEADS * 2 * DIFF_HEAD_DIM
D_MLA_PAD = MLA_HEADS * LANES
LOG2E = math.log2(math.e)
VMEM_LIMIT = 56 * 1024 * 1024

BF16 = jnp.bfloat16
F32 = jnp.float32

_C_Q = 0
_C_KV = _C_Q + Q_RANK
_C_PE = _C_KV + KV_RANK
_C_DQ = _C_PE + LANES
_C_DK = _C_DQ + D_DIFF
_C_END = _C_DK + D_DIFF


def _rms(x, g):
    return x * lax.rsqrt(jnp.mean(x * x, axis=-1, keepdims=True) + EPS) * g


def _dot(a, b):
    return jnp.dot(a, b, preferred_element_type=F32)


def _dot_nt(a, b):
    return lax.dot_general(a, b, (((1,), (1,)), ((), ())), preferred_element_type=F32)


ROW_TILES = D_MODEL // LANES
assert ROW_TILES == 8


def _token_tile(ref, t):
    return ref.at[pl.ds(pl.multiple_of(t * ROW_TILES, ROW_TILES), ROW_TILES)]


def _store_row_tiles(ref, x):
    for j in range(ROW_TILES):
        ref[pl.ds(j, x.shape[0], stride=ROW_TILES), :] = x[:, j * LANES:(j + 1) * LANES]


def _load_row_tiles(ref):
    rows = ref.shape[0] // ROW_TILES
    return jnp.concatenate([ref[pl.ds(j, rows, stride=ROW_TILES), :] for j in range(ROW_TILES)], axis=1)


def _rotary(x, table, half):
    return (x * table[:, 0:LANES] + pltpu.roll(x, half, 1) * table[:, LANES:2 * LANES]
            + pltpu.roll(x, LANES - half, 1) * table[:, 2 * LANES:3 * LANES])


def _proj_kernel(x_ref, an_ref, w1_ref, qn_ref, wq_ref, kvn_ref, wk_ref, wvt_ref, wdvt_ref,
                 tq_ref, tpe_ref, tdq_ref, tdk_ref,
                 q_ref, k_ref, vt_ref, dq_ref, dk_ref, dvt_ref):
    hn = _rms(x_ref[...], an_ref[...]).astype(BF16)

    def proj(lo, hi):
        return _dot(hn, w1_ref[:, lo:hi])

    cq = _rms(proj(_C_Q, _C_KV), qn_ref[...]).astype(BF16)
    tq = tq_ref[...]
    qa = _dot(cq, wq_ref[...])
    for h in range(MLA_HEADS):
        blk = slice(h * LANES, (h + 1) * LANES)
        q_ref[:, blk] = _rotary(qa[:, blk], tq, MLA_ROPE // 2).astype(BF16)

    kv_pe = proj(_C_KV, _C_DQ)
    ckv = _rms(kv_pe[:, 0:KV_RANK], kvn_ref[...]).astype(BF16)
    pe = _rotary(kv_pe[:, KV_RANK:], tpe_ref[...], MLA_ROPE // 2)
    pe_hi = pe.astype(BF16)
    pe_lo = (pe - pe_hi.astype(F32)).astype(BF16)
    k_all = _dot(jnp.concatenate([ckv, pe_hi, pe_lo], axis=1), wk_ref[...])
    k_ref[...] = k_all.astype(BF16)
    vt = _dot_nt(wvt_ref[...], ckv)
    ones_row = (lax.broadcasted_iota(jnp.int32, vt.shape, 0) & (LANES - 1)) == MLA_V
    vt_ref[...] = jnp.where(ones_row, 1.0, vt).astype(BF16)

    tdq, tdk = tdq_ref[...], tdk_ref[...]
    dq, dk = proj(_C_DQ, _C_DK), proj(_C_DK, _C_END)
    for h in range(DIFF_HEADS):
        blk = slice(h * LANES, (h + 1) * LANES)
        dq_ref[:, blk] = _rotary(dq[:, blk], tdq, DIFF_ROT // 2).astype(BF16)
        dk_ref[:, blk] = _rotary(dk[:, blk], tdk, DIFF_ROT // 2).astype(BF16)
    dvt_ref[...] = _dot_nt(wdvt_ref[...], hn).astype(BF16)


def _proj_call(x2, packed, tables, seq, tm):
    T = x2.shape[0]
    nper = seq // tm
    B = T // seq
    row = lambda i: (i, 0)
    const = lambda i: (0, 0)
    tab = lambda i: (i % nper, 0)
    col = lambda i: (i // nper, 0, i % nper)
    full = lambda a: pl.BlockSpec(a.shape, const)
    tspecs = [pl.BlockSpec((tm, 3 * LANES), lambda i, j=j: (i % nper, j)) for j in range(N_ROPE_TABLES)]
    tok = lambda d: (jax.ShapeDtypeStruct((T, d), BF16), pl.BlockSpec((tm, d), row))
    tr = lambda d: (jax.ShapeDtypeStruct((B, d, seq), BF16), pl.BlockSpec((None, d, tm), col))
    outs, ospecs = zip(tok(D_MLA_PAD), tok(D_MLA_PAD), tr(D_MLA_PAD), tok(D_DIFF), tok(D_DIFF), tr(D_DIFF))
    ws = [packed[k] for k in ("attn_norm", "w1", "q_norm", "wq", "kv_norm", "wk", "wvt", "wdvt")]
    return pl.pallas_call(
        _proj_kernel,
        grid=(T // tm,),
        in_specs=[pl.BlockSpec((tm, D_MODEL), row)] + [full(w) for w in ws] + tspecs,
        out_specs=ospecs,
        out_shape=outs,
        compiler_params=pltpu.CompilerParams(dimension_semantics=("arbitrary",), vmem_limit_bytes=VMEM_LIMIT),
        name="proj",
    )(x2, *ws, *([tables] * N_ROPE_TABLES))


def _block_index_maps(n_heads, n_qblocks, n_blocks):
    per_batch = n_heads * n_qblocks

    def split(blk):
        b = blk // per_batch
        r = blk % per_batch
        return b, r // n_qblocks, r % n_qblocks

    def rows(blk):
        b, h, i = split(blk)
        return b, i, h

    def keys(blk):
        b, h, _ = split(blk)
        return b, 0, h

    def values_t(blk):
        b, h, _ = split(blk)
        return b, h, 0

    cur = lambda t: jnp.minimum(t, n_blocks - 1)
    prev = lambda t: jnp.maximum(t - 1, 0)
    return (lambda t: rows(cur(t)), lambda t: keys(cur(t)), lambda t: rows(prev(t)), lambda t: values_t(prev(t)))


KEY_CHUNK = 256


def _key_chunk(c):
    return slice(c * KEY_CHUNK, (c + 1) * KEY_CHUNK)


def _running_max(m, st):
    mc = jnp.max(st, axis=0, keepdims=True)
    return mc if m is None else jnp.maximum(m, mc)


def _accumulate(acc, part):
    return part if acc is None else acc + part


def _pipelined_steps(step, zero_scratch):
    t = pl.program_id(0)

    @pl.when(t == 0)
    def _():
        zero_scratch()

    @pl.when(t % 2 == 0)
    def _():
        step(1, 0)

    @pl.when(t % 2 == 1)
    def _():
        step(0, 1)


def _mla_kernel(q_ref, k_ref, vt_ref, o_ref, s0_ref, s1_ref, m0_ref, m1_ref):
    s_refs, m_refs = (s0_ref, s1_ref), (m0_ref, m1_ref)

    def step(src, dst):
        q = q_ref[...]
        m_src = m_refs[src][...]
        scores = lambda c: _dot_nt(k_ref[_key_chunk(c), :], q)
        n_chunks = k_ref.shape[0] // KEY_CHUNK
        ahead, ot, m_run = scores(0), None, None
        for c in range(n_chunks):
            keys = _key_chunk(c)
            st, ahead = ahead, (scores(c + 1) if c + 1 < n_chunks else None)
            s_refs[dst][keys, :] = st
            m_run = _running_max(m_run, st)
            pt = jnp.exp2(s_refs[src][keys, :] - m_src).astype(BF16)
            ot = _accumulate(ot, _dot(vt_ref[:, keys], pt))
        m_refs[dst][...] = m_run
        o_ref[...] = (ot / ot[MLA_V:MLA_V + 1, :]).T.astype(BF16)

    def zero_scratch():
        s1_ref[...] = jnp.zeros_like(s1_ref)
        m1_ref[...] = jnp.zeros_like(m1_ref)

    _pipelined_steps(step, zero_scratch)


def _mla_call(q, k, vt, tq):
    B, S, _ = q.shape
    n_blocks = B * MLA_HEADS * (S // tq)
    q_map, k_map, o_map, vt_map = _block_index_maps(MLA_HEADS, S // tq, n_blocks)
    return pl.pallas_call(
        _mla_kernel,
        grid=(n_blocks + 1,),
        in_specs=[pl.BlockSpec((None, tq, LANES), q_map), pl.BlockSpec((None, S, LANES), k_map),
                  pl.BlockSpec((None, LANES, S), vt_map)],
        out_specs=pl.BlockSpec((None, tq, LANES), o_map),
        out_shape=jax.ShapeDtypeStruct((B, S, D_MLA_PAD), BF16),
        scratch_shapes=[pltpu.VMEM((S, tq), F32)] * 2 + [pltpu.VMEM((1, tq), F32)] * 2,
        compiler_params=pltpu.CompilerParams(dimension_semantics=("arbitrary",), vmem_limit_bytes=VMEM_LIMIT),
        name="mla_attn",
    )(q, k, vt)


def _diff_kernel(lq1_ref, lk1_ref, lq2_ref, lk2_ref, g_ref, q_ref, k_ref, vt_ref, o_ref,
                 sa0_ref, sa1_ref, sb0_ref, sb1_ref, ma0_ref, ma1_ref, mb0_ref, mb1_ref):
    sa_refs, sb_refs = (sa0_ref, sa1_ref), (sb0_ref, sb1_ref)
    ma_refs, mb_refs = (ma0_ref, ma1_ref), (mb0_ref, mb1_ref)

    def step(src, dst):
        lam = (jnp.exp(jnp.sum(lq1_ref[...] * lk1_ref[...], axis=-1, keepdims=True))
               - jnp.exp(jnp.sum(lq2_ref[...] * lk2_ref[...], axis=-1, keepdims=True)) + LAMBDA_INIT)
        q = q_ref[...]
        first = lax.broadcasted_iota(jnp.int32, q.shape, 1) < DIFF_HEAD_DIM
        zero = jnp.zeros_like(q)
        qs = (jnp.where(first, q, zero), jnp.where(first, zero, q))
        s_maps, m_maps = (sa_refs, sb_refs), (ma_refs, mb_refs)
        m_src = [m_refs[src][...] for m_refs in m_maps]
        scores = lambda c: [_dot_nt(k_ref[_key_chunk(c), :], qc) for qc in qs]
        n_chunks = k_ref.shape[0] // KEY_CHUNK
        ahead = scores(0)
        m_run, tot, ot = [None, None], [None, None], [None, None]
        for c in range(n_chunks):
            keys = _key_chunk(c)
            now, ahead = ahead, (scores(c + 1) if c + 1 < n_chunks else None)
            vt = vt_ref[:, keys]
            for i in range(2):
                s_maps[i][dst][keys, :] = now[i]
                m_run[i] = _running_max(m_run[i], now[i])
                e = jnp.exp2(s_maps[i][src][keys, :] - m_src[i])
                tot[i] = _accumulate(tot[i], jnp.sum(e, axis=0, keepdims=True))
                ot[i] = _accumulate(ot[i], _dot(vt, e.astype(BF16)))
        for i in range(2):
            m_maps[i][dst][...] = m_run[i]
        o = ot[0] * (1.0 / tot[0]) - ot[1] * (lam / tot[1])
        o_ref[...] = (_rms(o.T, g_ref[...]) * (1.0 - LAMBDA_INIT)).astype(BF16)

    def zero_scratch():
        for ref in (sa1_ref, sb1_ref, ma1_ref, mb1_ref):
            ref[...] = jnp.zeros_like(ref)

    _pipelined_steps(step, zero_scratch)


def _diff_call(lams, subln, dq, dk, dvt, tq):
    B, S, _ = dq.shape
    n_blocks = B * DIFF_HEADS * (S // tq)
    q_map, k_map, o_map, vt_map = _block_index_maps(DIFF_HEADS, S // tq, n_blocks)
    small = lambda a: pl.BlockSpec(a.shape, lambda t: (0, 0))
    return pl.pallas_call(
        _diff_kernel,
        grid=(n_blocks + 1,),
        in_specs=[small(a) for a in lams] + [small(subln), pl.BlockSpec((None, tq, LANES), q_map),
                                             pl.BlockSpec((None, S, LANES), k_map),
                                             pl.BlockSpec((None, LANES, S), vt_map)],
        out_specs=pl.BlockSpec((None, tq, LANES), o_map),
        out_shape=jax.ShapeDtypeStruct((B, S, D_DIFF), BF16),
        scratch_shapes=[pltpu.VMEM((S, tq), F32)] * 4 + [pltpu.VMEM((1, tq), F32)] * 4,
        compiler_params=pltpu.CompilerParams(dimension_semantics=("arbitrary",), vmem_limit_bytes=VMEM_LIMIT),
        name="diff_attn",
    )(*lams, subln, dq, dk, dvt)


def _mix_kernel(x_ref, a_ref, b_ref, woa_ref, wob_ref, fn_ref, wrg_ref, brg_ref, wre_ref, bre_ref, earlier_ref,
                xn_ref, hn_ref, gate_ref, route_ref, count_ref, base_ref, lg0_ref, lg1_ref, le0_ref, le1_ref):
    t = pl.program_id(0)
    group_refs, expert_refs = (lg0_ref, lg1_ref), (le0_ref, le1_ref)

    def step(src, dst):
        xn = x_ref[...] + _dot(a_ref[...], woa_ref[...]) + _dot(b_ref[...], wob_ref[...])
        xn_ref[...] = xn
        hn32 = _rms(xn, fn_ref[...])
        _store_row_tiles(hn_ref, hn32)
        hn = hn32.astype(BF16)
        group_refs[dst][...] = _dot(hn, wrg_ref[...]) + brg_ref[...]
        expert_refs[dst][...] = _dot(hn, wre_ref[...]) + bre_ref[...]

        lane = lax.broadcasted_iota(jnp.int32, (xn.shape[0], LANES), 1)
        neg = jnp.float32(-jnp.inf)
        big = jnp.int32(LANES)

        def first_lane_of_max(vals):
            top = jnp.max(vals, axis=-1, keepdims=True)
            return top, jnp.min(jnp.where(vals == top, lane, big), axis=-1, keepdims=True)

        gl = jnp.where(lane < N_GROUPS, group_refs[src][...], neg)
        ge = jnp.exp(gl - jnp.max(gl, axis=-1, keepdims=True))
        gp = ge / jnp.sum(ge, axis=-1, keepdims=True)
        p_g, g_idx = first_lane_of_max(gp)

        in_group = (lane < N_EXPERTS) & ((lane & -EXPERTS_PER_GROUP) == g_idx * EXPERTS_PER_GROUP)
        wl = jnp.where(in_group, expert_refs[src][...], neg)
        we = jnp.exp(wl - jnp.max(wl, axis=-1, keepdims=True))
        wp = jnp.where(in_group, we / jnp.sum(we, axis=-1, keepdims=True), -1.0)
        w_a, i_a = first_lane_of_max(wp)
        w_b, i_b = first_lane_of_max(jnp.where(lane == i_a, -1.0, wp))
        tot = w_a + w_b
        gate_ref[...] = (jnp.where(lane == 0, p_g * (w_a / tot), 0.0)
                         + jnp.where(lane == 1, p_g * (w_b / tot), 0.0))

        picked = jnp.where((lane == i_a) | (lane == i_b), 1.0, 0.0)
        before = base_ref[...] + _dot(earlier_ref[...], picked.astype(BF16))
        rank_a = jnp.sum(jnp.where(lane == i_a, before, 0.0), axis=-1, keepdims=True).astype(jnp.int32)
        rank_b = jnp.sum(jnp.where(lane == i_b, before, 0.0), axis=-1, keepdims=True).astype(jnp.int32)
        zero = jnp.zeros_like(lane)
        route_ref[...] = (jnp.where(lane == 0, i_a, zero) + jnp.where(lane == 1, i_b, zero)
                          + jnp.where(lane == 2, rank_a, zero) + jnp.where(lane == 3, rank_b, zero))
        base_ref[...] += jnp.where(t > 0, jnp.sum(picked, axis=0, keepdims=True), 0.0)
        count_ref[...] = base_ref[...]

    def zero_scratch():
        for ref in (base_ref, lg1_ref, le1_ref):
            ref[...] = jnp.zeros_like(ref)

    _pipelined_steps(step, zero_scratch)


def _mix_call(x2, a2, b2, packed, tm):
    T = x2.shape[0]
    n_tiles = T // tm
    cur = lambda t: (jnp.minimum(t, n_tiles - 1), 0)
    prev = lambda t: (jnp.maximum(t - 1, 0), 0)
    full = lambda a: pl.BlockSpec(a.shape, lambda t: (0, 0))
    ws = [packed[k] for k in ("woa", "wob", "ffn_norm", "wrg", "brg", "wre", "bre")]
    ws.append(jnp.asarray(np.tril(np.ones((tm, tm), np.float32), -1), BF16))
    return pl.pallas_call(
        _mix_kernel,
        grid=(n_tiles + 1,),
        in_specs=[pl.BlockSpec((tm, D_MODEL), cur), pl.BlockSpec((tm, D_MLA_PAD), cur),
                  pl.BlockSpec((tm, D_DIFF), cur)] + [full(w) for w in ws],
        out_specs=[pl.BlockSpec((tm, D_MODEL), cur), pl.BlockSpec((tm * ROW_TILES, LANES), cur),
                   pl.BlockSpec((tm, LANES), prev), pl.BlockSpec((tm, LANES), prev),
                   pl.BlockSpec((1, LANES), lambda t: (0, 0))],
        out_shape=[jax.ShapeDtypeStruct((T, D_MODEL), F32), jax.ShapeDtypeStruct((T * ROW_TILES, LANES), F32),
                   jax.ShapeDtypeStruct((T, LANES), F32), jax.ShapeDtypeStruct((T, LANES), jnp.int32),
                   jax.ShapeDtypeStruct((1, LANES), F32)],
        scratch_shapes=[pltpu.VMEM((1, LANES), F32)] + [pltpu.VMEM((tm, LANES), F32)] * 4,
        compiler_params=pltpu.CompilerParams(dimension_semantics=("arbitrary",), vmem_limit_bytes=VMEM_LIMIT),
        name="mix_router",
    )(x2, a2, b2, *ws)


def _route_plan(route, count, tile_rows):
    T = route.shape[0]
    P = 2 * T
    nt = P // tile_rows
    i32 = jnp.int32
    eid = jnp.arange(N_EXPERTS, dtype=i32)
    counts = count[0, :N_EXPERTS].astype(i32)
    seg_end = jnp.sum(jnp.where(eid[None, :] <= eid[:, None], counts[None, :], 0), axis=1)
    seg_start = seg_end - counts
    picks = route[:, 0:2]
    dest = jnp.sum(jnp.where(picks[:, :, None] == eid, seg_start, 0), axis=-1) + route[:, 2:4]

    tile_start = jnp.arange(nt, dtype=i32) * tile_rows
    slot_of_tile = jnp.arange(nt, dtype=i32) + jnp.sum(seg_start[None, :] < tile_start[:, None], axis=1).astype(i32)
    slot_of_seg = eid + jnp.minimum(seg_start // tile_rows + 1, nt)
    vals = jnp.concatenate([tile_start, seg_start])
    slots = jnp.concatenate([slot_of_tile, slot_of_seg])
    n_items = nt + N_EXPERTS
    bounds = jnp.sum(jnp.where(slots[None, :] == jnp.arange(n_items, dtype=i32)[:, None], vals[None, :], 0), axis=1)
    nxt = jnp.concatenate([bounds[1:], jnp.full((1,), P, i32)])
    tile = jnp.minimum(bounds // tile_rows, nt - 1)
    lo = bounds - tile * tile_rows
    hi = nxt - tile * tile_rows
    expert = jnp.minimum(jnp.sum(seg_end[None, :] <= bounds[:, None], axis=1).astype(i32), N_EXPERTS - 1)
    return dest.astype(i32), (tile, expert, lo, hi)


def _dispatch_kernel(dest_ref, hn_ref, xs_hbm, sem, *, tm):
    def issue(r, carry):
        for pick in range(2):
            pltpu.make_async_copy(_token_tile(hn_ref, r), _token_tile(xs_hbm, dest_ref[0, 2 * r + pick]),
                                  sem).start()
        return carry

    lax.fori_loop(0, tm, issue, 0, unroll=4)
    for _ in range(2):
        pltpu.make_async_copy(hn_ref, xs_hbm.at[pl.ds(0, tm * ROW_TILES)], sem).wait()


def _dispatch_call(dest, hn, tm):
    T = hn.shape[0] // ROW_TILES
    dest3 = dest.reshape(T // tm, 1, 2 * tm)
    return pl.pallas_call(
        functools.partial(_dispatch_kernel, tm=tm),
        grid=(T // tm,),
        in_specs=[pl.BlockSpec((None, 1, 2 * tm), lambda i: (i, 0, 0), memory_space=pltpu.SMEM),
                  pl.BlockSpec((tm * ROW_TILES, LANES), lambda i: (i, 0))],
        out_specs=pl.BlockSpec(memory_space=pl.ANY),
        out_shape=jax.ShapeDtypeStruct((2 * T * ROW_TILES, LANES), F32),
        scratch_shapes=[pltpu.SemaphoreType.DMA],
        compiler_params=pltpu.CompilerParams(dimension_semantics=("arbitrary",)),
        name="moe_dispatch",
    )(dest3, hn)


def _expert_kernel(tile_ref, exp_ref, lo_ref, hi_ref, xs_ref, wg_ref, wu_ref, wd_ref, ys_ref,
                   wg_bf, wu_bf, wd_bf):
    w = pl.program_id(0)
    lo = lo_ref[w]
    hi = hi_ref[w]
    before = jnp.maximum(w - 1, 0)
    first_visit = (w == 0) | (tile_ref[w] != tile_ref[before])

    @pl.when(first_visit)
    def _():
        ys_ref[...] = jnp.zeros_like(ys_ref)

    @pl.when((w == 0) | (exp_ref[w] != exp_ref[before]))
    def _():
        wg_bf[...] = wg_ref[...].astype(BF16)
        wu_bf[...] = wu_ref[...].astype(BF16)
        wd_bf[...] = wd_ref[...].astype(BF16)

    @pl.when(hi > lo)
    def _():
        x = _load_row_tiles(xs_ref).astype(BF16)
        g = _dot(x, wg_bf[...])
        h = (g * jax.nn.sigmoid(g) * _dot(x, wu_bf[...])).astype(BF16)
        y = _dot(h, wd_bf[...])
        row = lax.broadcasted_iota(jnp.int32, (y.shape[0], 1), 0)
        _store_row_tiles(ys_ref, jnp.where((row >= lo) & (row < hi), y, _load_row_tiles(ys_ref)))


def _expert_call(plan, xs, packed, tm):
    n_items = plan[0].shape[0]
    by_tile = lambda w, tile, exp, lo, hi: (tile[w], 0)
    by_expert = lambda w, tile, exp, lo, hi: (exp[w], 0, 0)
    grid_spec = pltpu.PrefetchScalarGridSpec(
        num_scalar_prefetch=4,
        grid=(n_items,),
        in_specs=[pl.BlockSpec((tm * ROW_TILES, LANES), by_tile),
                  pl.BlockSpec((None, D_MODEL, EXPERT_FF), by_expert),
                  pl.BlockSpec((None, D_MODEL, EXPERT_FF), by_expert),
                  pl.BlockSpec((None, EXPERT_FF, D_MODEL), by_expert)],
        out_specs=pl.BlockSpec((tm * ROW_TILES, LANES), by_tile),
        scratch_shapes=[pltpu.VMEM((D_MODEL, EXPERT_FF), BF16), pltpu.VMEM((D_MODEL, EXPERT_FF), BF16),
                        pltpu.VMEM((EXPERT_FF, D_MODEL), BF16)],
    )
    return pl.pallas_call(
        _expert_kernel,
        grid_spec=grid_spec,
        out_shape=jax.ShapeDtypeStruct(xs.shape, F32),
        compiler_params=pltpu.CompilerParams(dimension_semantics=("arbitrary",), vmem_limit_bytes=VMEM_LIMIT),
        name="moe_experts",
    )(*plan, xs, packed["wg"], packed["wu"], packed["wd"])


def _combine_kernel(dest_ref, dest_next_ref, ys_hbm, xn_ref, gate_ref, fin_ref, y_ref, buf, sems, *, tm):
    i = pl.program_id(0)

    def start_gather(dests, slot):
        def issue(r, carry):
            for pick in range(2):
                pltpu.make_async_copy(_token_tile(ys_hbm, dests[0, 2 * r + pick]),
                                      _token_tile(buf.at[slot, pick], r), sems.at[slot]).start()
            return carry
        lax.fori_loop(0, tm, issue, 0, unroll=4)

    def finish(slot):
        for pick in range(2):
            pltpu.make_async_copy(ys_hbm.at[pl.ds(0, tm * ROW_TILES)], buf.at[slot, pick], sems.at[slot]).wait()
        gate = gate_ref[...]
        y = (xn_ref[...] + gate[:, 0:1] * _load_row_tiles(buf.at[slot, 0])
             + gate[:, 1:2] * _load_row_tiles(buf.at[slot, 1]))
        y_ref[...] = _rms(y, fin_ref[...])

    @pl.when(i == 0)
    def _():
        start_gather(dest_ref, 0)

    for slot in range(2):
        @pl.when(i % 2 == slot)
        def _(slot=slot):
            @pl.when(i + 1 < pl.num_programs(0))
            def _():
                start_gather(dest_next_ref, 1 - slot)
            finish(slot)


def _combine_call(dest, ys, xn, gate, packed, tm):
    T = xn.shape[0]
    n_tiles = T // tm
    dest3 = dest.reshape(n_tiles, 1, 2 * tm)
    row = lambda i: (i, 0)
    dest_spec = lambda index: pl.BlockSpec((None, 1, 2 * tm), index, memory_space=pltpu.SMEM)
    return pl.pallas_call(
        functools.partial(_combine_kernel, tm=tm),
        grid=(n_tiles,),
        in_specs=[dest_spec(lambda i: (i, 0, 0)), dest_spec(lambda i: (jnp.minimum(i + 1, n_tiles - 1), 0, 0)),
                  pl.BlockSpec(memory_space=pl.ANY),
                  pl.BlockSpec((tm, D_MODEL), row), pl.BlockSpec((tm, LANES), row),
                  pl.BlockSpec((1, D_MODEL), lambda i: (0, 0))],
        out_specs=pl.BlockSpec((tm, D_MODEL), row),
        out_shape=jax.ShapeDtypeStruct((T, D_MODEL), F32),
        scratch_shapes=[pltpu.VMEM((2, 2, tm * ROW_TILES, LANES), F32), pltpu.SemaphoreType.DMA((2,))],
        compiler_params=pltpu.CompilerParams(dimension_semantics=("arbitrary",), vmem_limit_bytes=VMEM_LIMIT),
        name="moe_combine",
    )(dest3, dest3, ys, xn, gate, packed["final_norm"])


def _pad_blocks(w, block, to):
    K = w.shape[0]
    wb = w.reshape(K, -1, block)
    return jnp.pad(wb, ((0, 0), (0, 0), (0, to - block))).reshape(K, -1)


def _pack_weights(attn_norm, w_in, q_norm, w_uq, kv_norm, w_ukv, w_out, ffn_norm, router_group,
                  router_group_bias, router_expert, router_expert_bias, w_gate, w_up, w_down, final_norm):
    l = 0
    wi = w_in[l]
    o_q, o_kv, o_pe = 0, Q_RANK, Q_RANK + KV_RANK
    o_dq = o_pe + MLA_ROPE
    o_dk = o_dq + D_DIFF
    o_dv = o_dk + D_DIFF
    w1 = jnp.concatenate([wi[:, o_q:o_dq], jnp.zeros((D_MODEL, LANES - MLA_ROPE), F32), wi[:, o_dq:o_dv]],
                         axis=1).astype(BF16)
    wdvt = wi[:, o_dv:].T.astype(BF16)
    wq = _pad_blocks(w_uq[l], MLA_NOPE + MLA_ROPE, LANES).astype(BF16)

    wkv3 = w_ukv[l].reshape(KV_RANK, MLA_HEADS, MLA_NOPE + MLA_V)
    padk = lambda a: jnp.pad(a, ((0, 0), (0, 0), (0, LANES - a.shape[-1]))).reshape(KV_RANK, D_MLA_PAD)
    wvt = padk(wkv3[:, :, MLA_NOPE:]).T.astype(BF16)
    place = np.zeros((LANES, MLA_HEADS, LANES), np.float32)
    for r in range(MLA_ROPE):
        place[r, :, MLA_NOPE + r] = 1.0
    place = jnp.asarray(place.reshape(LANES, D_MLA_PAD))
    wk = jnp.concatenate([padk(wkv3[:, :, :MLA_NOPE]), place, place], axis=0).astype(BF16)

    wo = w_out[l]
    n_mla = MLA_HEADS * MLA_V
    woa = jnp.pad(wo[:n_mla].reshape(MLA_HEADS, MLA_V, D_MODEL),
                  ((0, 0), (0, LANES - MLA_V), (0, 0))).reshape(D_MLA_PAD, D_MODEL).astype(BF16)
    padl = lambda a: jnp.pad(a, ((0, 0), (0, LANES - a.shape[-1])))
    return {
        "attn_norm": attn_norm[l][None], "w1": w1, "q_norm": q_norm[l][None], "wq": wq,
        "kv_norm": kv_norm[l][None], "wk": wk, "wvt": wvt, "wdvt": wdvt,
        "woa": woa, "wob": wo[n_mla:].astype(BF16), "ffn_norm": ffn_norm[l][None],
        "wrg": padl(router_group[l]).astype(BF16), "brg": padl(router_group_bias[l][None]),
        "wre": padl(router_expert[l]).astype(BF16), "bre": padl(router_expert_bias[l][None]),
        "wg": w_gate[l], "wu": w_up[l], "wd": w_down[l],
        "final_norm": final_norm[None],
    }


def _cos_sin(seq, rot):
    half = rot // 2
    inv = ROPE_THETA ** (-jnp.arange(half, dtype=F32) / half)
    ang = jnp.arange(seq, dtype=F32)[:, None] * inv[None, :]
    return jnp.cos(ang), jnp.sin(ang)


N_ROPE_TABLES = 4


def _rope_tables(seq):
    c_mla, s_mla = _cos_sin(seq, MLA_ROPE)
    c_dif, s_dif = _cos_sin(seq, DIFF_ROT)
    basis = jnp.concatenate([jnp.ones((seq, 1), F32), c_mla, s_mla, c_dif, s_dif], axis=1)
    one, mla, dif = 0, 1, 1 + MLA_ROPE

    width = 3 * LANES
    place = np.zeros((basis.shape[1], N_ROPE_TABLES * width), np.float32)
    scale = np.ones((N_ROPE_TABLES * width,), np.float32)

    def fill(table, offsets, half, first):
        base = table * width
        place[one, base:base + LANES] = 1.0
        for off in offsets:
            for j in range(half):
                for lane in (off + j, off + half + j):
                    place[one, base + lane] = 0.0
                    place[first + j, base + lane] = 1.0
                place[first + half + j, base + LANES + off + half + j] = 1.0
                place[first + half + j, base + 2 * LANES + off + j] = -1.0

    fill(0, [MLA_NOPE], MLA_ROPE // 2, mla)
    fill(1, [0], MLA_ROPE // 2, mla)
    fill(2, [0, DIFF_HEAD_DIM], DIFF_ROT // 2, dif)
    fill(3, [0, DIFF_HEAD_DIM], DIFF_ROT // 2, dif)
    scale[0:width] = (MLA_NOPE + MLA_ROPE) ** -0.5 * LOG2E
    scale[2 * width:3 * width] = DIFF_HEAD_DIM ** -0.5 * LOG2E
    return jnp.dot(basis, jnp.asarray(place), precision=lax.Precision.HIGHEST) * jnp.asarray(scale)


def _forward(x, packed, tables, lams, subln):
    B, S, D = x.shape
    T = B * S
    x2 = x.reshape(T, D)
    q, k, vt, dq, dk, dvt = _proj_call(x2, packed, tables, S, tm=512)
    r3 = lambda a: a.reshape(B, S, a.shape[-1])
    a = _mla_call(r3(q), r3(k), vt, tq=512)
    b = _diff_call(lams, subln, r3(dq), r3(dk), dvt, tq=512)
    xn, hn, gate, route, count = _mix_call(x2, a.reshape(T, -1), b.reshape(T, -1), packed, tm=512)
    dest, plan = _route_plan(route, count, tile_rows=256)
    xs = _dispatch_call(dest, hn, tm=512)
    ys = _expert_call(plan, xs, packed, tm=256)
    y = _combine_call(dest, ys, xn, gate, packed, tm=512)
    return y.reshape(B, S, D)


def kernel(x_prompt, x_sample, attn_norm, w_in, q_norm, w_uq, kv_norm, w_ukv, diff_lq1, diff_lk1, diff_lq2,
           diff_lk2, diff_subln, w_out, ffn_norm, router_group, router_group_bias, router_expert,
           router_expert_bias, w_gate, w_up, w_down, final_norm):
    packed = _pack_weights(attn_norm, w_in, q_norm, w_uq, kv_norm, w_ukv, w_out, ffn_norm, router_group,
                           router_group_bias, router_expert, router_expert_bias, w_gate, w_up, w_down,
                           final_norm)
    lams = (diff_lq1, diff_lk1, diff_lq2, diff_lk2)
    subln = diff_subln
    tables = _rope_tables(max(x_prompt.shape[1], x_sample.shape[1]))
    return (_forward(x_prompt, packed, tables, lams, subln), _forward(x_sample, packed, tables, lams, subln))
```

```python
import functools
import math

import numpy as np
import jax
import jax.numpy as jnp
from jax import lax
from jax.experimental import pallas as pl
from jax.experimental.pallas import tpu as pltpu

D_MODEL = 1024
MLA_HEADS = 8
MLA_NOPE = 64
MLA_ROPE = 32
MLA_V = 64
Q_RANK = 256
KV_RANK = 128
DIFF_HEADS = 4
DIFF_HEAD_DIM = 64
DIFF_ROT = DIFF_HEAD_DIM // 4
ROPE_THETA = 500000.0
N_GROUPS = 4
EXPERTS_PER_GROUP = 8
N_EXPERTS = N_GROUPS * EXPERTS_PER_GROUP
EXPERT_FF = 256
EPS = 1e-6
LAMBDA_INIT = 0.8 - 0.6 * math.exp(-0.3 * 0)

LANES = 128
D_DIFF = DIFF_HEADS * 2 * DIFF_HEAD_DIM
D_MLA_PAD = MLA_HEADS * LANES
LOG2E = math.log2(math.e)
VMEM_LIMIT = 56 * 1024 * 1024

BF16 = jnp.bfloat16
F32 = jnp.float32

_C_Q = 0
_C_KV = _C_Q + Q_RANK
_C_PE = _C_KV + KV_RANK
_C_DQ = _C_PE + LANES
_C_DK = _C_DQ + D_DIFF
_C_END = _C_DK + D_DIFF


def _rms(x, g):
    return x * lax.rsqrt(jnp.mean(x * x, axis=-1, keepdims=True) + EPS) * g


def _dot(a, b):
    return jnp.dot(a, b, preferred_element_type=F32)


def _dot_nt(a, b):
    return lax.dot_general(a, b, (((1,), (1,)), ((), ())), preferred_element_type=F32)


ROW_TILES = D_MODEL // LANES
assert ROW_TILES == 8


def _token_tile(ref, t):
    return ref.at[pl.ds(pl.multiple_of(t * ROW_TILES, ROW_TILES), ROW_TILES)]


def _store_row_tiles(ref, x):
    for j in range(ROW_TILES):
        ref[pl.ds(j, x.shape[0], stride=ROW_TILES), :] = x[:, j * LANES:(j + 1) * LANES]


def _load_row_tiles(ref):
    rows = ref.shape[0] // ROW_TILES
    return jnp.concatenate([ref[pl.ds(j, rows, stride=ROW_TILES), :] for j in range(ROW_TILES)], axis=1)


def _rotary(x, table, half):
    return (x * table[:, 0:LANES] + pltpu.roll(x, half, 1) * table[:, LANES:2 * LANES]
            + pltpu.roll(x, LANES - half, 1) * table[:, 2 * LANES:3 * LANES])


def _proj_kernel(x_ref, an_ref, w1_ref, qn_ref, wq_ref, kvn_ref, wk_ref, wvt_ref, wdvt_ref,
                 tq_ref, tpe_ref, tdq_ref, tdk_ref,
                 q_ref, k_ref, vt_ref, dq_ref, dk_ref, dvt_ref):
    hn = _rms(x_ref[...], an_ref[...]).astype(BF16)

    def proj(lo, hi):
        return _dot(hn, w1_ref[:, lo:hi])

    cq = _rms(proj(_C_Q, _C_KV), qn_ref[...]).astype(BF16)
    tq = tq_ref[...]
    qa = _dot(cq, wq_ref[...])
    for h in range(MLA_HEADS):
        blk = slice(h * LANES, (h + 1) * LANES)
        q_ref[:, blk] = _rotary(qa[:, blk], tq, MLA_ROPE // 2).astype(BF16)

    kv_pe = proj(_C_KV, _C_DQ)
    ckv = _rms(kv_pe[:, 0:KV_RANK], kvn_ref[...]).astype(BF16)
    pe = _rotary(kv_pe[:, KV_RANK:], tpe_ref[...], MLA_ROPE // 2)
    pe_hi = pe.astype(BF16)
    pe_lo = (pe - pe_hi.astype(F32)).astype(BF16)
    k_all = _dot(jnp.concatenate([ckv, pe_hi, pe_lo], axis=1), wk_ref[...])
    k_ref[...] = k_all.astype(BF16)
    vt = _dot_nt(wvt_ref[...], ckv)
    ones_row = (lax.broadcasted_iota(jnp.int32, vt.shape, 0) & (LANES - 1)) == MLA_V
    vt_ref[...] = jnp.where(ones_row, 1.0, vt).astype(BF16)

    tdq, tdk = tdq_ref[...], tdk_ref[...]
    dq, dk = proj(_C_DQ, _C_DK), proj(_C_DK, _C_END)
    for h in range(DIFF_HEADS):
        blk = slice(h * LANES, (h + 1) * LANES)
        dq_ref[:, blk] = _rotary(dq[:, blk], tdq, DIFF_ROT // 2).astype(BF16)
        dk_ref[:, blk] = _rotary(dk[:, blk], tdk, DIFF_ROT // 2).astype(BF16)
    dvt_ref[...] = _dot_nt(wdvt_ref[...], hn).astype(BF16)


def _proj_call(x2, packed, tables, seq, tm):
    T = x2.shape[0]
    nper = seq // tm
    B = T // seq
    row = lambda i: (i, 0)
    const = lambda i: (0, 0)
    tab = lambda i: (i % nper, 0)
    col = lambda i: (i // nper, 0, i % nper)
    full = lambda a: pl.BlockSpec(a.shape, const)
    tspecs = [pl.BlockSpec((tm, 3 * LANES), lambda i, j=j: (i % nper, j)) for j in range(N_ROPE_TABLES)]
    tok = lambda d: (jax.ShapeDtypeStruct((T, d), BF16), pl.BlockSpec((tm, d), row))
    tr = lambda d: (jax.ShapeDtypeStruct((B, d, seq), BF16), pl.BlockSpec((None, d, tm), col))
    outs, ospecs = zip(tok(D_MLA_PAD), tok(D_MLA_PAD), tr(D_MLA_PAD), tok(D_DIFF), tok(D_DIFF), tr(D_DIFF))
    ws = [packed[k] for k in ("attn_norm", "w1", "q_norm", "wq", "kv_norm", "wk", "wvt", "wdvt")]
    return pl.pallas_call(
        _proj_kernel,
        grid=(T // tm,),
        in_specs=[pl.BlockSpec((tm, D_MODEL), row)] + [full(w) for w in ws] + tspecs,
        out_specs=ospecs,
        out_shape=outs,
        compiler_params=pltpu.CompilerParams(dimension_semantics=("arbitrary",), vmem_limit_bytes=VMEM_LIMIT),
        name="proj",
    )(x2, *ws, *([tables] * N_ROPE_TABLES))


def _block_index_maps(n_heads, n_qblocks, n_blocks):
    per_batch = n_heads * n_qblocks

    def split(blk):
        b = blk // per_batch
        r = blk % per_batch
        return b, r // n_qblocks, r % n_qblocks

    def rows(blk):
        b, h, i = split(blk)
        return b, i, h

    def keys(blk):
        b, h, _ = split(blk)
        return b, 0, h

    def values_t(blk):
        b, h, _ = split(blk)
        return b, h, 0

    cur = lambda t: jnp.minimum(t, n_blocks - 1)
    prev = lambda t: jnp.maximum(t - 1, 0)
    return (lambda t: rows(cur(t)), lambda t: keys(cur(t)), lambda t: rows(prev(t)), lambda t: values_t(prev(t)))


KEY_CHUNK = 256


def _key_chunk(c):
    return slice(c * KEY_CHUNK, (c + 1) * KEY_CHUNK)


def _running_max(m, st):
    mc = jnp.max(st, axis=0, keepdims=True)
    return mc if m is None else jnp.maximum(m, mc)


def _accumulate(acc, part):
    return part if acc is None else acc + part


def _pipelined_steps(step, zero_scratch):
    t = pl.program_id(0)

    @pl.when(t == 0)
    def _():
        zero_scratch()

    @pl.when(t % 2 == 0)
    def _():
        step(1, 0)

    @pl.when(t % 2 == 1)
    def _():
        step(0, 1)


def _mla_kernel(q_ref, k_ref, vt_ref, o_ref, s0_ref, s1_ref, m0_ref, m1_ref):
    s_refs, m_refs = (s0_ref, s1_ref), (m0_ref, m1_ref)

    def step(src, dst):
        q = q_ref[...]
        m_src = m_refs[src][...]
        scores = lambda c: _dot_nt(k_ref[_key_chunk(c), :], q)
        n_chunks = k_ref.shape[0] // KEY_CHUNK
        ahead, ot, m_run = scores(0), None, None
        for c in range(n_chunks):
            keys = _key_chunk(c)
            st, ahead = ahead, (scores(c + 1) if c + 1 < n_chunks else None)
            s_refs[dst][keys, :] = st
            m_run = _running_max(m_run, st)
            pt = jnp.exp2(s_refs[src][keys, :] - m_src).astype(BF16)
            ot = _accumulate(ot, _dot(vt_ref[:, keys], pt))
        m_refs[dst][...] = m_run
        o_ref[...] = (ot / ot[MLA_V:MLA_V + 1, :]).T.astype(BF16)

    def zero_scratch():
        s1_ref[...] = jnp.zeros_like(s1_ref)
        m1_ref[...] = jnp.zeros_like(m1_ref)

    _pipelined_steps(step, zero_scratch)


def _mla_call(q, k, vt, tq):
    B, S, _ = q.shape
    n_blocks = B * MLA_HEADS * (S // tq)
    q_map, k_map, o_map, vt_map = _block_index_maps(MLA_HEADS, S // tq, n_blocks)
    return pl.pallas_call(
        _mla_kernel,
        grid=(n_blocks + 1,),
        in_specs=[pl.BlockSpec((None, tq, LANES), q_map), pl.BlockSpec((None, S, LANES), k_map),
                  pl.BlockSpec((None, LANES, S), vt_map)],
        out_specs=pl.BlockSpec((None, tq, LANES), o_map),
        out_shape=jax.ShapeDtypeStruct((B, S, D_MLA_PAD), BF16),
        scratch_shapes=[pltpu.VMEM((S, tq), F32)] * 2 + [pltpu.VMEM((1, tq), F32)] * 2,
        compiler_params=pltpu.CompilerParams(dimension_semantics=("arbitrary",), vmem_limit_bytes=VMEM_LIMIT),
        name="mla_attn",
    )(q, k, vt)


def _diff_kernel(lq1_ref, lk1_ref, lq2_ref, lk2_ref, g_ref, q_ref, k_ref, vt_ref, o_ref,
                 sa0_ref, sa1_ref, sb0_ref, sb1_ref, ma0_ref, ma1_ref, mb0_ref, mb1_ref):
    sa_refs, sb_refs = (sa0_ref, sa1_ref), (sb0_ref, sb1_ref)
    ma_refs, mb_refs = (ma0_ref, ma1_ref), (mb0_ref, mb1_ref)

    def step(src, dst):
        lam = (jnp.exp(jnp.sum(lq1_ref[...] * lk1_ref[...], axis=-1, keepdims=True))
               - jnp.exp(jnp.sum(lq2_ref[...] * lk2_ref[...], axis=-1, keepdims=True)) + LAMBDA_INIT)
        q = q_ref[...]
        first = lax.broadcasted_iota(jnp.int32, q.shape, 1) < DIFF_HEAD_DIM
        zero = jnp.zeros_like(q)
        qs = (jnp.where(first, q, zero), jnp.where(first, zero, q))
        s_maps, m_maps = (sa_refs, sb_refs), (ma_refs, mb_refs)
        m_src = [m_refs[src][...] for m_refs in m_maps]
        scores = lambda c: [_dot_nt(k_ref[_key_chunk(c), :], qc) for qc in qs]
        n_chunks = k_ref.shape[0] // KEY_CHUNK
        ahead = scores(0)
        m_run, tot, ot = [None, None], [None, None], [None, None]
        for c in range(n_chunks):
            keys = _key_chunk(c)
            now, ahead = ahead, (scores(c + 1) if c + 1 < n_chunks else None)
            vt = vt_ref[:, keys]
            for i in range(2):
                s_maps[i][dst][keys, :] = now[i]
                m_run[i] = _running_max(m_run[i], now[i])
                e = jnp.exp2(s_maps[i][src][keys, :] - m_src[i])
                tot[i] = _accumulate(tot[i], jnp.sum(e, axis=0, keepdims=True))
                ot[i] = _accumulate(ot[i], _dot(vt, e.astype(BF16)))
        for i in range(2):
            m_maps[i][dst][...] = m_run[i]
        o = ot[0] * (1.0 / tot[0]) - ot[1] * (lam / tot[1])
        o_ref[...] = (_rms(o.T, g_ref[...]) * (1.0 - LAMBDA_INIT)).astype(BF16)

    def zero_scratch():
        for ref in (sa1_ref, sb1_ref, ma1_ref, mb1_ref):
            ref[...] = jnp.zeros_like(ref)

    _pipelined_steps(step, zero_scratch)


def _diff_call(lams, subln, dq, dk, dvt, tq):
    B, S, _ = dq.shape
    n_blocks = B * DIFF_HEADS * (S // tq)
    q_map, k_map, o_map, vt_map = _block_index_maps(DIFF_HEADS, S // tq, n_blocks)
    small = lambda a: pl.BlockSpec(a.shape, lambda t: (0, 0))
    return pl.pallas_call(
        _diff_kernel,
        grid=(n_blocks + 1,),
        in_specs=[small(a) for a in lams] + [small(subln), pl.BlockSpec((None, tq, LANES), q_map),
                                             pl.BlockSpec((None, S, LANES), k_map),
                                             pl.BlockSpec((None, LANES, S), vt_map)],
        out_specs=pl.BlockSpec((None, tq, LANES), o_map),
        out_shape=jax.ShapeDtypeStruct((B, S, D_DIFF), BF16),
        scratch_shapes=[pltpu.VMEM((S, tq), F32)] * 4 + [pltpu.VMEM((1, tq), F32)] * 4,
        compiler_params=pltpu.CompilerParams(dimension_semantics=("arbitrary",), vmem_limit_bytes=VMEM_LIMIT),
        name="diff_attn",
    )(*lams, subln, dq, dk, dvt)


def _mix_kernel(x_ref, a_ref, b_ref, woa_ref, wob_ref, fn_ref, wrg_ref, brg_ref, wre_ref, bre_ref, earlier_ref,
                xn_ref, hn_ref, gate_ref, route_ref, count_ref, base_ref, lg0_ref, lg1_ref, le0_ref, le1_ref):
    t = pl.program_id(0)
    group_refs, expert_refs = (lg0_ref, lg1_ref), (le0_ref, le1_ref)

    def step(src, dst):
        xn = x_ref[...] + _dot(a_ref[...], woa_ref[...]) + _dot(b_ref[...], wob_ref[...])
        xn_ref[...] = xn
        hn32 = _rms(xn, fn_ref[...])
        _store_row_tiles(hn_ref, hn32)
        hn = hn32.astype(BF16)
        group_refs[dst][...] = _dot(hn, wrg_ref[...]) + brg_ref[...]
        expert_refs[dst][...] = _dot(hn, wre_ref[...]) + bre_ref[...]

        lane = lax.broadcasted_iota(jnp.int32, (xn.shape[0], LANES), 1)
        neg = jnp.float32(-jnp.inf)
        big = jnp.int32(LANES)

        def first_lane_of_max(vals):
            top = jnp.max(vals, axis=-1, keepdims=True)
            return top, jnp.min(jnp.where(vals == top, lane, big), axis=-1, keepdims=True)

        gl = jnp.where(lane < N_GROUPS, group_refs[src][...], neg)
        ge = jnp.exp(gl - jnp.max(gl, axis=-1, keepdims=True))
        gp = ge / jnp.sum(ge, axis=-1, keepdims=True)
        p_g, g_idx = first_lane_of_max(gp)

        in_group = (lane < N_EXPERTS) & ((lane & -EXPERTS_PER_GROUP) == g_idx * EXPERTS_PER_GROUP)
        wl = jnp.where(in_group, expert_refs[src][...], neg)
        we = jnp.exp(wl - jnp.max(wl, axis=-1, keepdims=True))
        wp = jnp.where(in_group, we / jnp.sum(we, axis=-1, keepdims=True), -1.0)
        w_a, i_a = first_lane_of_max(wp)
        w_b, i_b = first_lane_of_max(jnp.where(lane == i_a, -1.0, wp))
        tot = w_a + w_b
        gate_ref[...] = (jnp.where(lane == 0, p_g * (w_a / tot), 0.0)
                         + jnp.where(lane == 1, p_g * (w_b / tot), 0.0))

        picked = jnp.where((lane == i_a) | (lane == i_b), 1.0, 0.0)
        before = base_ref[...] + _dot(earlier_ref[...], picked.astype(BF16))
        rank_a = jnp.sum(jnp.where(lane == i_a, before, 0.0), axis=-1, keepdims=True).astype(jnp.int32)
        rank_b = jnp.sum(jnp.where(lane == i_b, before, 0.0), axis=-1, keepdims=True).astype(jnp.int32)
        zero = jnp.zeros_like(lane)
        route_ref[...] = (jnp.where(lane == 0, i_a, zero) + jnp.where(lane == 1, i_b, zero)
                          + jnp.where(lane == 2, rank_a, zero) + jnp.where(lane == 3, rank_b, zero))
        base_ref[...] += jnp.where(t > 0, jnp.sum(picked, axis=0, keepdims=True), 0.0)
        count_ref[...] = base_ref[...]

    def zero_scratch():
        for ref in (base_ref, lg1_ref, le1_ref):
            ref[...] = jnp.zeros_like(ref)

    _pipelined_steps(step, zero_scratch)


def _mix_call(x2, a2, b2, packed, tm):
    T = x2.shape[0]
    n_tiles = T // tm
    cur = lambda t: (jnp.minimum(t, n_tiles - 1), 0)
    prev = lambda t: (jnp.maximum(t - 1, 0), 0)
    full = lambda a: pl.BlockSpec(a.shape, lambda t: (0, 0))
    ws = [packed[k] for k in ("woa", "wob", "ffn_norm", "wrg", "brg", "wre", "bre")]
    ws.append(jnp.asarray(np.tril(np.ones((tm, tm), np.float32), -1), BF16))
    return pl.pallas_call(
        _mix_kernel,
        grid=(n_tiles + 1,),
        in_specs=[pl.BlockSpec((tm, D_MODEL), cur), pl.BlockSpec((tm, D_MLA_PAD), cur),
                  pl.BlockSpec((tm, D_DIFF), cur)] + [full(w) for w in ws],
        out_specs=[pl.BlockSpec((tm, D_MODEL), cur), pl.BlockSpec((tm * ROW_TILES, LANES), cur),
                   pl.BlockSpec((tm, LANES), prev), pl.BlockSpec((tm, LANES), prev),
                   pl.BlockSpec((1, LANES), lambda t: (0, 0))],
        out_shape=[jax.ShapeDtypeStruct((T, D_MODEL), F32), jax.ShapeDtypeStruct((T * ROW_TILES, LANES), F32),
                   jax.ShapeDtypeStruct((T, LANES), F32), jax.ShapeDtypeStruct((T, LANES), jnp.int32),
                   jax.ShapeDtypeStruct((1, LANES), F32)],
        scratch_shapes=[pltpu.VMEM((1, LANES), F32)] + [pltpu.VMEM((tm, LANES), F32)] * 4,
        compiler_params=pltpu.CompilerParams(dimension_semantics=("arbitrary",), vmem_limit_bytes=VMEM_LIMIT),
        name="mix_router",
    )(x2, a2, b2, *ws)


def _route_plan(route, count, tile_rows):
    T = route.shape[0]
    P = 2 * T
    nt = P // tile_rows
    i32 = jnp.int32
    eid = jnp.arange(N_EXPERTS, dtype=i32)
    counts = count[0, :N_EXPERTS].astype(i32)
    seg_end = jnp.sum(jnp.where(eid[None, :] <= eid[:, None], counts[None, :], 0), axis=1)
    seg_start = seg_end - counts
    picks = route[:, 0:2]
    dest = jnp.sum(jnp.where(picks[:, :, None] == eid, seg_start, 0), axis=-1) + route[:, 2:4]

    tile_start = jnp.arange(nt, dtype=i32) * tile_rows
    slot_of_tile = jnp.arange(nt, dtype=i32) + jnp.sum(seg_start[None, :] < tile_start[:, None], axis=1).astype(i32)
    slot_of_seg = eid + jnp.minimum(seg_start // tile_rows + 1, nt)
    vals = jnp.concatenate([tile_start, seg_start])
    slots = jnp.concatenate([slot_of_tile, slot_of_seg])
    n_items = nt + N_EXPERTS
    bounds = jnp.sum(jnp.where(slots[None, :] == jnp.arange(n_items, dtype=i32)[:, None], vals[None, :], 0), axis=1)
    nxt = jnp.concatenate([bounds[1:], jnp.full((1,), P, i32)])
    tile = jnp.minimum(bounds // tile_rows, nt - 1)
    lo = bounds - tile * tile_rows
    hi = nxt - tile * tile_rows
    expert = jnp.minimum(jnp.sum(seg_end[None, :] <= bounds[:, None], axis=1).astype(i32), N_EXPERTS - 1)
    return dest.astype(i32), (tile, expert, lo, hi)


def _dispatch_kernel(dest_ref, hn_ref, xs_hbm, sem, *, tm):
    def issue(r, carry):
        for pick in range(2):
            pltpu.make_async_copy(_token_tile(hn_ref, r), _token_tile(xs_hbm, dest_ref[0, 2 * r + pick]),
                                  sem).start(priority=pick)
        return carry

    lax.fori_loop(0, tm, issue, 0, unroll=4)
    for _ in range(2):
        pltpu.make_async_copy(hn_ref, xs_hbm.at[pl.ds(0, tm * ROW_TILES)], sem).wait()


def _dispatch_call(dest, hn, tm):
    T = hn.shape[0] // ROW_TILES
    dest3 = dest.reshape(T // tm, 1, 2 * tm)
    return pl.pallas_call(
        functools.partial(_dispatch_kernel, tm=tm),
        grid=(T // tm,),
        in_specs=[pl.BlockSpec((None, 1, 2 * tm), lambda i: (i, 0, 0), memory_space=pltpu.SMEM),
                  pl.BlockSpec((tm * ROW_TILES, LANES), lambda i: (i, 0))],
        out_specs=pl.BlockSpec(memory_space=pl.ANY),
        out_shape=jax.ShapeDtypeStruct((2 * T * ROW_TILES, LANES), F32),
        scratch_shapes=[pltpu.SemaphoreType.DMA],
        compiler_params=pltpu.CompilerParams(dimension_semantics=("arbitrary",)),
        name="moe_dispatch",
    )(dest3, hn)


def _expert_kernel(tile_ref, exp_ref, lo_ref, hi_ref, xs_ref, wg_ref, wu_ref, wd_ref, ys_ref,
                   wg_bf, wu_bf, wd_bf):
    w = pl.program_id(0)
    lo = lo_ref[w]
    hi = hi_ref[w]
    before = jnp.maximum(w - 1, 0)
    first_visit = (w == 0) | (tile_ref[w] != tile_ref[before])

    @pl.when(first_visit)
    def _():
        ys_ref[...] = jnp.zeros_like(ys_ref)

    @pl.when((w == 0) | (exp_ref[w] != exp_ref[before]))
    def _():
        wg_bf[...] = wg_ref[...].astype(BF16)
        wu_bf[...] = wu_ref[...].astype(BF16)
        wd_bf[...] = wd_ref[...].astype(BF16)

    @pl.when(hi > lo)
    def _():
        x = _load_row_tiles(xs_ref).astype(BF16)
        g = _dot(x, wg_bf[...])
        h = (g * jax.nn.sigmoid(g) * _dot(x, wu_bf[...])).astype(BF16)
        y = _dot(h, wd_bf[...])
        row = lax.broadcasted_iota(jnp.int32, (y.shape[0], 1), 0)
        _store_row_tiles(ys_ref, jnp.where((row >= lo) & (row < hi), y, _load_row_tiles(ys_ref)))


def _expert_call(plan, xs, packed, tm):
    n_items = plan[0].shape[0]
    by_tile = lambda w, tile, exp, lo, hi: (tile[w], 0)
    by_expert = lambda w, tile, exp, lo, hi: (exp[w], 0, 0)
    grid_spec = pltpu.PrefetchScalarGridSpec(
        num_scalar_prefetch=4,
        grid=(n_items,),
        in_specs=[pl.BlockSpec((tm * ROW_TILES, LANES), by_tile),
                  pl.BlockSpec((None, D_MODEL, EXPERT_FF), by_expert),
                  pl.BlockSpec((None, D_MODEL, EXPERT_FF), by_expert),
                  pl.BlockSpec((None, EXPERT_FF, D_MODEL), by_expert)],
        out_specs=pl.BlockSpec((tm * ROW_TILES, LANES), by_tile),
        scratch_shapes=[pltpu.VMEM((D_MODEL, EXPERT_FF), BF16), pltpu.VMEM((D_MODEL, EXPERT_FF), BF16),
                        pltpu.VMEM((EXPERT_FF, D_MODEL), BF16)],
    )
    return pl.pallas_call(
        _expert_kernel,
        grid_spec=grid_spec,
        out_shape=jax.ShapeDtypeStruct(xs.shape, F32),
        compiler_params=pltpu.CompilerParams(dimension_semantics=("arbitrary",), vmem_limit_bytes=VMEM_LIMIT),
        name="moe_experts",
    )(*plan, xs, packed["wg"], packed["wu"], packed["wd"])


def _combine_kernel(dest_ref, dest_next_ref, ys_hbm, xn_ref, gate_ref, fin_ref, y_ref, buf, sems, *, tm):
    i = pl.program_id(0)

    def start_gather(dests, slot):
        def issue(r, carry):
            for pick in range(2):
                pltpu.make_async_copy(_token_tile(ys_hbm, dests[0, 2 * r + pick]),
                                      _token_tile(buf.at[slot, pick], r), sems.at[slot]).start(priority=pick)
            return carry
        lax.fori_loop(0, tm, issue, 0, unroll=4)

    def finish(slot):
        for pick in range(2):
            pltpu.make_async_copy(ys_hbm.at[pl.ds(0, tm * ROW_TILES)], buf.at[slot, pick], sems.at[slot]).wait()
        gate = gate_ref[...]
        y = (xn_ref[...] + gate[:, 0:1] * _load_row_tiles(buf.at[slot, 0])
             + gate[:, 1:2] * _load_row_tiles(buf.at[slot, 1]))
        y_ref[...] = _rms(y, fin_ref[...])

    @pl.when(i == 0)
    def _():
        start_gather(dest_ref, 0)

    for slot in range(2):
        @pl.when(i % 2 == slot)
        def _(slot=slot):
            @pl.when(i + 1 < pl.num_programs(0))
            def _():
                start_gather(dest_next_ref, 1 - slot)
            finish(slot)


def _combine_call(dest, ys, xn, gate, packed, tm):
    T = xn.shape[0]
    n_tiles = T // tm
    dest3 = dest.reshape(n_tiles, 1, 2 * tm)
    row = lambda i: (i, 0)
    dest_spec = lambda index: pl.BlockSpec((None, 1, 2 * tm), index, memory_space=pltpu.SMEM)
    return pl.pallas_call(
        functools.partial(_combine_kernel, tm=tm),
        grid=(n_tiles,),
        in_specs=[dest_spec(lambda i: (i, 0, 0)), dest_spec(lambda i: (jnp.minimum(i + 1, n_tiles - 1), 0, 0)),
                  pl.BlockSpec(memory_space=pl.ANY),
                  pl.BlockSpec((tm, D_MODEL), row), pl.BlockSpec((tm, LANES), row),
                  pl.BlockSpec((1, D_MODEL), lambda i: (0, 0))],
        out_specs=pl.BlockSpec((tm, D_MODEL), row),
        out_shape=jax.ShapeDtypeStruct((T, D_MODEL), F32),
        scratch_shapes=[pltpu.VMEM((2, 2, tm * ROW_TILES, LANES), F32), pltpu.SemaphoreType.DMA((2,))],
        compiler_params=pltpu.CompilerParams(dimension_semantics=("arbitrary",), vmem_limit_bytes=VMEM_LIMIT),
        name="moe_combine",
    )(dest3, dest3, ys, xn, gate, packed["final_norm"])


def _pad_blocks(w, block, to):
    K = w.shape[0]
    wb = w.reshape(K, -1, block)
    return jnp.pad(wb, ((0, 0), (0, 0), (0, to - block))).reshape(K, -1)


def _pack_weights(attn_norm, w_in, q_norm, w_uq, kv_norm, w_ukv, w_out, ffn_norm, router_group,
                  router_group_bias, router_expert, router_expert_bias, w_gate, w_up, w_down, final_norm):
    l = 0
    wi = w_in[l]
    o_q, o_kv, o_pe = 0, Q_RANK, Q_RANK + KV_RANK
    o_dq = o_pe + MLA_ROPE
    o_dk = o_dq + D_DIFF
    o_dv = o_dk + D_DIFF
    w1 = jnp.concatenate([wi[:, o_q:o_dq], jnp.zeros((D_MODEL, LANES - MLA_ROPE), F32), wi[:, o_dq:o_dv]],
                         axis=1).astype(BF16)
    wdvt = wi[:, o_dv:].T.astype(BF16)
    wq = _pad_blocks(w_uq[l], MLA_NOPE + MLA_ROPE, LANES).astype(BF16)

    wkv3 = w_ukv[l].reshape(KV_RANK, MLA_HEADS, MLA_NOPE + MLA_V)
    padk = lambda a: jnp.pad(a, ((0, 0), (0, 0), (0, LANES - a.shape[-1]))).reshape(KV_RANK, D_MLA_PAD)
    wvt = padk(wkv3[:, :, MLA_NOPE:]).T.astype(BF16)
    place = np.zeros((LANES, MLA_HEADS, LANES), np.float32)
    for r in range(MLA_ROPE):
        place[r, :, MLA_NOPE + r] = 1.0
    place = jnp.asarray(place.reshape(LANES, D_MLA_PAD))
    wk = jnp.concatenate([padk(wkv3[:, :, :MLA_NOPE]), place, place], axis=0).astype(BF16)

    wo = w_out[l]
    n_mla = MLA_HEADS * MLA_V
    woa = jnp.pad(wo[:n_mla].reshape(MLA_HEADS, MLA_V, D_MODEL),
                  ((0, 0), (0, LANES - MLA_V), (0, 0))).reshape(D_MLA_PAD, D_MODEL).astype(BF16)
    padl = lambda a: jnp.pad(a, ((0, 0), (0, LANES - a.shape[-1])))
    return {
        "attn_norm": attn_norm[l][None], "w1": w1, "q_norm": q_norm[l][None], "wq": wq,
        "kv_norm": kv_norm[l][None], "wk": wk, "wvt": wvt, "wdvt": wdvt,
        "woa": woa, "wob": wo[n_mla:].astype(BF16), "ffn_norm": ffn_norm[l][None],
        "wrg": padl(router_group[l]).astype(BF16), "brg": padl(router_group_bias[l][None]),
        "wre": padl(router_expert[l]).astype(BF16), "bre": padl(router_expert_bias[l][None]),
        "wg": w_gate[l], "wu": w_up[l], "wd": w_down[l],
        "final_norm": final_norm[None],
    }


def _cos_sin(seq, rot):
    half = rot // 2
    inv = ROPE_THETA ** (-jnp.arange(half, dtype=F32) / half)
    ang = jnp.arange(seq, dtype=F32)[:, None] * inv[None, :]
    return jnp.cos(ang), jnp.sin(ang)


N_ROPE_TABLES = 4


def _rope_tables(seq):
    lane = np.arange(LANES)

    def table(offsets, half, cos, sin, scale=1.0):
        first = np.zeros(LANES, bool)
        second = np.zeros(LANES, bool)
        for off in offsets:
            first |= (lane >= off) & (lane < off + half)
            second |= (lane >= off + half) & (lane < off + 2 * half)
        cos_l = jnp.tile(cos, (1, LANES // half))
        sin_l = jnp.tile(sin, (1, LANES // half))
        parts = [jnp.where(first | second, cos_l, 1.0), jnp.where(second, sin_l, 0.0), jnp.where(first, -sin_l, 0.0)]
        return [p * scale for p in parts]

    c, s = _cos_sin(seq, MLA_ROPE)
    tq = table([MLA_NOPE], MLA_ROPE // 2, c, s, (MLA_NOPE + MLA_ROPE) ** -0.5 * LOG2E)
    tpe = table([0], MLA_ROPE // 2, c, s)
    c, s = _cos_sin(seq, DIFF_ROT)
    tdq = table([0, DIFF_HEAD_DIM], DIFF_ROT // 2, c, s, DIFF_HEAD_DIM ** -0.5 * LOG2E)
    tdk = table([0, DIFF_HEAD_DIM], DIFF_ROT // 2, c, s)
    return jnp.concatenate(tq + tpe + tdq + tdk, axis=1)


def _forward(x, packed, tables, lams, subln):
    B, S, D = x.shape
    T = B * S
    x2 = x.reshape(T, D)
    q, k, vt, dq, dk, dvt = _proj_call(x2, packed, tables, S, tm=512)
    r3 = lambda a: a.reshape(B, S, a.shape[-1])
    a = _mla_call(r3(q), r3(k), vt, tq=512)
    b = _diff_call(lams, subln, r3(dq), r3(dk), dvt, tq=512)
    xn, hn, gate, route, count = _mix_call(x2, a.reshape(T, -1), b.reshape(T, -1), packed, tm=512)
    dest, plan = _route_plan(route, count, tile_rows=256)
    xs = _dispatch_call(dest, hn, tm=512)
    ys = _expert_call(plan, xs, packed, tm=256)
    y = _combine_call(dest, ys, xn, gate, packed, tm=512)
    return y.reshape(B, S, D)


def kernel(x_prompt, x_sample, attn_norm, w_in, q_norm, w_uq, kv_norm, w_ukv, diff_lq1, diff_lk1, diff_lq2,
           diff_lk2, diff_subln, w_out, ffn_norm, router_group, router_group_bias, router_expert,
           router_expert_bias, w_gate, w_up, w_down, final_norm):
    packed = _pack_weights(attn_norm, w_in, q_norm, w_uq, kv_norm, w_ukv, w_out, ffn_norm, router_group,
                           router_group_bias, router_expert, router_expert_bias, w_gate, w_up, w_down,
                           final_norm)
    lams = (diff_lq1, diff_lk1, diff_lq2, diff_lk2)
    subln = diff_subln
    tables = _rope_tables(max(x_prompt.shape[1], x_sample.shape[1]))
    return (_forward(x_prompt, packed, tables, lams, subln), _forward(x_sample, packed, tables, lams, subln))
```

```python
import functools
import math

import numpy as np
import jax
import jax.numpy as jnp
from jax import lax
from jax.experimental import pallas as pl
from jax.experimental.pallas import tpu as pltpu

D_MODEL = 1024
MLA_HEADS = 8
MLA_NOPE = 64
MLA_ROPE = 32
MLA_V = 64
Q_RANK = 256
KV_RANK = 128
DIFF_HEADS = 4
DIFF_HEAD_DIM = 64
DIFF_ROT = DIFF_HEAD_DIM // 4
ROPE_THETA = 500000.0
N_GROUPS = 4
EXPERTS_PER_GROUP = 8
N_EXPERTS = N_GROUPS * EXPERTS_PER_GROUP
EXPERT_FF = 256
EPS = 1e-6
LAMBDA_INIT = 0.8 - 0.6 * math.exp(-0.3 * 0)

LANES = 128
D_DIFF = DIFF_HEADS * 2 * DIFF_HEAD_DIM
D_MLA_PAD = MLA_HEADS * LANES
LOG2E = math.log2(math.e)
VMEM_LIMIT = 56 * 1024 * 1024

BF16 = jnp.bfloat16
F32 = jnp.float32

_C_Q = 0
_C_KV = _C_Q + Q_RANK
_C_PE = _C_KV + KV_RANK
_C_DQ = _C_PE + LANES
_C_DK = _C_DQ + D_DIFF
_C_END = _C_DK + D_DIFF


def _rms(x, g):
    return x * lax.rsqrt(jnp.mean(x * x, axis=-1, keepdims=True) + EPS) * g


def _dot(a, b):
    return jnp.dot(a, b, preferred_element_type=F32)


def _dot_nt(a, b):
    return lax.dot_general(a, b, (((1,), (1,)), ((), ())), preferred_element_type=F32)


ROW_TILES = D_MODEL // LANES
assert ROW_TILES == 8


def _token_tile(ref, t):
    return ref.at[pl.ds(pl.multiple_of(t * ROW_TILES, ROW_TILES), ROW_TILES)]


def _store_row_tiles(ref, x):
    for j in range(ROW_TILES):
        ref[pl.ds(j, x.shape[0], stride=ROW_TILES), :] = x[:, j * LANES:(j + 1) * LANES]


def _load_row_tiles(ref):
    rows = ref.shape[0] // ROW_TILES
    return jnp.concatenate([ref[pl.ds(j, rows, stride=ROW_TILES), :] for j in range(ROW_TILES)], axis=1)


ROT_X1_LANES = (0, 16)
DIFF_MAP_A_LANES = ((0, 8), (16, 72))


def _rotary(x, table):
    return x * table[:, 0:LANES] + pltpu.roll(x, LANES // 2, 1) * table[:, LANES:2 * LANES]


def _proj_kernel(x_ref, an_ref, w1_ref, qn_ref, wq_ref, kvn_ref, wk_ref, wvt_ref, wdvt_ref,
                 tq_ref, tpe_ref, tdq_ref, tdk_ref,
                 q_ref, k_ref, vt_ref, dq_ref, dk_ref, dvt_ref):
    hn = _rms(x_ref[...], an_ref[...]).astype(BF16)

    def proj(lo, hi):
        return _dot(hn, w1_ref[:, lo:hi])

    cq_raw = proj(_C_Q, _C_KV)
    kv_pe = proj(_C_KV, _C_DQ)
    dq, dk = proj(_C_DQ, _C_DK), proj(_C_DK, _C_END)
    dvt_ref[...] = _dot_nt(wdvt_ref[...], hn).astype(BF16)

    cq = _rms(cq_raw, qn_ref[...]).astype(BF16)
    tq = tq_ref[...]
    qa = _dot(cq, wq_ref[...])
    for h in range(MLA_HEADS):
        blk = slice(h * LANES, (h + 1) * LANES)
        q_ref[:, blk] = _rotary(qa[:, blk], tq).astype(BF16)

    ckv = _rms(kv_pe[:, 0:KV_RANK], kvn_ref[...]).astype(BF16)
    pe = _rotary(kv_pe[:, KV_RANK:], tpe_ref[...])
    pe_hi = pe.astype(BF16)
    pe_lo = (pe - pe_hi.astype(F32)).astype(BF16)
    k_all = _dot(jnp.concatenate([ckv, pe_hi, pe_lo], axis=1), wk_ref[...])
    k_ref[...] = k_all.astype(BF16)
    vt = _dot_nt(wvt_ref[...], ckv)
    ones_row = (lax.broadcasted_iota(jnp.int32, vt.shape, 0) & (LANES - 1)) == MLA_V
    vt_ref[...] = jnp.where(ones_row, 1.0, vt).astype(BF16)

    tdq, tdk = tdq_ref[...], tdk_ref[...]
    for h in range(DIFF_HEADS):
        blk = slice(h * LANES, (h + 1) * LANES)
        dq_ref[:, blk] = _rotary(dq[:, blk], tdq).astype(BF16)
        dk_ref[:, blk] = _rotary(dk[:, blk], tdk).astype(BF16)


def _proj_call(x2, packed, tables, seq, tm):
    T = x2.shape[0]
    nper = seq // tm
    B = T // seq
    row = lambda i: (i, 0)
    const = lambda i: (0, 0)
    tab = lambda i: (i % nper, 0)
    col = lambda i: (i // nper, 0, i % nper)
    full = lambda a: pl.BlockSpec(a.shape, const)
    tspecs = [pl.BlockSpec((tm, 2 * LANES), lambda i, j=j: (i % nper, j)) for j in range(N_ROPE_TABLES)]
    tok = lambda d: (jax.ShapeDtypeStruct((T, d), BF16), pl.BlockSpec((tm, d), row))
    tr = lambda d: (jax.ShapeDtypeStruct((B, d, seq), BF16), pl.BlockSpec((None, d, tm), col))
    outs, ospecs = zip(tok(D_MLA_PAD), tok(D_MLA_PAD), tr(D_MLA_PAD), tok(D_DIFF), tok(D_DIFF), tr(D_DIFF))
    ws = [packed[k] for k in ("attn_norm", "w1", "q_norm", "wq", "kv_norm", "wk", "wvt", "wdvt")]
    return pl.pallas_call(
        _proj_kernel,
        grid=(T // tm,),
        in_specs=[pl.BlockSpec((tm, D_MODEL), row)] + [full(w) for w in ws] + tspecs,
        out_specs=ospecs,
        out_shape=outs,
        compiler_params=pltpu.CompilerParams(dimension_semantics=("arbitrary",), vmem_limit_bytes=VMEM_LIMIT),
        name="proj",
    )(x2, *ws, *([tables] * N_ROPE_TABLES))


def _block_index_maps(n_heads, n_qblocks, n_blocks):
    per_batch = n_heads * n_qblocks

    def split(blk):
        b = blk // per_batch
        r = blk % per_batch
        return b, r // n_qblocks, r % n_qblocks

    def rows(blk):
        b, h, i = split(blk)
        return b, i, h

    def keys(blk):
        b, h, _ = split(blk)
        return b, 0, h

    def values_t(blk):
        b, h, _ = split(blk)
        return b, h, 0

    cur = lambda t: jnp.minimum(t, n_blocks - 1)
    prev = lambda t: jnp.maximum(t - 1, 0)
    return (lambda t: rows(cur(t)), lambda t: keys(cur(t)), lambda t: rows(prev(t)), lambda t: values_t(prev(t)))


KEY_CHUNK = 256


def _key_chunk(c):
    return slice(c * KEY_CHUNK, (c + 1) * KEY_CHUNK)


def _running_max(m, st):
    mc = jnp.max(st, axis=0, keepdims=True)
    return mc if m is None else jnp.maximum(m, mc)


def _accumulate(acc, part):
    return part if acc is None else acc + part


def _pipelined_steps(step, zero_scratch):
    t = pl.program_id(0)

    @pl.when(t == 0)
    def _():
        zero_scratch()

    @pl.when(t % 2 == 0)
    def _():
        step(1, 0)

    @pl.when(t % 2 == 1)
    def _():
        step(0, 1)


def _mla_kernel(q_ref, k_ref, vt_ref, o_ref, s0_ref, s1_ref, m0_ref, m1_ref):
    s_refs, m_refs = (s0_ref, s1_ref), (m0_ref, m1_ref)

    def step(src, dst):
        q = q_ref[...]
        m_src = m_refs[src][...]
        scores = lambda c: _dot_nt(k_ref[_key_chunk(c), :], q)
        n_chunks = k_ref.shape[0] // KEY_CHUNK
        ahead, ot, m_run = scores(0), None, None
        for c in range(n_chunks):
            keys = _key_chunk(c)
            st, ahead = ahead, (scores(c + 1) if c + 1 < n_chunks else None)
            s_refs[dst][keys, :] = st
            m_run = _running_max(m_run, st)
            pt = jnp.exp2(s_refs[src][keys, :] - m_src).astype(BF16)
            ot = _accumulate(ot, _dot(vt_ref[:, keys], pt))
        m_refs[dst][...] = m_run
        o_ref[...] = (ot / ot[MLA_V:MLA_V + 1, :]).T.astype(BF16)

    def zero_scratch():
        s1_ref[...] = jnp.zeros_like(s1_ref)
        m1_ref[...] = jnp.zeros_like(m1_ref)

    _pipelined_steps(step, zero_scratch)


def _mla_call(q, k, vt, tq):
    B, S, _ = q.shape
    n_blocks = B * MLA_HEADS * (S // tq)
    q_map, k_map, o_map, vt_map = _block_index_maps(MLA_HEADS, S // tq, n_blocks)
    return pl.pallas_call(
        _mla_kernel,
        grid=(n_blocks + 1,),
        in_specs=[pl.BlockSpec((None, tq, LANES), q_map), pl.BlockSpec((None, S, LANES), k_map),
                  pl.BlockSpec((None, LANES, S), vt_map)],
        out_specs=pl.BlockSpec((None, tq, LANES), o_map),
        out_shape=jax.ShapeDtypeStruct((B, S, D_MLA_PAD), BF16),
        scratch_shapes=[pltpu.VMEM((S, tq), F32)] * 2 + [pltpu.VMEM((1, tq), F32)] * 2,
        compiler_params=pltpu.CompilerParams(dimension_semantics=("arbitrary",), vmem_limit_bytes=VMEM_LIMIT),
        name="mla_attn",
    )(q, k, vt)


def _diff_kernel(lq1_ref, lk1_ref, lq2_ref, lk2_ref, g_ref, q_ref, k_ref, vt_ref, o_ref,
                 sa0_ref, sa1_ref, sb0_ref, sb1_ref, ma0_ref, ma1_ref, mb0_ref, mb1_ref):
    sa_refs, sb_refs = (sa0_ref, sa1_ref), (sb0_ref, sb1_ref)
    ma_refs, mb_refs = (ma0_ref, ma1_ref), (mb0_ref, mb1_ref)

    def step(src, dst):
        lam = (jnp.exp(jnp.sum(lq1_ref[...] * lk1_ref[...], axis=-1, keepdims=True))
               - jnp.exp(jnp.sum(lq2_ref[...] * lk2_ref[...], axis=-1, keepdims=True)) + LAMBDA_INIT)
        q = q_ref[...]
        lane = lax.broadcasted_iota(jnp.int32, q.shape, 1)
        (a0, a1), (a2, a3) = DIFF_MAP_A_LANES
        first = ((lane >= a0) & (lane < a1)) | ((lane >= a2) & (lane < a3))
        zero = jnp.zeros_like(q)
        qs = (jnp.where(first, q, zero), jnp.where(first, zero, q))
        s_maps, m_maps = (sa_refs, sb_refs), (ma_refs, mb_refs)
        m_src = [m_refs[src][...] for m_refs in m_maps]
        scores = lambda c: [_dot_nt(k_ref[_key_chunk(c), :], qc) for qc in qs]
        n_chunks = k_ref.shape[0] // KEY_CHUNK
        ahead = scores(0)
        m_run, tot, ot = [None, None], [None, None], [None, None]
        for c in range(n_chunks):
            keys = _key_chunk(c)
            now, ahead = ahead, (scores(c + 1) if c + 1 < n_chunks else None)
            vt = vt_ref[:, keys]
            for i in range(2):
                s_maps[i][dst][keys, :] = now[i]
                m_run[i] = _running_max(m_run[i], now[i])
                e = jnp.exp2(s_maps[i][src][keys, :] - m_src[i])
                tot[i] = _accumulate(tot[i], jnp.sum(e, axis=0, keepdims=True))
                ot[i] = _accumulate(ot[i], _dot(vt, e.astype(BF16)))
        for i in range(2):
            m_maps[i][dst][...] = m_run[i]
        o = ot[0] * (1.0 / tot[0]) - ot[1] * (lam / tot[1])
        o_ref[...] = (_rms(o.T, g_ref[...]) * (1.0 - LAMBDA_INIT)).astype(BF16)

    def zero_scratch():
        for ref in (sa1_ref, sb1_ref, ma1_ref, mb1_ref):
            ref[...] = jnp.zeros_like(ref)

    _pipelined_steps(step, zero_scratch)


def _diff_call(lams, subln, dq, dk, dvt, tq):
    B, S, _ = dq.shape
    n_blocks = B * DIFF_HEADS * (S // tq)
    q_map, k_map, o_map, vt_map = _block_index_maps(DIFF_HEADS, S // tq, n_blocks)
    small = lambda a: pl.BlockSpec(a.shape, lambda t: (0, 0))
    return pl.pallas_call(
        _diff_kernel,
        grid=(n_blocks + 1,),
        in_specs=[small(a) for a in lams] + [small(subln), pl.BlockSpec((None, tq, LANES), q_map),
                                             pl.BlockSpec((None, S, LANES), k_map),
                                             pl.BlockSpec((None, LANES, S), vt_map)],
        out_specs=pl.BlockSpec((None, tq, LANES), o_map),
        out_shape=jax.ShapeDtypeStruct((B, S, D_DIFF), BF16),
        scratch_shapes=[pltpu.VMEM((S, tq), F32)] * 4 + [pltpu.VMEM((1, tq), F32)] * 4,
        compiler_params=pltpu.CompilerParams(dimension_semantics=("arbitrary",), vmem_limit_bytes=VMEM_LIMIT),
        name="diff_attn",
    )(*lams, subln, dq, dk, dvt)


def _mix_kernel(x_ref, a_ref, b_ref, woa_ref, wob_ref, fn_ref, wrg_ref, brg_ref, wre_ref, bre_ref, earlier_ref,
                xn_ref, hn_ref, gate_ref, route_ref, count_ref, base_ref, lg0_ref, lg1_ref, le0_ref, le1_ref):
    t = pl.program_id(0)
    group_refs, expert_refs = (lg0_ref, lg1_ref), (le0_ref, le1_ref)

    def step(src, dst):
        xn = x_ref[...] + _dot(a_ref[...], woa_ref[...]) + _dot(b_ref[...], wob_ref[...])
        xn_ref[...] = xn
        hn32 = _rms(xn, fn_ref[...])
        _store_row_tiles(hn_ref, hn32)
        hn = hn32.astype(BF16)

        lane = lax.broadcasted_iota(jnp.int32, (xn.shape[0], LANES), 1)
        neg = jnp.float32(-jnp.inf)
        big = jnp.int32(LANES)

        def first_lane_of_max(vals):
            top = jnp.max(vals, axis=-1, keepdims=True)
            return top, jnp.min(jnp.where(vals == top, lane, big), axis=-1, keepdims=True)

        gl = jnp.where(lane < N_GROUPS, group_refs[src][...], neg)
        ge = jnp.exp(gl - jnp.max(gl, axis=-1, keepdims=True))
        gp = ge / jnp.sum(ge, axis=-1, keepdims=True)
        p_g, g_idx = first_lane_of_max(gp)

        in_group = (lane < N_EXPERTS) & ((lane & -EXPERTS_PER_GROUP) == g_idx * EXPERTS_PER_GROUP)
        wl = jnp.where(in_group, expert_refs[src][...], neg)
        we = jnp.exp(wl - jnp.max(wl, axis=-1, keepdims=True))
        wp = jnp.where(in_group, we / jnp.sum(we, axis=-1, keepdims=True), -1.0)
        w_a, i_a = first_lane_of_max(wp)
        w_b, i_b = first_lane_of_max(jnp.where(lane == i_a, -1.0, wp))
        tot = w_a + w_b
        gate_ref[...] = (jnp.where(lane == 0, p_g * (w_a / tot), 0.0)
                         + jnp.where(lane == 1, p_g * (w_b / tot), 0.0))

        picked = jnp.where((lane == i_a) | (lane == i_b), 1.0, 0.0)
        before = base_ref[...] + _dot(earlier_ref[...], picked.astype(BF16))
        rank_a = jnp.sum(jnp.where(lane == i_a, before, 0.0), axis=-1, keepdims=True).astype(jnp.int32)
        rank_b = jnp.sum(jnp.where(lane == i_b, before, 0.0), axis=-1, keepdims=True).astype(jnp.int32)
        zero = jnp.zeros_like(lane)
        route_ref[...] = (jnp.where(lane == 0, i_a, zero) + jnp.where(lane == 1, i_b, zero)
                          + jnp.where(lane == 2, rank_a, zero) + jnp.where(lane == 3, rank_b, zero))
        base_ref[...] += jnp.where(t > 0, jnp.sum(picked, axis=0, keepdims=True), 0.0)
        count_ref[...] = base_ref[...]

        group_refs[dst][...] = _dot(hn, wrg_ref[...]) + brg_ref[...]
        expert_refs[dst][...] = _dot(hn, wre_ref[...]) + bre_ref[...]

    def zero_scratch():
        for ref in (base_ref, lg1_ref, le1_ref):
            ref[...] = jnp.zeros_like(ref)

    _pipelined_steps(step, zero_scratch)


def _mix_call(x2, a2, b2, packed, tm):
    T = x2.shape[0]
    n_tiles = T // tm
    cur = lambda t: (jnp.minimum(t, n_tiles - 1), 0)
    prev = lambda t: (jnp.maximum(t - 1, 0), 0)
    full = lambda a: pl.BlockSpec(a.shape, lambda t: (0, 0))
    ws = [packed[k] for k in ("woa", "wob", "ffn_norm", "wrg", "brg", "wre", "bre")]
    ws.append(jnp.asarray(np.tril(np.ones((tm, tm), np.float32), -1), BF16))
    return pl.pallas_call(
        _mix_kernel,
        grid=(n_tiles + 1,),
        in_specs=[pl.BlockSpec((tm, D_MODEL), cur), pl.BlockSpec((tm, D_MLA_PAD), cur),
                  pl.BlockSpec((tm, D_DIFF), cur)] + [full(w) for w in ws],
        out_specs=[pl.BlockSpec((tm, D_MODEL), cur), pl.BlockSpec((tm * ROW_TILES, LANES), cur),
                   pl.BlockSpec((tm, LANES), prev), pl.BlockSpec((tm, LANES), prev),
                   pl.BlockSpec((1, LANES), lambda t: (0, 0))],
        out_shape=[jax.ShapeDtypeStruct((T, D_MODEL), F32), jax.ShapeDtypeStruct((T * ROW_TILES, LANES), F32),
                   jax.ShapeDtypeStruct((T, LANES), F32), jax.ShapeDtypeStruct((T, LANES), jnp.int32),
                   jax.ShapeDtypeStruct((1, LANES), F32)],
        scratch_shapes=[pltpu.VMEM((1, LANES), F32)] + [pltpu.VMEM((tm, LANES), F32)] * 4,
        compiler_params=pltpu.CompilerParams(dimension_semantics=("arbitrary",), vmem_limit_bytes=VMEM_LIMIT),
        name="mix_router",
    )(x2, a2, b2, *ws)


def _route_plan(route, count, tile_rows):
    T = route.shape[0]
    P = 2 * T
    nt = P // tile_rows
    i32 = jnp.int32
    eid = jnp.arange(N_EXPERTS, dtype=i32)
    counts = count[0, :N_EXPERTS].astype(i32)
    seg_end = jnp.sum(jnp.where(eid[None, :] <= eid[:, None], counts[None, :], 0), axis=1)
    seg_start = seg_end - counts
    picks = route[:, 0:2]
    dest = jnp.sum(jnp.where(picks[:, :, None] == eid, seg_start, 0), axis=-1) + route[:, 2:4]

    tile_start = jnp.arange(nt, dtype=i32) * tile_rows
    slot_of_tile = jnp.arange(nt, dtype=i32) + jnp.sum(seg_start[None, :] < tile_start[:, None], axis=1).astype(i32)
    slot_of_seg = eid + jnp.minimum(seg_start // tile_rows + 1, nt)
    vals = jnp.concatenate([tile_start, seg_start])
    slots = jnp.concatenate([slot_of_tile, slot_of_seg])
    n_items = nt + N_EXPERTS
    bounds = jnp.sum(jnp.where(slots[None, :] == jnp.arange(n_items, dtype=i32)[:, None], vals[None, :], 0), axis=1)
    nxt = jnp.concatenate([bounds[1:], jnp.full((1,), P, i32)])
    tile = jnp.minimum(bounds // tile_rows, nt - 1)
    lo = bounds - tile * tile_rows
    hi = nxt - tile * tile_rows
    expert = jnp.minimum(jnp.sum(seg_end[None, :] <= bounds[:, None], axis=1).astype(i32), N_EXPERTS - 1)
    return dest.astype(i32), (tile, expert, lo, hi)


def _dispatch_kernel(dest_ref, hn_ref, xs_hbm, sem, *, tm):
    def issue(r, carry):
        for pick in range(2):
            pltpu.make_async_copy(_token_tile(hn_ref, r), _token_tile(xs_hbm, dest_ref[0, 2 * r + pick]),
                                  sem).start(priority=pick)
        return carry

    lax.fori_loop(0, tm, issue, 0, unroll=4)
    for _ in range(2):
        pltpu.make_async_copy(hn_ref, xs_hbm.at[pl.ds(0, tm * ROW_TILES)], sem).wait()


def _dispatch_call(dest, hn, tm):
    T = hn.shape[0] // ROW_TILES
    dest3 = dest.reshape(T // tm, 1, 2 * tm)
    return pl.pallas_call(
        functools.partial(_dispatch_kernel, tm=tm),
        grid=(T // tm,),
        in_specs=[pl.BlockSpec((None, 1, 2 * tm), lambda i: (i, 0, 0), memory_space=pltpu.SMEM),
                  pl.BlockSpec((tm * ROW_TILES, LANES), lambda i: (i, 0))],
        out_specs=pl.BlockSpec(memory_space=pl.ANY),
        out_shape=jax.ShapeDtypeStruct((2 * T * ROW_TILES, LANES), F32),
        scratch_shapes=[pltpu.SemaphoreType.DMA],
        compiler_params=pltpu.CompilerParams(dimension_semantics=("arbitrary",)),
        name="moe_dispatch",
    )(dest3, hn)


def _expert_kernel(tile_ref, exp_ref, lo_ref, hi_ref, xs_ref, wg_ref, wu_ref, wd_ref, ys_ref,
                   wg_bf, wu_bf, wd_bf):
    w = pl.program_id(0)
    lo = lo_ref[w]
    hi = hi_ref[w]
    before = jnp.maximum(w - 1, 0)
    first_visit = (w == 0) | (tile_ref[w] != tile_ref[before])

    @pl.when(first_visit)
    def _():
        ys_ref[...] = jnp.zeros_like(ys_ref)

    @pl.when((w == 0) | (exp_ref[w] != exp_ref[before]))
    def _():
        wg_bf[...] = wg_ref[...].astype(BF16)
        wu_bf[...] = wu_ref[...].astype(BF16)
        wd_bf[...] = wd_ref[...].astype(BF16)

    @pl.when(hi > lo)
    def _():
        x = _load_row_tiles(xs_ref).astype(BF16)
        half = x.shape[0] // 2
        gate_up = [(_dot(xh, wg_bf[...]), _dot(xh, wu_bf[...])) for xh in (x[:half], x[half:])]
        y = jnp.concatenate([_dot((g * jax.nn.sigmoid(g) * u).astype(BF16), wd_bf[...]) for g, u in gate_up], axis=0)
        row = lax.broadcasted_iota(jnp.int32, (y.shape[0], 1), 0)
        _store_row_tiles(ys_ref, jnp.where((row >= lo) & (row < hi), y, _load_row_tiles(ys_ref)))


def _expert_call(plan, xs, packed, tm):
    n_items = plan[0].shape[0]
    by_tile = lambda w, tile, exp, lo, hi: (tile[w], 0)
    by_expert = lambda w, tile, exp, lo, hi: (exp[w], 0, 0)
    grid_spec = pltpu.PrefetchScalarGridSpec(
        num_scalar_prefetch=4,
        grid=(n_items,),
        in_specs=[pl.BlockSpec((tm * ROW_TILES, LANES), by_tile),
                  pl.BlockSpec((None, D_MODEL, EXPERT_FF), by_expert),
                  pl.BlockSpec((None, D_MODEL, EXPERT_FF), by_expert),
                  pl.BlockSpec((None, EXPERT_FF, D_MODEL), by_expert)],
        out_specs=pl.BlockSpec((tm * ROW_TILES, LANES), by_tile),
        scratch_shapes=[pltpu.VMEM((D_MODEL, EXPERT_FF), BF16), pltpu.VMEM((D_MODEL, EXPERT_FF), BF16),
                        pltpu.VMEM((EXPERT_FF, D_MODEL), BF16)],
    )
    return pl.pallas_call(
        _expert_kernel,
        grid_spec=grid_spec,
        out_shape=jax.ShapeDtypeStruct(xs.shape, F32),
        compiler_params=pltpu.CompilerParams(dimension_semantics=("arbitrary",), vmem_limit_bytes=VMEM_LIMIT),
        name="moe_experts",
    )(*plan, xs, packed["wg"], packed["wu"], packed["wd"])


def _combine_kernel(dest_ref, dest_next_ref, ys_hbm, xn_ref, gate_ref, fin_ref, y_ref, buf, sems, *, tm):
    i = pl.program_id(0)

    def start_gather(dests, slot):
        def issue(r, carry):
            for pick in range(2):
                pltpu.make_async_copy(_token_tile(ys_hbm, dests[0, 2 * r + pick]),
                                      _token_tile(buf.at[slot, pick], r), sems.at[slot]).start(priority=pick)
            return carry
        lax.fori_loop(0, tm, issue, 0, unroll=4)

    def finish(slot):
        for pick in range(2):
            pltpu.make_async_copy(ys_hbm.at[pl.ds(0, tm * ROW_TILES)], buf.at[slot, pick], sems.at[slot]).wait()
        gate = gate_ref[...]
        y = (xn_ref[...] + gate[:, 0:1] * _load_row_tiles(buf.at[slot, 0])
             + gate[:, 1:2] * _load_row_tiles(buf.at[slot, 1]))
        y_ref[...] = _rms(y, fin_ref[...])

    @pl.when(i == 0)
    def _():
        start_gather(dest_ref, 0)

    for slot in range(2):
        @pl.when(i % 2 == slot)
        def _(slot=slot):
            @pl.when(i + 1 < pl.num_programs(0))
            def _():
                start_gather(dest_next_ref, 1 - slot)
            finish(slot)


def _combine_call(dest, ys, xn, gate, packed, tm):
    T = xn.shape[0]
    n_tiles = T // tm
    dest3 = dest.reshape(n_tiles, 1, 2 * tm)
    row = lambda i: (i, 0)
    dest_spec = lambda index: pl.BlockSpec((None, 1, 2 * tm), index, memory_space=pltpu.SMEM)
    return pl.pallas_call(
        functools.partial(_combine_kernel, tm=tm),
        grid=(n_tiles,),
        in_specs=[dest_spec(lambda i: (i, 0, 0)), dest_spec(lambda i: (jnp.minimum(i + 1, n_tiles - 1), 0, 0)),
                  pl.BlockSpec(memory_space=pl.ANY),
                  pl.BlockSpec((tm, D_MODEL), row), pl.BlockSpec((tm, LANES), row),
                  pl.BlockSpec((1, D_MODEL), lambda i: (0, 0))],
        out_specs=pl.BlockSpec((tm, D_MODEL), row),
        out_shape=jax.ShapeDtypeStruct((T, D_MODEL), F32),
        scratch_shapes=[pltpu.VMEM((2, 2, tm * ROW_TILES, LANES), F32), pltpu.SemaphoreType.DMA((2,))],
        compiler_params=pltpu.CompilerParams(dimension_semantics=("arbitrary",), vmem_limit_bytes=VMEM_LIMIT),
        name="moe_combine",
    )(dest3, dest3, ys, xn, gate, packed["final_norm"])


def _pack_weights(attn_norm, w_in, q_norm, w_uq, kv_norm, w_ukv, w_out, ffn_norm, router_group,
                  router_group_bias, router_expert, router_expert_bias, w_gate, w_up, w_down, final_norm):
    l = 0
    wi = w_in[l]
    o_q, o_kv, o_pe = 0, Q_RANK, Q_RANK + KV_RANK
    o_dq = o_pe + MLA_ROPE
    o_dk = o_dq + D_DIFF
    o_dv = o_dk + D_DIFF
    cat = lambda parts: jnp.concatenate(parts, axis=-1)
    zeros = lambda like, n: jnp.zeros(like.shape[:-1] + (n,), F32)
    h_rope, h_diff = MLA_ROPE // 2, DIFF_ROT // 2
    gap = LANES // 2 - h_rope

    def diff_tiles(w):
        t = w.reshape(w.shape[0], DIFF_HEADS, 2, DIFF_HEAD_DIM)
        a, b = t[:, :, 0], t[:, :, 1]
        x1, x2, rest = slice(0, h_diff), slice(h_diff, DIFF_ROT), slice(DIFF_ROT, DIFF_HEAD_DIM)
        return cat([a[..., x1], b[..., x1], a[..., rest], a[..., x2], b[..., x2], b[..., rest]]).reshape(w.shape)

    w_pe = wi[:, o_pe:o_dq]
    pe_tile = cat([w_pe[:, :h_rope], zeros(w_pe, gap), w_pe[:, h_rope:], zeros(w_pe, gap)])
    w1 = cat([wi[:, o_q:o_pe], pe_tile, diff_tiles(wi[:, o_dq:o_dk]), diff_tiles(wi[:, o_dk:o_dv])]).astype(BF16)
    wdvt = wi[:, o_dv:].T.astype(BF16)

    def mla_tiles(nope, rope):
        r1 = zeros(nope, h_rope) if rope is None else rope[..., :h_rope]
        r2 = zeros(nope, h_rope) if rope is None else rope[..., h_rope:]
        return cat([r1, nope[..., :gap], r2, nope[..., gap:], zeros(nope, LANES - MLA_NOPE - MLA_ROPE)])

    wq3 = w_uq[l].reshape(Q_RANK, MLA_HEADS, MLA_NOPE + MLA_ROPE)
    wq = mla_tiles(wq3[..., :MLA_NOPE], wq3[..., MLA_NOPE:]).reshape(Q_RANK, D_MLA_PAD).astype(BF16)

    wkv3 = w_ukv[l].reshape(KV_RANK, MLA_HEADS, MLA_NOPE + MLA_V)
    padk = lambda a: jnp.pad(a, ((0, 0), (0, 0), (0, LANES - a.shape[-1]))).reshape(KV_RANK, D_MLA_PAD)
    wvt = padk(wkv3[:, :, MLA_NOPE:]).T.astype(BF16)
    place = np.zeros((LANES, MLA_HEADS, LANES), np.float32)
    for r in range(h_rope):
        for lane_ in (ROT_X1_LANES[0] + r, ROT_X1_LANES[0] + LANES // 2 + r):
            place[lane_, :, lane_] = 1.0
    place = jnp.asarray(place.reshape(LANES, D_MLA_PAD))
    wk_nope = mla_tiles(wkv3[:, :, :MLA_NOPE], None).reshape(KV_RANK, D_MLA_PAD)
    wk = jnp.concatenate([wk_nope, place, place], axis=0).astype(BF16)

    wo = w_out[l]
    n_mla = MLA_HEADS * MLA_V
    woa = jnp.pad(wo[:n_mla].reshape(MLA_HEADS, MLA_V, D_MODEL),
                  ((0, 0), (0, LANES - MLA_V), (0, 0))).reshape(D_MLA_PAD, D_MODEL).astype(BF16)
    padl = lambda a: jnp.pad(a, ((0, 0), (0, LANES - a.shape[-1])))
    return {
        "attn_norm": attn_norm[l][None], "w1": w1, "q_norm": q_norm[l][None], "wq": wq,
        "kv_norm": kv_norm[l][None], "wk": wk, "wvt": wvt, "wdvt": wdvt,
        "woa": woa, "wob": wo[n_mla:].astype(BF16), "ffn_norm": ffn_norm[l][None],
        "wrg": padl(router_group[l]).astype(BF16), "brg": padl(router_group_bias[l][None]),
        "wre": padl(router_expert[l]).astype(BF16), "bre": padl(router_expert_bias[l][None]),
        "wg": w_gate[l], "wu": w_up[l], "wd": w_down[l],
        "final_norm": final_norm[None],
    }


def _cos_sin(seq, rot):
    half = rot // 2
    inv = ROPE_THETA ** (-jnp.arange(half, dtype=F32) / half)
    ang = jnp.arange(seq, dtype=F32)[:, None] * inv[None, :]
    return jnp.cos(ang), jnp.sin(ang)


N_ROPE_TABLES = 4


def _rope_tables(seq):
    lane = np.arange(LANES)
    first = (lane >= ROT_X1_LANES[0]) & (lane < ROT_X1_LANES[1])
    second = np.roll(first, LANES // 2)

    def table(half, cos, sin, scale=1.0):
        cos_l = jnp.tile(cos, (1, LANES // half))
        sin_l = jnp.tile(sin, (1, LANES // half))
        return [jnp.where(first | second, cos_l, 1.0) * scale,
                (jnp.where(second, sin_l, 0.0) - jnp.where(first, sin_l, 0.0)) * scale]

    c, s = _cos_sin(seq, MLA_ROPE)
    tq = table(MLA_ROPE // 2, c, s, (MLA_NOPE + MLA_ROPE) ** -0.5 * LOG2E)
    tpe = table(MLA_ROPE // 2, c, s)
    c, s = _cos_sin(seq, DIFF_ROT)
    tdq = table(DIFF_ROT // 2, c, s, DIFF_HEAD_DIM ** -0.5 * LOG2E)
    tdk = table(DIFF_ROT // 2, c, s)
    return jnp.concatenate(tq + tpe + tdq + tdk, axis=1)


def _forward(x, packed, tables, lams, subln):
    B, S, D = x.shape
    T = B * S
    x2 = x.reshape(T, D)
    q, k, vt, dq, dk, dvt = _proj_call(x2, packed, tables, S, tm=512)
    r3 = lambda a: a.reshape(B, S, a.shape[-1])
    a = _mla_call(r3(q), r3(k), vt, tq=512)
    b = _diff_call(lams, subln, r3(dq), r3(dk), dvt, tq=512)
    xn, hn, gate, route, count = _mix_call(x2, a.reshape(T, -1), b.reshape(T, -1), packed, tm=512)
    dest, plan = _route_plan(route, count, tile_rows=256)
    xs = _dispatch_call(dest, hn, tm=512)
    ys = _expert_call(plan, xs, packed, tm=256)
    y = _combine_call(dest, ys, xn, gate, packed, tm=512)
    return y.reshape(B, S, D)


def kernel(x_prompt, x_sample, attn_norm, w_in, q_norm, w_uq, kv_norm, w_ukv, diff_lq1, diff_lk1, diff_lq2,
           diff_lk2, diff_subln, w_out, ffn_norm, router_group, router_group_bias, router_expert,
           router_expert_bias, w_gate, w_up, w_down, final_norm):
    packed = _pack_weights(attn_norm, w_in, q_norm, w_uq, kv_norm, w_ukv, w_out, ffn_norm, router_group,
                           router_group_bias, router_expert, router_expert_bias, w_gate, w_up, w_down,
                           final_norm)
    lams = (diff_lq1, diff_lk1, diff_lq2, diff_lk2)
    subln = diff_subln
    tables = _rope_tables(max(x_prompt.shape[1], x_sample.shape[1]))
    return (_forward(x_prompt, packed, tables, lams, subln), _forward(x_sample, packed, tables, lams, subln))
```

```python
import functools
import math

import numpy as np
import jax
import jax.numpy as jnp
from jax import lax
from jax.experimental import pallas as pl
from jax.experimental.pallas import tpu as pltpu

D_MODEL = 1024
MLA_HEADS = 8
MLA_NOPE = 64
MLA_ROPE = 32
MLA_V = 64
Q_RANK = 256
KV_RANK = 128
DIFF_HEADS = 4
DIFF_HEAD_DIM = 64
DIFF_ROT = DIFF_HEAD_DIM // 4
ROPE_THETA = 500000.0
N_GROUPS = 4
EXPERTS_PER_GROUP = 8
N_EXPERTS = N_GROUPS * EXPERTS_PER_GROUP
EXPERT_FF = 256
EPS = 1e-6
LAMBDA_INIT = 0.8 - 0.6 * math.exp(-0.3 * 0)

LANES = 128
D_DIFF = DIFF_HEADS * 2 * DIFF_HEAD_DIM
D_MLA_PAD = MLA_HEADS * LANES
LOG2E = math.log2(math.e)
VMEM_LIMIT = 56 * 1024 * 1024

BF16 = jnp.bfloat16
F32 = jnp.float32

_C_Q = 0
_C_KV = _C_Q + Q_RANK
_C_PE = _C_KV + KV_RANK
_C_DQ = _C_PE + LANES
_C_DK = _C_DQ + D_DIFF
_C_END = _C_DK + D_DIFF


def _rms(x, g):
    return x * lax.rsqrt(jnp.mean(x * x, axis=-1, keepdims=True) + EPS) * g


def _dot(a, b):
    return jnp.dot(a, b, preferred_element_type=F32)


def _dot_nt(a, b):
    return lax.dot_general(a, b, (((1,), (1,)), ((), ())), preferred_element_type=F32)


ROW_TILES = D_MODEL // LANES
assert ROW_TILES == 8


def _token_tile(ref, t):
    return ref.at[pl.ds(pl.multiple_of(t * ROW_TILES, ROW_TILES), ROW_TILES)]


def _store_row_tiles(ref, x):
    for j in range(ROW_TILES):
        ref[pl.ds(j, x.shape[0], stride=ROW_TILES), :] = x[:, j * LANES:(j + 1) * LANES]


def _load_row_tiles(ref):
    rows = ref.shape[0] // ROW_TILES
    return jnp.concatenate([ref[pl.ds(j, rows, stride=ROW_TILES), :] for j in range(ROW_TILES)], axis=1)


ROT_X1_LANES = (0, 16)
DIFF_MAP_A_LANES = ((0, 8), (16, 72))


def _rotary(x, table):
    return x * table[:, 0:LANES] + pltpu.roll(x, LANES // 2, 1) * table[:, LANES:2 * LANES]


def _proj_kernel(x_ref, an_ref, w1_ref, qn_ref, wq_ref, kvn_ref, wk_ref, wvt_ref, wdvt_ref,
                 tq_ref, tpe_ref, tdq_ref, tdk_ref,
                 q_ref, k_ref, vt_ref, dq_ref, dk_ref, dvt_ref):
    hn = _rms(x_ref[...], an_ref[...]).astype(BF16)

    def proj(lo, hi):
        return _dot(hn, w1_ref[:, lo:hi])

    cq_raw = proj(_C_Q, _C_KV)
    kv_pe = proj(_C_KV, _C_DQ)
    dq, dk = proj(_C_DQ, _C_DK), proj(_C_DK, _C_END)
    dvt_ref[...] = _dot_nt(wdvt_ref[...], hn).astype(BF16)

    cq = _rms(cq_raw, qn_ref[...]).astype(BF16)
    tq = tq_ref[...]
    qa = _dot(cq, wq_ref[...])
    for h in range(MLA_HEADS):
        blk = slice(h * LANES, (h + 1) * LANES)
        q_ref[:, blk] = _rotary(qa[:, blk], tq).astype(BF16)

    ckv = _rms(kv_pe[:, 0:KV_RANK], kvn_ref[...]).astype(BF16)
    pe = _rotary(kv_pe[:, KV_RANK:], tpe_ref[...])
    pe_hi = pe.astype(BF16)
    pe_lo = (pe - pe_hi.astype(F32)).astype(BF16)
    k_all = _dot(jnp.concatenate([ckv, pe_hi, pe_lo], axis=1), wk_ref[...])
    k_ref[...] = k_all.astype(BF16)
    vt = _dot_nt(wvt_ref[...], ckv)
    ones_row = (lax.broadcasted_iota(jnp.int32, vt.shape, 0) & (LANES - 1)) == MLA_V
    vt_ref[...] = jnp.where(ones_row, 1.0, vt).astype(BF16)

    tdq, tdk = tdq_ref[...], tdk_ref[...]
    for h in range(DIFF_HEADS):
        blk = slice(h * LANES, (h + 1) * LANES)
        dq_ref[:, blk] = _rotary(dq[:, blk], tdq).astype(BF16)
        dk_ref[:, blk] = _rotary(dk[:, blk], tdk).astype(BF16)


def _proj_call(x2, packed, tables, seq, tm):
    T = x2.shape[0]
    nper = seq // tm
    B = T // seq
    row = lambda i: (i, 0)
    const = lambda i: (0, 0)
    tab = lambda i: (i % nper, 0)
    col = lambda i: (i // nper, 0, i % nper)
    full = lambda a: pl.BlockSpec(a.shape, const)
    tspecs = [pl.BlockSpec((tm, 2 * LANES), lambda i, j=j: (i % nper, j)) for j in range(N_ROPE_TABLES)]
    tok = lambda d: (jax.ShapeDtypeStruct((T, d), BF16), pl.BlockSpec((tm, d), row))
    tr = lambda d: (jax.ShapeDtypeStruct((B, d, seq), BF16), pl.BlockSpec((None, d, tm), col))
    outs, ospecs = zip(tok(D_MLA_PAD), tok(D_MLA_PAD), tr(D_MLA_PAD), tok(D_DIFF), tok(D_DIFF), tr(D_DIFF))
    ws = [packed[k] for k in ("attn_norm", "w1", "q_norm", "wq", "kv_norm", "wk", "wvt", "wdvt")]
    return pl.pallas_call(
        _proj_kernel,
        grid=(T // tm,),
        in_specs=[pl.BlockSpec((tm, D_MODEL), row)] + [full(w) for w in ws] + tspecs,
        out_specs=ospecs,
        out_shape=outs,
        compiler_params=pltpu.CompilerParams(dimension_semantics=("arbitrary",), vmem_limit_bytes=VMEM_LIMIT),
        name="proj",
    )(x2, *ws, *([tables] * N_ROPE_TABLES))


def _block_index_maps(n_heads, n_qblocks, n_blocks):
    per_batch = n_heads * n_qblocks

    def split(blk):
        b = blk // per_batch
        r = blk % per_batch
        return b, r // n_qblocks, r % n_qblocks

    def rows(blk):
        b, h, i = split(blk)
        return b, i, h

    def keys(blk):
        b, h, _ = split(blk)
        return b, 0, h

    def values_t(blk):
        b, h, _ = split(blk)
        return b, h, 0

    cur = lambda t: jnp.minimum(t, n_blocks - 1)
    prev = lambda t: jnp.clip(t - 1, 0, n_blocks - 1)
    out = lambda t: jnp.maximum(t - 2, 0)
    return (lambda t: rows(cur(t)), lambda t: keys(cur(t)), lambda t: rows(out(t)), lambda t: values_t(prev(t)))


KEY_CHUNK = 256


def _key_chunk(c):
    return slice(c * KEY_CHUNK, (c + 1) * KEY_CHUNK)


def _running_max(m, st):
    mc = jnp.max(st, axis=0, keepdims=True)
    return mc if m is None else jnp.maximum(m, mc)


def _accumulate(acc, part):
    return part if acc is None else acc + part


def _pipelined_steps(step, zero_scratch):
    t = pl.program_id(0)

    @pl.when(t == 0)
    def _():
        zero_scratch()

    @pl.when(t % 2 == 0)
    def _():
        step(1, 0)

    @pl.when(t % 2 == 1)
    def _():
        step(0, 1)


def _mla_kernel(q_ref, k_ref, vt_ref, o_ref, s0_ref, s1_ref, m0_ref, m1_ref, o0_ref, o1_ref):
    s_refs, m_refs, o_refs = (s0_ref, s1_ref), (m0_ref, m1_ref), (o0_ref, o1_ref)

    def step(src, dst):
        done = o_refs[src][...]
        o_ref[...] = (done / done[MLA_V:MLA_V + 1, :]).T.astype(BF16)

        q = q_ref[...]
        m_src = m_refs[src][...]
        scores = lambda c: _dot_nt(k_ref[_key_chunk(c), :], q)
        n_chunks = k_ref.shape[0] // KEY_CHUNK
        ahead, ot, m_run = scores(0), None, None
        for c in range(n_chunks):
            keys = _key_chunk(c)
            st, ahead = ahead, (scores(c + 1) if c + 1 < n_chunks else None)
            s_refs[dst][keys, :] = st
            m_run = _running_max(m_run, st)
            pt = jnp.exp2(s_refs[src][keys, :] - m_src).astype(BF16)
            ot = _accumulate(ot, _dot(vt_ref[:, keys], pt))
        m_refs[dst][...] = m_run
        o_refs[dst][...] = ot

    def zero_scratch():
        s1_ref[...] = jnp.zeros_like(s1_ref)
        m1_ref[...] = jnp.zeros_like(m1_ref)
        o1_ref[...] = jnp.ones_like(o1_ref)

    _pipelined_steps(step, zero_scratch)


def _mla_call(q, k, vt, tq):
    B, S, _ = q.shape
    n_blocks = B * MLA_HEADS * (S // tq)
    q_map, k_map, o_map, vt_map = _block_index_maps(MLA_HEADS, S // tq, n_blocks)
    return pl.pallas_call(
        _mla_kernel,
        grid=(n_blocks + 2,),
        in_specs=[pl.BlockSpec((None, tq, LANES), q_map), pl.BlockSpec((None, S, LANES), k_map),
                  pl.BlockSpec((None, LANES, S), vt_map)],
        out_specs=pl.BlockSpec((None, tq, LANES), o_map),
        out_shape=jax.ShapeDtypeStruct((B, S, D_MLA_PAD), BF16),
        scratch_shapes=([pltpu.VMEM((S, tq), F32)] * 2 + [pltpu.VMEM((1, tq), F32)] * 2
                        + [pltpu.VMEM((LANES, tq), F32)] * 2),
        compiler_params=pltpu.CompilerParams(dimension_semantics=("arbitrary",), vmem_limit_bytes=VMEM_LIMIT),
        name="mla_attn",
    )(q, k, vt)


def _diff_kernel(lq1_ref, lk1_ref, lq2_ref, lk2_ref, g_ref, q_ref, k_ref, vt_ref, o_ref,
                 sa0_ref, sa1_ref, sb0_ref, sb1_ref, ma0_ref, ma1_ref, mb0_ref, mb1_ref, o0_ref, o1_ref):
    sa_refs, sb_refs = (sa0_ref, sa1_ref), (sb0_ref, sb1_ref)
    ma_refs, mb_refs = (ma0_ref, ma1_ref), (mb0_ref, mb1_ref)
    o_refs = (o0_ref, o1_ref)

    def step(src, dst):
        o_ref[...] = (_rms(o_refs[src][...].T, g_ref[...]) * (1.0 - LAMBDA_INIT)).astype(BF16)

        lam = (jnp.exp(jnp.sum(lq1_ref[...] * lk1_ref[...], axis=-1, keepdims=True))
               - jnp.exp(jnp.sum(lq2_ref[...] * lk2_ref[...], axis=-1, keepdims=True)) + LAMBDA_INIT)
        q = q_ref[...]
        lane = lax.broadcasted_iota(jnp.int32, q.shape, 1)
        (a0, a1), (a2, a3) = DIFF_MAP_A_LANES
        first = ((lane >= a0) & (lane < a1)) | ((lane >= a2) & (lane < a3))
        zero = jnp.zeros_like(q)
        qs = (jnp.where(first, q, zero), jnp.where(first, zero, q))
        s_maps, m_maps = (sa_refs, sb_refs), (ma_refs, mb_refs)
        m_src = [m_refs[src][...] for m_refs in m_maps]
        scores = lambda c: [_dot_nt(k_ref[_key_chunk(c), :], qc) for qc in qs]
        n_chunks = k_ref.shape[0] // KEY_CHUNK
        ahead = scores(0)
        m_run, tot, ot = [None, None], [None, None], [None, None]
        for c in range(n_chunks):
            keys = _key_chunk(c)
            now, ahead = ahead, (scores(c + 1) if c + 1 < n_chunks else None)
            vt = vt_ref[:, keys]
            for i in range(2):
                s_maps[i][dst][keys, :] = now[i]
                m_run[i] = _running_max(m_run[i], now[i])
                e = jnp.exp2(s_maps[i][src][keys, :] - m_src[i])
                tot[i] = _accumulate(tot[i], jnp.sum(e, axis=0, keepdims=True))
                ot[i] = _accumulate(ot[i], _dot(vt, e.astype(BF16)))
        for i in range(2):
            m_maps[i][dst][...] = m_run[i]
        o_refs[dst][...] = ot[0] * (1.0 / tot[0]) - ot[1] * (lam / tot[1])

    def zero_scratch():
        for ref in (sa1_ref, sb1_ref, ma1_ref, mb1_ref, o1_ref):
            ref[...] = jnp.zeros_like(ref)

    _pipelined_steps(step, zero_scratch)


def _diff_call(lams, subln, dq, dk, dvt, tq):
    B, S, _ = dq.shape
    n_blocks = B * DIFF_HEADS * (S // tq)
    q_map, k_map, o_map, vt_map = _block_index_maps(DIFF_HEADS, S // tq, n_blocks)
    small = lambda a: pl.BlockSpec(a.shape, lambda t: (0, 0))
    return pl.pallas_call(
        _diff_kernel,
        grid=(n_blocks + 2,),
        in_specs=[small(a) for a in lams] + [small(subln), pl.BlockSpec((None, tq, LANES), q_map),
                                             pl.BlockSpec((None, S, LANES), k_map),
                                             pl.BlockSpec((None, LANES, S), vt_map)],
        out_specs=pl.BlockSpec((None, tq, LANES), o_map),
        out_shape=jax.ShapeDtypeStruct((B, S, D_DIFF), BF16),
        scratch_shapes=([pltpu.VMEM((S, tq), F32)] * 4 + [pltpu.VMEM((1, tq), F32)] * 4
                        + [pltpu.VMEM((LANES, tq), F32)] * 2),
        compiler_params=pltpu.CompilerParams(dimension_semantics=("arbitrary",), vmem_limit_bytes=VMEM_LIMIT),
        name="diff_attn",
    )(*lams, subln, dq, dk, dvt)


def _mix_kernel(x_ref, a_ref, b_ref, woa_ref, wob_ref, fn_ref, wrg_ref, brg_ref, wre_ref, bre_ref, earlier_ref,
                xn_ref, hn_ref, gate_ref, route_ref, count_ref, base_ref, lg0_ref, lg1_ref, le0_ref, le1_ref):
    t = pl.program_id(0)
    group_refs, expert_refs = (lg0_ref, lg1_ref), (le0_ref, le1_ref)

    def step(src, dst):
        xn = x_ref[...] + _dot(a_ref[...], woa_ref[...]) + _dot(b_ref[...], wob_ref[...])
        xn_ref[...] = xn
        hn32 = _rms(xn, fn_ref[...])
        _store_row_tiles(hn_ref, hn32)
        hn = hn32.astype(BF16)

        lane = lax.broadcasted_iota(jnp.int32, (xn.shape[0], LANES), 1)
        neg = jnp.float32(-jnp.inf)
        big = jnp.int32(LANES)

        def first_lane_of_max(vals):
            top = jnp.max(vals, axis=-1, keepdims=True)
            return top, jnp.min(jnp.where(vals == top, lane, big), axis=-1, keepdims=True)

        gl = jnp.where(lane < N_GROUPS, group_refs[src][...], neg)
        ge = jnp.exp(gl - jnp.max(gl, axis=-1, keepdims=True))
        gp = ge / jnp.sum(ge, axis=-1, keepdims=True)
        p_g, g_idx = first_lane_of_max(gp)

        in_group = (lane < N_EXPERTS) & ((lane & -EXPERTS_PER_GROUP) == g_idx * EXPERTS_PER_GROUP)
        wl = jnp.where(in_group, expert_refs[src][...], neg)
        we = jnp.exp(wl - jnp.max(wl, axis=-1, keepdims=True))
        wp = jnp.where(in_group, we / jnp.sum(we, axis=-1, keepdims=True), -1.0)
        w_a, i_a = first_lane_of_max(wp)
        w_b, i_b = first_lane_of_max(jnp.where(lane == i_a, -1.0, wp))
        tot = w_a + w_b
        gate_ref[...] = (jnp.where(lane == 0, p_g * (w_a / tot), 0.0)
                         + jnp.where(lane == 1, p_g * (w_b / tot), 0.0))

        picked = jnp.where((lane == i_a) | (lane == i_b), 1.0, 0.0)
        before = base_ref[...] + _dot(earlier_ref[...], picked.astype(BF16))
        rank_a = jnp.sum(jnp.where(lane == i_a, before, 0.0), axis=-1, keepdims=True).astype(jnp.int32)
        rank_b = jnp.sum(jnp.where(lane == i_b, before, 0.0), axis=-1, keepdims=True).astype(jnp.int32)
        zero = jnp.zeros_like(lane)
        route_ref[...] = (jnp.where(lane == 0, i_a, zero) + jnp.where(lane == 1, i_b, zero)
                          + jnp.where(lane == 2, rank_a, zero) + jnp.where(lane == 3, rank_b, zero))
        base_ref[...] += jnp.where(t > 0, jnp.sum(picked, axis=0, keepdims=True), 0.0)
        count_ref[...] = base_ref[...]

        group_refs[dst][...] = _dot(hn, wrg_ref[...]) + brg_ref[...]
        expert_refs[dst][...] = _dot(hn, wre_ref[...]) + bre_ref[...]

    def zero_scratch():
        for ref in (base_ref, lg1_ref, le1_ref):
            ref[...] = jnp.zeros_like(ref)

    _pipelined_steps(step, zero_scratch)


def _mix_call(x2, a2, b2, packed, tm):
    T = x2.shape[0]
    n_tiles = T // tm
    cur = lambda t: (jnp.minimum(t, n_tiles - 1), 0)
    prev = lambda t: (jnp.maximum(t - 1, 0), 0)
    full = lambda a: pl.BlockSpec(a.shape, lambda t: (0, 0))
    ws = [packed[k] for k in ("woa", "wob", "ffn_norm", "wrg", "brg", "wre", "bre")]
    ws.append(jnp.asarray(np.tril(np.ones((tm, tm), np.float32), -1), BF16))
    return pl.pallas_call(
        _mix_kernel,
        grid=(n_tiles + 1,),
        in_specs=[pl.BlockSpec((tm, D_MODEL), cur), pl.BlockSpec((tm, D_MLA_PAD), cur),
                  pl.BlockSpec((tm, D_DIFF), cur)] + [full(w) for w in ws],
        out_specs=[pl.BlockSpec((tm, D_MODEL), cur), pl.BlockSpec((tm * ROW_TILES, LANES), cur),
                   pl.BlockSpec((tm, LANES), prev), pl.BlockSpec((tm, LANES), prev),
                   pl.BlockSpec((1, LANES), lambda t: (0, 0))],
        out_shape=[jax.ShapeDtypeStruct((T, D_MODEL), F32), jax.ShapeDtypeStruct((T * ROW_TILES, LANES), F32),
                   jax.ShapeDtypeStruct((T, LANES), F32), jax.ShapeDtypeStruct((T, LANES), jnp.int32),
                   jax.ShapeDtypeStruct((1, LANES), F32)],
        scratch_shapes=[pltpu.VMEM((1, LANES), F32)] + [pltpu.VMEM((tm, LANES), F32)] * 4,
        compiler_params=pltpu.CompilerParams(dimension_semantics=("arbitrary",), vmem_limit_bytes=VMEM_LIMIT),
        name="mix_router",
    )(x2, a2, b2, *ws)


def _route_plan(route, count, tile_rows):
    T = route.shape[0]
    P = 2 * T
    nt = P // tile_rows
    i32 = jnp.int32
    eid = jnp.arange(N_EXPERTS, dtype=i32)
    counts = count[0, :N_EXPERTS].astype(i32)
    seg_end = jnp.sum(jnp.where(eid[None, :] <= eid[:, None], counts[None, :], 0), axis=1)
    seg_start = seg_end - counts
    picks = route[:, 0:2]
    dest = jnp.sum(jnp.where(picks[:, :, None] == eid, seg_start, 0), axis=-1) + route[:, 2:4]

    tile_start = jnp.arange(nt, dtype=i32) * tile_rows
    slot_of_tile = jnp.arange(nt, dtype=i32) + jnp.sum(seg_start[None, :] < tile_start[:, None], axis=1).astype(i32)
    slot_of_seg = eid + jnp.minimum(seg_start // tile_rows + 1, nt)
    vals = jnp.concatenate([tile_start, seg_start])
    slots = jnp.concatenate([slot_of_tile, slot_of_seg])
    n_items = nt + N_EXPERTS
    bounds = jnp.sum(jnp.where(slots[None, :] == jnp.arange(n_items, dtype=i32)[:, None], vals[None, :], 0), axis=1)
    nxt = jnp.concatenate([bounds[1:], jnp.full((1,), P, i32)])
    tile = jnp.minimum(bounds // tile_rows, nt - 1)
    lo = bounds - tile * tile_rows
    hi = nxt - tile * tile_rows
    expert = jnp.minimum(jnp.sum(seg_end[None, :] <= bounds[:, None], axis=1).astype(i32), N_EXPERTS - 1)
    return dest.astype(i32), (tile, expert, lo, hi)


def _dispatch_kernel(dest_ref, hn_ref, xs_hbm, sem, *, tm):
    def issue(r, carry):
        for pick in range(2):
            pltpu.make_async_copy(_token_tile(hn_ref, r), _token_tile(xs_hbm, dest_ref[0, 2 * r + pick]),
                                  sem).start(priority=pick)
        return carry

    lax.fori_loop(0, tm, issue, 0, unroll=4)
    for _ in range(2):
        pltpu.make_async_copy(hn_ref, xs_hbm.at[pl.ds(0, tm * ROW_TILES)], sem).wait()


def _dispatch_call(dest, hn, tm):
    T = hn.shape[0] // ROW_TILES
    dest3 = dest.reshape(T // tm, 1, 2 * tm)
    return pl.pallas_call(
        functools.partial(_dispatch_kernel, tm=tm),
        grid=(T // tm,),
        in_specs=[pl.BlockSpec((None, 1, 2 * tm), lambda i: (i, 0, 0), memory_space=pltpu.SMEM),
                  pl.BlockSpec((tm * ROW_TILES, LANES), lambda i: (i, 0))],
        out_specs=pl.BlockSpec(memory_space=pl.ANY),
        out_shape=jax.ShapeDtypeStruct((2 * T * ROW_TILES, LANES), F32),
        scratch_shapes=[pltpu.SemaphoreType.DMA],
        compiler_params=pltpu.CompilerParams(dimension_semantics=("arbitrary",)),
        name="moe_dispatch",
    )(dest3, hn)


def _expert_kernel(tile_ref, exp_ref, lo_ref, hi_ref, xs_ref, wg_ref, wu_ref, wd_ref, ys_ref,
                   wg_bf, wu_bf, wd_bf):
    w = pl.program_id(0)
    lo = lo_ref[w]
    hi = hi_ref[w]
    before = jnp.maximum(w - 1, 0)
    first_visit = (w == 0) | (tile_ref[w] != tile_ref[before])

    @pl.when(first_visit)
    def _():
        ys_ref[...] = jnp.zeros_like(ys_ref)

    @pl.when((w == 0) | (exp_ref[w] != exp_ref[before]))
    def _():
        wg_bf[...] = wg_ref[...].astype(BF16)
        wu_bf[...] = wu_ref[...].astype(BF16)
        wd_bf[...] = wd_ref[...].astype(BF16)

    @pl.when(hi > lo)
    def _():
        x = _load_row_tiles(xs_ref).astype(BF16)
        half = x.shape[0] // 2
        gate_up = [(_dot(xh, wg_bf[...]), _dot(xh, wu_bf[...])) for xh in (x[:half], x[half:])]
        y = jnp.concatenate([_dot((g * jax.nn.sigmoid(g) * u).astype(BF16), wd_bf[...]) for g, u in gate_up], axis=0)
        row = lax.broadcasted_iota(jnp.int32, (y.shape[0], 1), 0)
        _store_row_tiles(ys_ref, jnp.where((row >= lo) & (row < hi), y, _load_row_tiles(ys_ref)))


def _expert_call(plan, xs, packed, tm):
    n_items = plan[0].shape[0]
    by_tile = lambda w, tile, exp, lo, hi: (tile[w], 0)
    by_expert = lambda w, tile, exp, lo, hi: (exp[w], 0, 0)
    grid_spec = pltpu.PrefetchScalarGridSpec(
        num_scalar_prefetch=4,
        grid=(n_items,),
        in_specs=[pl.BlockSpec((tm * ROW_TILES, LANES), by_tile),
                  pl.BlockSpec((None, D_MODEL, EXPERT_FF), by_expert),
                  pl.BlockSpec((None, D_MODEL, EXPERT_FF), by_expert),
                  pl.BlockSpec((None, EXPERT_FF, D_MODEL), by_expert)],
        out_specs=pl.BlockSpec((tm * ROW_TILES, LANES), by_tile),
        scratch_shapes=[pltpu.VMEM((D_MODEL, EXPERT_FF), BF16), pltpu.VMEM((D_MODEL, EXPERT_FF), BF16),
                        pltpu.VMEM((EXPERT_FF, D_MODEL), BF16)],
    )
    return pl.pallas_call(
        _expert_kernel,
        grid_spec=grid_spec,
        out_shape=jax.ShapeDtypeStruct(xs.shape, F32),
        compiler_params=pltpu.CompilerParams(dimension_semantics=("arbitrary",), vmem_limit_bytes=VMEM_LIMIT),
        name="moe_experts",
    )(*plan, xs, packed["wg"], packed["wu"], packed["wd"])


def _combine_kernel(dest_ref, dest_next_ref, ys_hbm, xn_ref, gate_ref, fin_ref, y_ref, buf, sems, *, tm):
    i = pl.program_id(0)

    def start_gather(dests, slot):
        def issue(r, carry):
            for pick in range(2):
                pltpu.make_async_copy(_token_tile(ys_hbm, dests[0, 2 * r + pick]),
                                      _token_tile(buf.at[slot, pick], r), sems.at[slot]).start(priority=pick)
            return carry
        lax.fori_loop(0, tm, issue, 0, unroll=4)

    def finish(slot):
        for pick in range(2):
            pltpu.make_async_copy(ys_hbm.at[pl.ds(0, tm * ROW_TILES)], buf.at[slot, pick], sems.at[slot]).wait()
        gate = gate_ref[...]
        y = (xn_ref[...] + gate[:, 0:1] * _load_row_tiles(buf.at[slot, 0])
             + gate[:, 1:2] * _load_row_tiles(buf.at[slot, 1]))
        y_ref[...] = _rms(y, fin_ref[...])

    @pl.when(i == 0)
    def _():
        start_gather(dest_ref, 0)

    for slot in range(2):
        @pl.when(i % 2 == slot)
        def _(slot=slot):
            @pl.when(i + 1 < pl.num_programs(0))
            def _():
                start_gather(dest_next_ref, 1 - slot)
            finish(slot)


def _combine_call(dest, ys, xn, gate, packed, tm):
    T = xn.shape[0]
    n_tiles = T // tm
    dest3 = dest.reshape(n_tiles, 1, 2 * tm)
    row = lambda i: (i, 0)
    dest_spec = lambda index: pl.BlockSpec((None, 1, 2 * tm), index, memory_space=pltpu.SMEM)
    return pl.pallas_call(
        functools.partial(_combine_kernel, tm=tm),
        grid=(n_tiles,),
        in_specs=[dest_spec(lambda i: (i, 0, 0)), dest_spec(lambda i: (jnp.minimum(i + 1, n_tiles - 1), 0, 0)),
                  pl.BlockSpec(memory_space=pl.ANY),
                  pl.BlockSpec((tm, D_MODEL), row), pl.BlockSpec((tm, LANES), row),
                  pl.BlockSpec((1, D_MODEL), lambda i: (0, 0))],
        out_specs=pl.BlockSpec((tm, D_MODEL), row),
        out_shape=jax.ShapeDtypeStruct((T, D_MODEL), F32),
        scratch_shapes=[pltpu.VMEM((2, 2, tm * ROW_TILES, LANES), F32), pltpu.SemaphoreType.DMA((2,))],
        compiler_params=pltpu.CompilerParams(dimension_semantics=("arbitrary",), vmem_limit_bytes=VMEM_LIMIT),
        name="moe_combine",
    )(dest3, dest3, ys, xn, gate, packed["final_norm"])


def _pack_weights(attn_norm, w_in, q_norm, w_uq, kv_norm, w_ukv, w_out, ffn_norm, router_group,
                  router_group_bias, router_expert, router_expert_bias, w_gate, w_up, w_down, final_norm):
    l = 0
    wi = w_in[l]
    o_q, o_kv, o_pe = 0, Q_RANK, Q_RANK + KV_RANK
    o_dq = o_pe + MLA_ROPE
    o_dk = o_dq + D_DIFF
    o_dv = o_dk + D_DIFF
    cat = lambda parts: jnp.concatenate(parts, axis=-1)
    zeros = lambda like, n: jnp.zeros(like.shape[:-1] + (n,), F32)
    h_rope, h_diff = MLA_ROPE // 2, DIFF_ROT // 2
    gap = LANES // 2 - h_rope

    def diff_tiles(w):
        t = w.reshape(w.shape[0], DIFF_HEADS, 2, DIFF_HEAD_DIM)
        a, b = t[:, :, 0], t[:, :, 1]
        x1, x2, rest = slice(0, h_diff), slice(h_diff, DIFF_ROT), slice(DIFF_ROT, DIFF_HEAD_DIM)
        return cat([a[..., x1], b[..., x1], a[..., rest], a[..., x2], b[..., x2], b[..., rest]]).reshape(w.shape)

    w_pe = wi[:, o_pe:o_dq]
    pe_tile = cat([w_pe[:, :h_rope], zeros(w_pe, gap), w_pe[:, h_rope:], zeros(w_pe, gap)])
    w1 = cat([wi[:, o_q:o_pe], pe_tile, diff_tiles(wi[:, o_dq:o_dk]), diff_tiles(wi[:, o_dk:o_dv])]).astype(BF16)
    wdvt = wi[:, o_dv:].T.astype(BF16)

    def mla_tiles(nope, rope):
        r1 = zeros(nope, h_rope) if rope is None else rope[..., :h_rope]
        r2 = zeros(nope, h_rope) if rope is None else rope[..., h_rope:]
        return cat([r1, nope[..., :gap], r2, nope[..., gap:], zeros(nope, LANES - MLA_NOPE - MLA_ROPE)])

    wq3 = w_uq[l].reshape(Q_RANK, MLA_HEADS, MLA_NOPE + MLA_ROPE)
    wq = mla_tiles(wq3[..., :MLA_NOPE], wq3[..., MLA_NOPE:]).reshape(Q_RANK, D_MLA_PAD).astype(BF16)

    wkv3 = w_ukv[l].reshape(KV_RANK, MLA_HEADS, MLA_NOPE + MLA_V)
    padk = lambda a: jnp.pad(a, ((0, 0), (0, 0), (0, LANES - a.shape[-1]))).reshape(KV_RANK, D_MLA_PAD)
    wvt = padk(wkv3[:, :, MLA_NOPE:]).T.astype(BF16)
    place = np.zeros((LANES, MLA_HEADS, LANES), np.float32)
    for r in range(h_rope):
        for lane_ in (ROT_X1_LANES[0] + r, ROT_X1_LANES[0] + LANES // 2 + r):
            place[lane_, :, lane_] = 1.0
    place = jnp.asarray(place.reshape(LANES, D_MLA_PAD))
    wk_nope = mla_tiles(wkv3[:, :, :MLA_NOPE], None).reshape(KV_RANK, D_MLA_PAD)
    wk = jnp.concatenate([wk_nope, place, place], axis=0).astype(BF16)

    wo = w_out[l]
    n_mla = MLA_HEADS * MLA_V
    woa = jnp.pad(wo[:n_mla].reshape(MLA_HEADS, MLA_V, D_MODEL),
                  ((0, 0), (0, LANES - MLA_V), (0, 0))).reshape(D_MLA_PAD, D_MODEL).astype(BF16)
    padl = lambda a: jnp.pad(a, ((0, 0), (0, LANES - a.shape[-1])))
    return {
        "attn_norm": attn_norm[l][None], "w1": w1, "q_norm": q_norm[l][None], "wq": wq,
        "kv_norm": kv_norm[l][None], "wk": wk, "wvt": wvt, "wdvt": wdvt,
        "woa": woa, "wob": wo[n_mla:].astype(BF16), "ffn_norm": ffn_norm[l][None],
        "wrg": padl(router_group[l]).astype(BF16), "brg": padl(router_group_bias[l][None]),
        "wre": padl(router_expert[l]).astype(BF16), "bre": padl(router_expert_bias[l][None]),
        "wg": w_gate[l], "wu": w_up[l], "wd": w_down[l],
        "final_norm": final_norm[None],
    }


def _cos_sin(seq, rot):
    half = rot // 2
    inv = ROPE_THETA ** (-jnp.arange(half, dtype=F32) / half)
    ang = jnp.arange(seq, dtype=F32)[:, None] * inv[None, :]
    return jnp.cos(ang), jnp.sin(ang)


N_ROPE_TABLES = 4


def _rope_tables(seq):
    lane = np.arange(LANES)
    first = (lane >= ROT_X1_LANES[0]) & (lane < ROT_X1_LANES[1])
    second = np.roll(first, LANES // 2)

    def table(half, cos, sin, scale=1.0):
        cos_l = jnp.tile(cos, (1, LANES // half))
        sin_l = jnp.tile(sin, (1, LANES // half))
        return [jnp.where(first | second, cos_l, 1.0) * scale,
                (jnp.where(second, sin_l, 0.0) - jnp.where(first, sin_l, 0.0)) * scale]

    c, s = _cos_sin(seq, MLA_ROPE)
    tq = table(MLA_ROPE // 2, c, s, (MLA_NOPE + MLA_ROPE) ** -0.5 * LOG2E)
    tpe = table(MLA_ROPE // 2, c, s)
    c, s = _cos_sin(seq, DIFF_ROT)
    tdq = table(DIFF_ROT // 2, c, s, DIFF_HEAD_DIM ** -0.5 * LOG2E)
    tdk = table(DIFF_ROT // 2, c, s)
    return jnp.concatenate(tq + tpe + tdq + tdk, axis=1)


def _forward(x, packed, tables, lams, subln):
    B, S, D = x.shape
    T = B * S
    x2 = x.reshape(T, D)
    q, k, vt, dq, dk, dvt = _proj_call(x2, packed, tables, S, tm=512)
    r3 = lambda a: a.reshape(B, S, a.shape[-1])
    a = _mla_call(r3(q), r3(k), vt, tq=512)
    b = _diff_call(lams, subln, r3(dq), r3(dk), dvt, tq=512)
    xn, hn, gate, route, count = _mix_call(x2, a.reshape(T, -1), b.reshape(T, -1), packed, tm=512)
    dest, plan = _route_plan(route, count, tile_rows=256)
    xs = _dispatch_call(dest, hn, tm=512)
    ys = _expert_call(plan, xs, packed, tm=256)
    y = _combine_call(dest, ys, xn, gate, packed, tm=512)
    return y.reshape(B, S, D)


def kernel(x_prompt, x_sample, attn_norm, w_in, q_norm, w_uq, kv_norm, w_ukv, diff_lq1, diff_lk1, diff_lq2,
           diff_lk2, diff_subln, w_out, ffn_norm, router_group, router_group_bias, router_expert,
           router_expert_bias, w_gate, w_up, w_down, final_norm):
    packed = _pack_weights(attn_norm, w_in, q_norm, w_uq, kv_norm, w_ukv, w_out, ffn_norm, router_group,
                           router_group_bias, router_expert, router_expert_bias, w_gate, w_up, w_down,
                           final_norm)
    lams = (diff_lq1, diff_lk1, diff_lq2, diff_lk2)
    subln = diff_subln
    tables = _rope_tables(max(x_prompt.shape[1], x_sample.shape[1]))
    return (_forward(x_prompt, packed, tables, lams, subln), _forward(x_sample, packed, tables, lams, subln))
```

```python
import functools
import math

import numpy as np
import jax
import jax.numpy as jnp
from jax import lax
from jax.experimental import pallas as pl
from jax.experimental.pallas import tpu as pltpu

D_MODEL = 1024
MLA_HEADS = 8
MLA_NOPE = 64
MLA_ROPE = 32
MLA_V = 64
Q_RANK = 256
KV_RANK = 128
DIFF_HEADS = 4
DIFF_HEAD_DIM = 64
DIFF_ROT = DIFF_HEAD_DIM // 4
ROPE_THETA = 500000.0
N_GROUPS = 4
EXPERTS_PER_GROUP = 8
N_EXPERTS = N_GROUPS * EXPERTS_PER_GROUP
EXPERT_FF = 256
EPS = 1e-6
LAMBDA_INIT = 0.8 - 0.6 * math.exp(-0.3 * 0)

LANES = 128
D_DIFF = DIFF_HEADS * 2 * DIFF_HEAD_DIM
D_MLA_PAD = MLA_HEADS * LANES
LOG2E = math.log2(math.e)
VMEM_LIMIT = 56 * 1024 * 1024

BF16 = jnp.bfloat16
F32 = jnp.float32

_C_Q = 0
_C_KV = _C_Q + Q_RANK
_C_PE = _C_KV + KV_RANK
_C_DQ = _C_PE + LANES
_C_DK = _C_DQ + D_DIFF
_C_END = _C_DK + D_DIFF


def _rms(x, g):
    return x * lax.rsqrt(jnp.mean(x * x, axis=-1, keepdims=True) + EPS) * g


def _dot(a, b):
    return jnp.dot(a, b, preferred_element_type=F32)


def _dot_nt(a, b):
    return lax.dot_general(a, b, (((1,), (1,)), ((), ())), preferred_element_type=F32)


ROW_TILES = D_MODEL // LANES
assert ROW_TILES == 8


def _token_tile(ref, t):
    return ref.at[pl.ds(pl.multiple_of(t * ROW_TILES, ROW_TILES), ROW_TILES)]


def _store_row_tiles(ref, x):
    for j in range(ROW_TILES):
        ref[pl.ds(j, x.shape[0], stride=ROW_TILES), :] = x[:, j * LANES:(j + 1) * LANES]


def _load_row_tiles(ref):
    rows = ref.shape[0] // ROW_TILES
    return jnp.concatenate([ref[pl.ds(j, rows, stride=ROW_TILES), :] for j in range(ROW_TILES)], axis=1)


ROT_X1_LANES = (0, 16)
DIFF_MAP_A_LANES = ((0, 8), (16, 72))


def _rotary(x, table):
    return x * table[:, 0:LANES] + pltpu.roll(x, LANES // 2, 1) * table[:, LANES:2 * LANES]


def _proj_kernel(x_ref, an_ref, w1_ref, qn_ref, wq_ref, kvn_ref, wk_ref, wvt_ref, wdvt_ref,
                 tq_ref, tpe_ref, tdq_ref, tdk_ref,
                 q_ref, k_ref, vt_ref, dq_ref, dk_ref, dvt_ref):
    hn = _rms(x_ref[...], an_ref[...]).astype(BF16)

    def proj(lo, hi):
        return _dot(hn, w1_ref[:, lo:hi])

    cq_raw = proj(_C_Q, _C_KV)
    kv_pe = proj(_C_KV, _C_DQ)
    dq, dk = proj(_C_DQ, _C_DK), proj(_C_DK, _C_END)
    dvt_ref[...] = _dot_nt(wdvt_ref[...], hn).astype(BF16)

    cq = _rms(cq_raw, qn_ref[...]).astype(BF16)
    tq = tq_ref[...]
    qa = _dot(cq, wq_ref[...])
    for h in range(MLA_HEADS):
        blk = slice(h * LANES, (h + 1) * LANES)
        q_ref[:, blk] = _rotary(qa[:, blk], tq).astype(BF16)

    ckv = _rms(kv_pe[:, 0:KV_RANK], kvn_ref[...]).astype(BF16)
    pe = _rotary(kv_pe[:, KV_RANK:], tpe_ref[...])
    pe_hi = pe.astype(BF16)
    pe_lo = (pe - pe_hi.astype(F32)).astype(BF16)
    k_all = _dot(jnp.concatenate([ckv, pe_hi, pe_lo], axis=1), wk_ref[...])
    k_ref[...] = k_all.astype(BF16)
    vt = _dot_nt(wvt_ref[...], ckv)
    ones_row = (lax.broadcasted_iota(jnp.int32, vt.shape, 0) & (LANES - 1)) == MLA_V
    vt_ref[...] = jnp.where(ones_row, 1.0, vt).astype(BF16)

    tdq, tdk = tdq_ref[...], tdk_ref[...]
    for h in range(DIFF_HEADS):
        blk = slice(h * LANES, (h + 1) * LANES)
        dq_ref[:, blk] = _rotary(dq[:, blk], tdq).astype(BF16)
        dk_ref[:, blk] = _rotary(dk[:, blk], tdk).astype(BF16)


def _proj_call(x2, packed, tables, seq, tm):
    T = x2.shape[0]
    nper = seq // tm
    B = T // seq
    row = lambda i: (i, 0)
    const = lambda i: (0, 0)
    tab = lambda i: (i % nper, 0)
    col = lambda i: (i // nper, 0, i % nper)
    full = lambda a: pl.BlockSpec(a.shape, const)
    tspecs = [pl.BlockSpec((tm, 2 * LANES), lambda i, j=j: (i % nper, j)) for j in range(N_ROPE_TABLES)]
    tok = lambda d: (jax.ShapeDtypeStruct((T, d), BF16), pl.BlockSpec((tm, d), row))
    tr = lambda d: (jax.ShapeDtypeStruct((B, d, seq), BF16), pl.BlockSpec((None, d, tm), col))
    outs, ospecs = zip(tok(D_MLA_PAD), tok(D_MLA_PAD), tr(D_MLA_PAD), tok(D_DIFF), tok(D_DIFF), tr(D_DIFF))
    ws = [packed[k] for k in ("attn_norm", "w1", "q_norm", "wq", "kv_norm", "wk", "wvt", "wdvt")]
    return pl.pallas_call(
        _proj_kernel,
        grid=(T // tm,),
        in_specs=[pl.BlockSpec((tm, D_MODEL), row)] + [full(w) for w in ws] + tspecs,
        out_specs=ospecs,
        out_shape=outs,
        compiler_params=pltpu.CompilerParams(dimension_semantics=("arbitrary",), vmem_limit_bytes=VMEM_LIMIT),
        name="proj",
    )(x2, *ws, *([tables] * N_ROPE_TABLES))


def _block_index_maps(n_heads, n_qblocks, n_blocks):
    per_batch = n_heads * n_qblocks

    def split(blk):
        b = blk // per_batch
        r = blk % per_batch
        return b, r // n_qblocks, r % n_qblocks

    def rows(blk):
        b, h, i = split(blk)
        return b, i, h

    def keys(blk):
        b, h, _ = split(blk)
        return b, 0, h

    def values_t(blk):
        b, h, _ = split(blk)
        return b, h, 0

    cur = lambda t: jnp.minimum(t, n_blocks - 1)
    prev = lambda t: jnp.clip(t - 1, 0, n_blocks - 1)
    out = lambda t: jnp.maximum(t - 2, 0)
    return (lambda t: rows(cur(t)), lambda t: keys(cur(t)), lambda t: rows(out(t)), lambda t: values_t(prev(t)))


KEY_CHUNK = 256


def _key_chunk(c):
    return slice(c * KEY_CHUNK, (c + 1) * KEY_CHUNK)


def _running_max(m, st):
    mc = jnp.max(st, axis=0, keepdims=True)
    return mc if m is None else jnp.maximum(m, mc)


def _accumulate(acc, part):
    return part if acc is None else acc + part


def _pipelined_steps(step, zero_scratch):
    t = pl.program_id(0)

    @pl.when(t == 0)
    def _():
        zero_scratch()

    @pl.when(t % 2 == 0)
    def _():
        step(1, 0)

    @pl.when(t % 2 == 1)
    def _():
        step(0, 1)


def _mla_kernel(q_ref, k_ref, vt_ref, o_ref, s0_ref, s1_ref, m0_ref, m1_ref, o0_ref, o1_ref):
    s_refs, m_refs, o_refs = (s0_ref, s1_ref), (m0_ref, m1_ref), (o0_ref, o1_ref)

    def step(src, dst):
        done = o_refs[src][...]
        o_ref[...] = (done / done[MLA_V:MLA_V + 1, :]).T.astype(BF16)

        q = q_ref[...]
        m_src = m_refs[src][...]
        scores = lambda c: _dot_nt(k_ref[_key_chunk(c), :], q)
        n_chunks = k_ref.shape[0] // KEY_CHUNK
        ahead, ot, m_run = scores(0), None, None
        for c in range(n_chunks):
            keys = _key_chunk(c)
            st, ahead = ahead, (scores(c + 1) if c + 1 < n_chunks else None)
            s_refs[dst][keys, :] = st
            m_run = _running_max(m_run, st)
            pt = jnp.exp2(s_refs[src][keys, :] - m_src).astype(BF16)
            ot = _accumulate(ot, _dot(vt_ref[:, keys], pt))
        m_refs[dst][...] = m_run
        o_refs[dst][...] = ot

    def zero_scratch():
        s1_ref[...] = jnp.zeros_like(s1_ref)
        m1_ref[...] = jnp.zeros_like(m1_ref)
        o1_ref[...] = jnp.ones_like(o1_ref)

    _pipelined_steps(step, zero_scratch)


def _mla_call(q, k, vt, tq):
    B, S, _ = q.shape
    n_blocks = B * MLA_HEADS * (S // tq)
    q_map, k_map, o_map, vt_map = _block_index_maps(MLA_HEADS, S // tq, n_blocks)
    return pl.pallas_call(
        _mla_kernel,
        grid=(n_blocks + 2,),
        in_specs=[pl.BlockSpec((None, tq, LANES), q_map), pl.BlockSpec((None, S, LANES), k_map),
                  pl.BlockSpec((None, LANES, S), vt_map)],
        out_specs=pl.BlockSpec((None, tq, LANES), o_map),
        out_shape=jax.ShapeDtypeStruct((B, S, D_MLA_PAD), BF16),
        scratch_shapes=([pltpu.VMEM((S, tq), F32)] * 2 + [pltpu.VMEM((1, tq), F32)] * 2
                        + [pltpu.VMEM((LANES, tq), F32)] * 2),
        compiler_params=pltpu.CompilerParams(dimension_semantics=("arbitrary",), vmem_limit_bytes=VMEM_LIMIT),
        name="mla_attn",
    )(q, k, vt)


def _diff_kernel(lq1_ref, lk1_ref, lq2_ref, lk2_ref, g_ref, q_ref, k_ref, vt_ref, o_ref,
                 sa0_ref, sa1_ref, sb0_ref, sb1_ref, ma0_ref, ma1_ref, mb0_ref, mb1_ref, o0_ref, o1_ref):
    sa_refs, sb_refs = (sa0_ref, sa1_ref), (sb0_ref, sb1_ref)
    ma_refs, mb_refs = (ma0_ref, ma1_ref), (mb0_ref, mb1_ref)
    o_refs = (o0_ref, o1_ref)

    def step(src, dst):
        o_ref[...] = (_rms(o_refs[src][...].T, g_ref[...]) * (1.0 - LAMBDA_INIT)).astype(BF16)

        lam = (jnp.exp(jnp.sum(lq1_ref[...] * lk1_ref[...], axis=-1, keepdims=True))
               - jnp.exp(jnp.sum(lq2_ref[...] * lk2_ref[...], axis=-1, keepdims=True)) + LAMBDA_INIT)
        q = q_ref[...]
        lane = lax.broadcasted_iota(jnp.int32, q.shape, 1)
        (a0, a1), (a2, a3) = DIFF_MAP_A_LANES
        first = ((lane >= a0) & (lane < a1)) | ((lane >= a2) & (lane < a3))
        zero = jnp.zeros_like(q)
        qs = (jnp.where(first, q, zero), jnp.where(first, zero, q))
        s_maps, m_maps = (sa_refs, sb_refs), (ma_refs, mb_refs)
        m_src = [m_refs[src][...] for m_refs in m_maps]
        scores = lambda c: [_dot_nt(k_ref[_key_chunk(c), :], qc) for qc in qs]
        n_chunks = k_ref.shape[0] // KEY_CHUNK
        ahead = scores(0)
        m_run, tot, ot = [None, None], [None, None], [None, None]
        for c in range(n_chunks):
            keys = _key_chunk(c)
            now, ahead = ahead, (scores(c + 1) if c + 1 < n_chunks else None)
            vt = vt_ref[:, keys]
            for i in range(2):
                s_maps[i][dst][keys, :] = now[i]
                m_run[i] = _running_max(m_run[i], now[i])
                e = jnp.exp2(s_maps[i][src][keys, :] - m_src[i])
                tot[i] = _accumulate(tot[i], jnp.sum(e, axis=0, keepdims=True))
                ot[i] = _accumulate(ot[i], _dot(vt, e.astype(BF16)))
        for i in range(2):
            m_maps[i][dst][...] = m_run[i]
        o_refs[dst][...] = ot[0] * (1.0 / tot[0]) - ot[1] * (lam / tot[1])

    def zero_scratch():
        for ref in (sa1_ref, sb1_ref, ma1_ref, mb1_ref, o1_ref):
            ref[...] = jnp.zeros_like(ref)

    _pipelined_steps(step, zero_scratch)


def _diff_call(lams, subln, dq, dk, dvt, tq):
    B, S, _ = dq.shape
    n_blocks = B * DIFF_HEADS * (S // tq)
    q_map, k_map, o_map, vt_map = _block_index_maps(DIFF_HEADS, S // tq, n_blocks)
    small = lambda a: pl.BlockSpec(a.shape, lambda t: (0, 0))
    return pl.pallas_call(
        _diff_kernel,
        grid=(n_blocks + 2,),
        in_specs=[small(a) for a in lams] + [small(subln), pl.BlockSpec((None, tq, LANES), q_map),
                                             pl.BlockSpec((None, S, LANES), k_map),
                                             pl.BlockSpec((None, LANES, S), vt_map)],
        out_specs=pl.BlockSpec((None, tq, LANES), o_map),
        out_shape=jax.ShapeDtypeStruct((B, S, D_DIFF), BF16),
        scratch_shapes=([pltpu.VMEM((S, tq), F32)] * 4 + [pltpu.VMEM((1, tq), F32)] * 4
                        + [pltpu.VMEM((LANES, tq), F32)] * 2),
        compiler_params=pltpu.CompilerParams(dimension_semantics=("arbitrary",), vmem_limit_bytes=VMEM_LIMIT),
        name="diff_attn",
    )(*lams, subln, dq, dk, dvt)


def _mix_kernel(x_ref, a_ref, b_ref, woa_ref, wob_ref, fn_ref, wrg_ref, brg_ref, wre_ref, bre_ref, earlier_ref,
                xn_ref, hn_ref, gate_ref, route_ref, count_ref, base_ref, lg0_ref, lg1_ref, le0_ref, le1_ref):
    t = pl.program_id(0)
    group_refs, expert_refs = (lg0_ref, lg1_ref), (le0_ref, le1_ref)

    def step(src, dst):
        xn = x_ref[...] + _dot(a_ref[...], woa_ref[...]) + _dot(b_ref[...], wob_ref[...])
        xn_ref[...] = xn
        hn32 = _rms(xn, fn_ref[...])
        _store_row_tiles(hn_ref, hn32)
        hn = hn32.astype(BF16)

        lane = lax.broadcasted_iota(jnp.int32, (xn.shape[0], LANES), 1)
        neg = jnp.float32(-jnp.inf)
        big = jnp.int32(LANES)

        def first_lane_of_max(vals):
            top = jnp.max(vals, axis=-1, keepdims=True)
            return top, jnp.min(jnp.where(vals == top, lane, big), axis=-1, keepdims=True)

        gl = jnp.where(lane < N_GROUPS, group_refs[src][...], neg)
        ge = jnp.exp(gl - jnp.max(gl, axis=-1, keepdims=True))
        gp = ge / jnp.sum(ge, axis=-1, keepdims=True)
        p_g, g_idx = first_lane_of_max(gp)

        in_group = (lane < N_EXPERTS) & ((lane & -EXPERTS_PER_GROUP) == g_idx * EXPERTS_PER_GROUP)
        wl = jnp.where(in_group, expert_refs[src][...], neg)
        we = jnp.exp(wl - jnp.max(wl, axis=-1, keepdims=True))
        wp = jnp.where(in_group, we / jnp.sum(we, axis=-1, keepdims=True), -1.0)
        w_a, i_a = first_lane_of_max(wp)
        w_b, i_b = first_lane_of_max(jnp.where(lane == i_a, -1.0, wp))
        tot = w_a + w_b
        gate_ref[...] = (jnp.where(lane == 0, p_g * (w_a / tot), 0.0)
                         + jnp.where(lane == 1, p_g * (w_b / tot), 0.0))

        picked = jnp.where((lane == i_a) | (lane == i_b), 1.0, 0.0)
        before = base_ref[...] + _dot(earlier_ref[...], picked.astype(BF16))
        rank_a = jnp.sum(jnp.where(lane == i_a, before, 0.0), axis=-1, keepdims=True).astype(jnp.int32)
        rank_b = jnp.sum(jnp.where(lane == i_b, before, 0.0), axis=-1, keepdims=True).astype(jnp.int32)
        zero = jnp.zeros_like(lane)
        route_ref[...] = (jnp.where(lane == 0, i_a, zero) + jnp.where(lane == 1, i_b, zero)
                          + jnp.where(lane == 2, rank_a, zero) + jnp.where(lane == 3, rank_b, zero))
        base_ref[...] += jnp.where(t > 0, jnp.sum(picked, axis=0, keepdims=True), 0.0)
        count_ref[...] = base_ref[...]

        group_refs[dst][...] = _dot(hn, wrg_ref[...]) + brg_ref[...]
        expert_refs[dst][...] = _dot(hn, wre_ref[...]) + bre_ref[...]

    def zero_scratch():
        for ref in (base_ref, lg1_ref, le1_ref):
            ref[...] = jnp.zeros_like(ref)

    _pipelined_steps(step, zero_scratch)


def _mix_call(x2, a2, b2, packed, tm):
    T = x2.shape[0]
    n_tiles = T // tm
    cur = lambda t: (jnp.minimum(t, n_tiles - 1), 0)
    prev = lambda t: (jnp.maximum(t - 1, 0), 0)
    full = lambda a: pl.BlockSpec(a.shape, lambda t: (0, 0))
    ws = [packed[k] for k in ("woa", "wob", "ffn_norm", "wrg", "brg", "wre", "bre")]
    ws.append(jnp.asarray(np.tril(np.ones((tm, tm), np.float32), -1), BF16))
    return pl.pallas_call(
        _mix_kernel,
        grid=(n_tiles + 1,),
        in_specs=[pl.BlockSpec((tm, D_MODEL), cur), pl.BlockSpec((tm, D_MLA_PAD), cur),
                  pl.BlockSpec((tm, D_DIFF), cur)] + [full(w) for w in ws],
        out_specs=[pl.BlockSpec((tm, D_MODEL), cur), pl.BlockSpec((tm * ROW_TILES, LANES), cur),
                   pl.BlockSpec((tm, LANES), prev), pl.BlockSpec((tm, LANES), prev),
                   pl.BlockSpec((1, LANES), lambda t: (0, 0))],
        out_shape=[jax.ShapeDtypeStruct((T, D_MODEL), F32), jax.ShapeDtypeStruct((T * ROW_TILES, LANES), F32),
                   jax.ShapeDtypeStruct((T, LANES), F32), jax.ShapeDtypeStruct((T, LANES), jnp.int32),
                   jax.ShapeDtypeStruct((1, LANES), F32)],
        scratch_shapes=[pltpu.VMEM((1, LANES), F32)] + [pltpu.VMEM((tm, LANES), F32)] * 4,
        compiler_params=pltpu.CompilerParams(dimension_semantics=("arbitrary",), vmem_limit_bytes=VMEM_LIMIT),
        name="mix_router",
    )(x2, a2, b2, *ws)


def _route_plan(route, count, tile_rows):
    T = route.shape[0]
    P = 2 * T
    nt = P // tile_rows
    i32 = jnp.int32
    eid = jnp.arange(N_EXPERTS, dtype=i32)
    counts = count[0, :N_EXPERTS].astype(i32)
    seg_end = jnp.sum(jnp.where(eid[None, :] <= eid[:, None], counts[None, :], 0), axis=1)
    seg_start = seg_end - counts
    dest = tuple((jnp.sum(jnp.where(route[:, pick, None] == eid, seg_start, 0), axis=-1)
                  + route[:, 2 + pick]).astype(i32) for pick in range(2))

    tile_start = jnp.arange(nt, dtype=i32) * tile_rows
    slot_of_tile = jnp.arange(nt, dtype=i32) + jnp.sum(seg_start[None, :] < tile_start[:, None], axis=1).astype(i32)
    slot_of_seg = eid + jnp.minimum(seg_start // tile_rows + 1, nt)
    vals = jnp.concatenate([tile_start, seg_start])
    slots = jnp.concatenate([slot_of_tile, slot_of_seg])
    n_items = nt + N_EXPERTS
    bounds = jnp.sum(jnp.where(slots[None, :] == jnp.arange(n_items, dtype=i32)[:, None], vals[None, :], 0), axis=1)
    nxt = jnp.concatenate([bounds[1:], jnp.full((1,), P, i32)])
    tile = jnp.minimum(bounds // tile_rows, nt - 1)
    lo = bounds - tile * tile_rows
    hi = nxt - tile * tile_rows
    expert = jnp.minimum(jnp.sum(seg_end[None, :] <= bounds[:, None], axis=1).astype(i32), N_EXPERTS - 1)
    return dest, (tile, expert, lo, hi)


def _dispatch_kernel(dest_a_ref, dest_b_ref, hn_ref, xs_hbm, sem, *, tm):
    dests = (dest_a_ref, dest_b_ref)

    def issue(r, carry):
        for pick in range(2):
            pltpu.make_async_copy(_token_tile(hn_ref, r), _token_tile(xs_hbm, dests[pick][0, r]),
                                  sem).start(priority=pick)
        return carry

    lax.fori_loop(0, tm, issue, 0, unroll=4)
    for _ in range(2):
        pltpu.make_async_copy(hn_ref, xs_hbm.at[pl.ds(0, tm * ROW_TILES)], sem).wait()


def _dispatch_call(dest, hn, tm):
    T = hn.shape[0] // ROW_TILES
    dest3 = [d.reshape(T // tm, 1, tm) for d in dest]
    dest_spec = pl.BlockSpec((None, 1, tm), lambda i: (i, 0, 0), memory_space=pltpu.SMEM)
    return pl.pallas_call(
        functools.partial(_dispatch_kernel, tm=tm),
        grid=(T // tm,),
        in_specs=[dest_spec, dest_spec, pl.BlockSpec((tm * ROW_TILES, LANES), lambda i: (i, 0))],
        out_specs=pl.BlockSpec(memory_space=pl.ANY),
        out_shape=jax.ShapeDtypeStruct((2 * T * ROW_TILES, LANES), F32),
        scratch_shapes=[pltpu.SemaphoreType.DMA],
        compiler_params=pltpu.CompilerParams(dimension_semantics=("arbitrary",)),
        name="moe_dispatch",
    )(*dest3, hn)


def _expert_kernel(tile_ref, exp_ref, lo_ref, hi_ref, xs_ref, wg_ref, wu_ref, wd_ref, ys_ref,
                   wg_bf, wu_bf, wd_bf):
    w = pl.program_id(0)
    lo = lo_ref[w]
    hi = hi_ref[w]
    before = jnp.maximum(w - 1, 0)
    first_visit = (w == 0) | (tile_ref[w] != tile_ref[before])

    @pl.when(first_visit)
    def _():
        ys_ref[...] = jnp.zeros_like(ys_ref)

    @pl.when((w == 0) | (exp_ref[w] != exp_ref[before]))
    def _():
        wg_bf[...] = wg_ref[...].astype(BF16)
        wu_bf[...] = wu_ref[...].astype(BF16)
        wd_bf[...] = wd_ref[...].astype(BF16)

    @pl.when(hi > lo)
    def _():
        x = _load_row_tiles(xs_ref).astype(BF16)
        half = x.shape[0] // 2
        gate_up = [(_dot(xh, wg_bf[...]), _dot(xh, wu_bf[...])) for xh in (x[:half], x[half:])]
        y = jnp.concatenate([_dot((g * jax.nn.sigmoid(g) * u).astype(BF16), wd_bf[...]) for g, u in gate_up], axis=0)
        row = lax.broadcasted_iota(jnp.int32, (y.shape[0], 1), 0)
        _store_row_tiles(ys_ref, jnp.where((row >= lo) & (row < hi), y, _load_row_tiles(ys_ref)))


def _expert_call(plan, xs, packed, tm):
    n_items = plan[0].shape[0]
    by_tile = lambda w, tile, exp, lo, hi: (tile[w], 0)
    by_expert = lambda w, tile, exp, lo, hi: (exp[w], 0, 0)
    grid_spec = pltpu.PrefetchScalarGridSpec(
        num_scalar_prefetch=4,
        grid=(n_items,),
        in_specs=[pl.BlockSpec((tm * ROW_TILES, LANES), by_tile),
                  pl.BlockSpec((None, D_MODEL, EXPERT_FF), by_expert),
                  pl.BlockSpec((None, D_MODEL, EXPERT_FF), by_expert),
                  pl.BlockSpec((None, EXPERT_FF, D_MODEL), by_expert)],
        out_specs=pl.BlockSpec((tm * ROW_TILES, LANES), by_tile),
        scratch_shapes=[pltpu.VMEM((D_MODEL, EXPERT_FF), BF16), pltpu.VMEM((D_MODEL, EXPERT_FF), BF16),
                        pltpu.VMEM((EXPERT_FF, D_MODEL), BF16)],
    )
    return pl.pallas_call(
        _expert_kernel,
        grid_spec=grid_spec,
        out_shape=jax.ShapeDtypeStruct(xs.shape, F32),
        compiler_params=pltpu.CompilerParams(dimension_semantics=("arbitrary",), vmem_limit_bytes=VMEM_LIMIT),
        name="moe_experts",
    )(*plan, xs, packed["wg"], packed["wu"], packed["wd"])


def _combine_kernel(dest_a_ref, dest_b_ref, next_a_ref, next_b_ref, ys_hbm, xn_ref, gate_ref, fin_ref, y_ref,
                    buf, sems, *, tm):
    i = pl.program_id(0)

    def start_gather(dests, slot):
        def issue(r, carry):
            for pick in range(2):
                pltpu.make_async_copy(_token_tile(ys_hbm, dests[pick][0, r]),
                                      _token_tile(buf.at[slot, pick], r), sems.at[slot]).start(priority=pick)
            return carry
        lax.fori_loop(0, tm, issue, 0, unroll=4)

    def finish(slot):
        for pick in range(2):
            pltpu.make_async_copy(ys_hbm.at[pl.ds(0, tm * ROW_TILES)], buf.at[slot, pick], sems.at[slot]).wait()
        gate = gate_ref[...]
        y = (xn_ref[...] + gate[:, 0:1] * _load_row_tiles(buf.at[slot, 0])
             + gate[:, 1:2] * _load_row_tiles(buf.at[slot, 1]))
        y_ref[...] = _rms(y, fin_ref[...])

    @pl.when(i == 0)
    def _():
        start_gather((dest_a_ref, dest_b_ref), 0)

    for slot in range(2):
        @pl.when(i % 2 == slot)
        def _(slot=slot):
            @pl.when(i + 1 < pl.num_programs(0))
            def _():
                start_gather((next_a_ref, next_b_ref), 1 - slot)
            finish(slot)


def _combine_call(dest, ys, xn, gate, packed, tm):
    T = xn.shape[0]
    n_tiles = T // tm
    dest3 = [d.reshape(n_tiles, 1, tm) for d in dest]
    row = lambda i: (i, 0)
    dest_spec = lambda index: pl.BlockSpec((None, 1, tm), index, memory_space=pltpu.SMEM)
    this_tile = dest_spec(lambda i: (i, 0, 0))
    next_tile = dest_spec(lambda i: (jnp.minimum(i + 1, n_tiles - 1), 0, 0))
    return pl.pallas_call(
        functools.partial(_combine_kernel, tm=tm),
        grid=(n_tiles,),
        in_specs=[this_tile, this_tile, next_tile, next_tile,
                  pl.BlockSpec(memory_space=pl.ANY),
                  pl.BlockSpec((tm, D_MODEL), row), pl.BlockSpec((tm, LANES), row),
                  pl.BlockSpec((1, D_MODEL), lambda i: (0, 0))],
        out_specs=pl.BlockSpec((tm, D_MODEL), row),
        out_shape=jax.ShapeDtypeStruct((T, D_MODEL), F32),
        scratch_shapes=[pltpu.VMEM((2, 2, tm * ROW_TILES, LANES), F32), pltpu.SemaphoreType.DMA((2,))],
        compiler_params=pltpu.CompilerParams(dimension_semantics=("arbitrary",), vmem_limit_bytes=VMEM_LIMIT),
        name="moe_combine",
    )(*dest3, *dest3, ys, xn, gate, packed["final_norm"])


def _pack_weights(attn_norm, w_in, q_norm, w_uq, kv_norm, w_ukv, w_out, ffn_norm, router_group,
                  router_group_bias, router_expert, router_expert_bias, w_gate, w_up, w_down, final_norm):
    l = 0
    wi = w_in[l]
    o_q, o_kv, o_pe = 0, Q_RANK, Q_RANK + KV_RANK
    o_dq = o_pe + MLA_ROPE
    o_dk = o_dq + D_DIFF
    o_dv = o_dk + D_DIFF
    cat = lambda parts: jnp.concatenate(parts, axis=-1)
    zeros = lambda like, n: jnp.zeros(like.shape[:-1] + (n,), F32)
    h_rope, h_diff = MLA_ROPE // 2, DIFF_ROT // 2
    gap = LANES // 2 - h_rope

    def diff_tiles(w):
        t = w.reshape(w.shape[0], DIFF_HEADS, 2, DIFF_HEAD_DIM)
        a, b = t[:, :, 0], t[:, :, 1]
        x1, x2, rest = slice(0, h_diff), slice(h_diff, DIFF_ROT), slice(DIFF_ROT, DIFF_HEAD_DIM)
        return cat([a[..., x1], b[..., x1], a[..., rest], a[..., x2], b[..., x2], b[..., rest]]).reshape(w.shape)

    w_pe = wi[:, o_pe:o_dq]
    pe_tile = cat([w_pe[:, :h_rope], zeros(w_pe, gap), w_pe[:, h_rope:], zeros(w_pe, gap)])
    w1 = cat([wi[:, o_q:o_pe], pe_tile, diff_tiles(wi[:, o_dq:o_dk]), diff_tiles(wi[:, o_dk:o_dv])]).astype(BF16)
    wdvt = wi[:, o_dv:].T.astype(BF16)

    def mla_tiles(nope, rope):
        r1 = zeros(nope, h_rope) if rope is None else rope[..., :h_rope]
        r2 = zeros(nope, h_rope) if rope is None else rope[..., h_rope:]
        return cat([r1, nope[..., :gap], r2, nope[..., gap:], zeros(nope, LANES - MLA_NOPE - MLA_ROPE)])

    wq3 = w_uq[l].reshape(Q_RANK, MLA_HEADS, MLA_NOPE + MLA_ROPE)
    wq = mla_tiles(wq3[..., :MLA_NOPE], wq3[..., MLA_NOPE:]).reshape(Q_RANK, D_MLA_PAD).astype(BF16)

    wkv3 = w_ukv[l].reshape(KV_RANK, MLA_HEADS, MLA_NOPE + MLA_V)
    padk = lambda a: jnp.pad(a, ((0, 0), (0, 0), (0, LANES - a.shape[-1]))).reshape(KV_RANK, D_MLA_PAD)
    wvt = padk(wkv3[:, :, MLA_NOPE:]).T.astype(BF16)
    place = np.zeros((LANES, MLA_HEADS, LANES), np.float32)
    for r in range(h_rope):
        for lane_ in (ROT_X1_LANES[0] + r, ROT_X1_LANES[0] + LANES // 2 + r):
            place[lane_, :, lane_] = 1.0
    place = jnp.asarray(place.reshape(LANES, D_MLA_PAD))
    wk_nope = mla_tiles(wkv3[:, :, :MLA_NOPE], None).reshape(KV_RANK, D_MLA_PAD)
    wk = jnp.concatenate([wk_nope, place, place], axis=0).astype(BF16)

    wo = w_out[l]
    n_mla = MLA_HEADS * MLA_V
    woa = jnp.pad(wo[:n_mla].reshape(MLA_HEADS, MLA_V, D_MODEL),
                  ((0, 0), (0, LANES - MLA_V), (0, 0))).reshape(D_MLA_PAD, D_MODEL).astype(BF16)
    padl = lambda a: jnp.pad(a, ((0, 0), (0, LANES - a.shape[-1])))
    return {
        "attn_norm": attn_norm[l][None], "w1": w1, "q_norm": q_norm[l][None], "wq": wq,
        "kv_norm": kv_norm[l][None], "wk": wk, "wvt": wvt, "wdvt": wdvt,
        "woa": woa, "wob": wo[n_mla:].astype(BF16), "ffn_norm": ffn_norm[l][None],
        "wrg": padl(router_group[l]).astype(BF16), "brg": padl(router_group_bias[l][None]),
        "wre": padl(router_expert[l]).astype(BF16), "bre": padl(router_expert_bias[l][None]),
        "wg": w_gate[l], "wu": w_up[l], "wd": w_down[l],
        "final_norm": final_norm[None],
    }


def _cos_sin(seq, rot):
    half = rot // 2
    inv = ROPE_THETA ** (-jnp.arange(half, dtype=F32) / half)
    ang = jnp.arange(seq, dtype=F32)[:, None] * inv[None, :]
    return jnp.cos(ang), jnp.sin(ang)


N_ROPE_TABLES = 4


def _rope_tables(seq):
    lane = np.arange(LANES)
    first = (lane >= ROT_X1_LANES[0]) & (lane < ROT_X1_LANES[1])
    second = np.roll(first, LANES // 2)

    def table(half, cos, sin, scale=1.0):
        cos_l = jnp.tile(cos, (1, LANES // half))
        sin_l = jnp.tile(sin, (1, LANES // half))
        return [jnp.where(first | second, cos_l, 1.0) * scale,
                (jnp.where(second, sin_l, 0.0) - jnp.where(first, sin_l, 0.0)) * scale]

    c, s = _cos_sin(seq, MLA_ROPE)
    tq = table(MLA_ROPE // 2, c, s, (MLA_NOPE + MLA_ROPE) ** -0.5 * LOG2E)
    tpe = table(MLA_ROPE // 2, c, s)
    c, s = _cos_sin(seq, DIFF_ROT)
    tdq = table(DIFF_ROT // 2, c, s, DIFF_HEAD_DIM ** -0.5 * LOG2E)
    tdk = table(DIFF_ROT // 2, c, s)
    return jnp.concatenate(tq + tpe + tdq + tdk, axis=1)


def _forward(x, packed, tables, lams, subln):
    B, S, D = x.shape
    T = B * S
    x2 = x.reshape(T, D)
    q, k, vt, dq, dk, dvt = _proj_call(x2, packed, tables, S, tm=512)
    r3 = lambda a: a.reshape(B, S, a.shape[-1])
    a = _mla_call(r3(q), r3(k), vt, tq=512)
    b = _diff_call(lams, subln, r3(dq), r3(dk), dvt, tq=512)
    xn, hn, gate, route, count = _mix_call(x2, a.reshape(T, -1), b.reshape(T, -1), packed, tm=512)
    dest, plan = _route_plan(route, count, tile_rows=256)
    xs = _dispatch_call(dest, hn, tm=512)
    ys = _expert_call(plan, xs, packed, tm=256)
    y = _combine_call(dest, ys, xn, gate, packed, tm=512)
    return y.reshape(B, S, D)


def kernel(x_prompt, x_sample, attn_norm, w_in, q_norm, w_uq, kv_norm, w_ukv, diff_lq1, diff_lk1, diff_lq2,
           diff_lk2, diff_subln, w_out, ffn_norm, router_group, router_group_bias, router_expert,
           router_expert_bias, w_gate, w_up, w_down, final_norm):
    packed = _pack_weights(attn_norm, w_in, q_norm, w_uq, kv_norm, w_ukv, w_out, ffn_norm, router_group,
                           router_group_bias, router_expert, router_expert_bias, w_gate, w_up, w_down,
                           final_norm)
    lams = (diff_lq1, diff_lk1, diff_lq2, diff_lk2)
    subln = diff_subln
    tables = _rope_tables(max(x_prompt.shape[1], x_sample.shape[1]))
    return (_forward(x_prompt, packed, tables, lams, subln), _forward(x_sample, packed, tables, lams, subln))
```

```python
import functools
import math

import numpy as np
import jax
import jax.numpy as jnp
from jax import lax
from jax.experimental import pallas as pl
from jax.experimental.pallas import tpu as pltpu

D_MODEL = 1024
MLA_HEADS = 8
MLA_NOPE = 64
MLA_ROPE = 32
MLA_V = 64
Q_RANK = 256
KV_RANK = 128
DIFF_HEADS = 4
DIFF_HEAD_DIM = 64
DIFF_ROT = DIFF_HEAD_DIM // 4
ROPE_THETA = 500000.0
N_GROUPS = 4
EXPERTS_PER_GROUP = 8
N_EXPERTS = N_GROUPS * EXPERTS_PER_GROUP
EXPERT_FF = 256
EPS = 1e-6
LAMBDA_INIT = 0.8 - 0.6 * math.exp(-0.3 * 0)

LANES = 128
D_DIFF = DIFF_HEADS * 2 * DIFF_HEAD_DIM
D_MLA_PAD = MLA_HEADS * LANES
LOG2E = math.log2(math.e)
VMEM_LIMIT = 56 * 1024 * 1024

BF16 = jnp.bfloat16
F32 = jnp.float32

_C_Q = 0
_C_KV = _C_Q + Q_RANK
_C_PE = _C_KV + KV_RANK
_C_DQ = _C_PE + LANES
_C_DK = _C_DQ + D_DIFF
_C_END = _C_DK + D_DIFF


def _rms(x, g):
    return x * lax.rsqrt(jnp.mean(x * x, axis=-1, keepdims=True) + EPS) * g


def _dot(a, b):
    return jnp.dot(a, b, preferred_element_type=F32)


def _dot_nt(a, b):
    return lax.dot_general(a, b, (((1,), (1,)), ((), ())), preferred_element_type=F32)


ROW_TILES = D_MODEL // LANES
assert ROW_TILES == 8


def _token_tile(ref, t):
    return ref.at[pl.ds(pl.multiple_of(t * ROW_TILES, ROW_TILES), ROW_TILES)]


def _store_row_tiles(ref, x):
    for j in range(ROW_TILES):
        ref[pl.ds(j, x.shape[0], stride=ROW_TILES), :] = x[:, j * LANES:(j + 1) * LANES]


def _load_row_tiles(ref):
    rows = ref.shape[0] // ROW_TILES
    return jnp.concatenate([ref[pl.ds(j, rows, stride=ROW_TILES), :] for j in range(ROW_TILES)], axis=1)


ROT_X1_LANES = (0, 16)
DIFF_MAP_A_LANES = ((0, 8), (16, 72))


def _rotary(x, table):
    return x * table[:, 0:LANES] + pltpu.roll(x, LANES // 2, 1) * table[:, LANES:2 * LANES]


def _proj_kernel(x_ref, an_ref, w1_ref, qn_ref, wq_ref, kvn_ref, wk_ref, wvt_ref, wdvt_ref,
                 tq_ref, tpe_ref, tdq_ref, tdk_ref,
                 q_ref, k_ref, vt_ref, dq_ref, dk_ref, dvt_ref):
    hn = _rms(x_ref[...], an_ref[...]).astype(BF16)

    def proj(lo, hi):
        return _dot(hn, w1_ref[:, lo:hi])

    cq_raw = proj(_C_Q, _C_KV)
    kv_pe = proj(_C_KV, _C_DQ)
    dq, dk = proj(_C_DQ, _C_DK), proj(_C_DK, _C_END)
    dvt_ref[...] = _dot_nt(wdvt_ref[...], hn).astype(BF16)

    cq = _rms(cq_raw, qn_ref[...]).astype(BF16)
    tq = tq_ref[...]
    qa = _dot(cq, wq_ref[...])
    for h in range(MLA_HEADS):
        blk = slice(h * LANES, (h + 1) * LANES)
        q_ref[:, blk] = _rotary(qa[:, blk], tq).astype(BF16)

    ckv = _rms(kv_pe[:, 0:KV_RANK], kvn_ref[...]).astype(BF16)
    pe = _rotary(kv_pe[:, KV_RANK:], tpe_ref[...])
    pe_hi = pe.astype(BF16)
    pe_lo = (pe - pe_hi.astype(F32)).astype(BF16)
    k_all = _dot(jnp.concatenate([ckv, pe_hi, pe_lo], axis=1), wk_ref[...])
    k_ref[...] = k_all.astype(BF16)
    vt = _dot_nt(wvt_ref[...], ckv)
    ones_row = (lax.broadcasted_iota(jnp.int32, vt.shape, 0) & (LANES - 1)) == MLA_V
    vt_ref[...] = jnp.where(ones_row, 1.0, vt).astype(BF16)

    tdq, tdk = tdq_ref[...], tdk_ref[...]
    for h in range(DIFF_HEADS):
        blk = slice(h * LANES, (h + 1) * LANES)
        dq_ref[:, blk] = _rotary(dq[:, blk], tdq).astype(BF16)
        dk_ref[:, blk] = _rotary(dk[:, blk], tdk).astype(BF16)


def _proj_call(x2, packed, tables, seq, tm):
    T = x2.shape[0]
    nper = seq // tm
    B = T // seq
    row = lambda i: (i, 0)
    const = lambda i: (0, 0)
    tab = lambda i: (i % nper, 0)
    col = lambda i: (i // nper, 0, i % nper)
    full = lambda a: pl.BlockSpec(a.shape, const)
    tspecs = [pl.BlockSpec((tm, 2 * LANES), lambda i, j=j: (i % nper, j)) for j in range(N_ROPE_TABLES)]
    tok = lambda d: (jax.ShapeDtypeStruct((T, d), BF16), pl.BlockSpec((tm, d), row))
    tr = lambda d: (jax.ShapeDtypeStruct((B, d, seq), BF16), pl.BlockSpec((None, d, tm), col))
    outs, ospecs = zip(tok(D_MLA_PAD), tok(D_MLA_PAD), tr(D_MLA_PAD), tok(D_DIFF), tok(D_DIFF), tr(D_DIFF))
    ws = [packed[k] for k in ("attn_norm", "w1", "q_norm", "wq", "kv_norm", "wk", "wvt", "wdvt")]
    return pl.pallas_call(
        _proj_kernel,
        grid=(T // tm,),
        in_specs=[pl.BlockSpec((tm, D_MODEL), row)] + [full(w) for w in ws] + tspecs,
        out_specs=ospecs,
        out_shape=outs,
        compiler_params=pltpu.CompilerParams(dimension_semantics=("arbitrary",), vmem_limit_bytes=VMEM_LIMIT),
        name="proj",
    )(x2, *ws, *([tables] * N_ROPE_TABLES))


def _block_index_maps(n_heads, n_qblocks, n_blocks):
    per_batch = n_heads * n_qblocks

    def split(blk):
        b = blk // per_batch
        r = blk % per_batch
        return b, r // n_qblocks, r % n_qblocks

    def rows(blk):
        b, h, i = split(blk)
        return b, i, h

    def keys(blk):
        b, h, _ = split(blk)
        return b, 0, h

    def values_t(blk):
        b, h, _ = split(blk)
        return b, h, 0

    cur = lambda t: jnp.minimum(t, n_blocks - 1)
    prev = lambda t: jnp.clip(t - 1, 0, n_blocks - 1)
    out = lambda t: jnp.maximum(t - 2, 0)
    return (lambda t: rows(cur(t)), lambda t: keys(cur(t)), lambda t: rows(out(t)), lambda t: values_t(prev(t)))


KEY_CHUNK = 256


def _key_chunk(c):
    return slice(c * KEY_CHUNK, (c + 1) * KEY_CHUNK)


def _running_max(m, st):
    mc = jnp.max(st, axis=0, keepdims=True)
    return mc if m is None else jnp.maximum(m, mc)


def _accumulate(acc, part):
    return part if acc is None else acc + part


def _pipelined_steps(step, zero_scratch):
    t = pl.program_id(0)

    @pl.when(t == 0)
    def _():
        zero_scratch()

    @pl.when(t % 2 == 0)
    def _():
        step(1, 0)

    @pl.when(t % 2 == 1)
    def _():
        step(0, 1)


def _mla_kernel(q_ref, k_ref, vt_ref, o_ref, s0_ref, s1_ref, m0_ref, m1_ref, o0_ref, o1_ref):
    s_refs, m_refs, o_refs = (s0_ref, s1_ref), (m0_ref, m1_ref), (o0_ref, o1_ref)

    def step(src, dst):
        done = o_refs[src][...]
        o_ref[...] = (done / done[MLA_V:MLA_V + 1, :]).T.astype(BF16)

        q = q_ref[...]
        m_src = m_refs[src][...]
        scores = lambda c: _dot_nt(k_ref[_key_chunk(c), :], q)
        n_chunks = k_ref.shape[0] // KEY_CHUNK
        ahead, ot, m_run = scores(0), None, None
        for c in range(n_chunks):
            keys = _key_chunk(c)
            st, ahead = ahead, (scores(c + 1) if c + 1 < n_chunks else None)
            s_refs[dst][keys, :] = st
            m_run = _running_max(m_run, st)
            pt = jnp.exp2(s_refs[src][keys, :] - m_src).astype(BF16)
            ot = _accumulate(ot, _dot(vt_ref[:, keys], pt))
        m_refs[dst][...] = m_run
        o_refs[dst][...] = ot

    def zero_scratch():
        s1_ref[...] = jnp.zeros_like(s1_ref)
        m1_ref[...] = jnp.zeros_like(m1_ref)
        o1_ref[...] = jnp.ones_like(o1_ref)

    _pipelined_steps(step, zero_scratch)


def _mla_call(q, k, vt, tq):
    B, S, _ = q.shape
    n_blocks = B * MLA_HEADS * (S // tq)
    q_map, k_map, o_map, vt_map = _block_index_maps(MLA_HEADS, S // tq, n_blocks)
    return pl.pallas_call(
        _mla_kernel,
        grid=(n_blocks + 2,),
        in_specs=[pl.BlockSpec((None, tq, LANES), q_map), pl.BlockSpec((None, S, LANES), k_map),
                  pl.BlockSpec((None, LANES, S), vt_map)],
        out_specs=pl.BlockSpec((None, tq, LANES), o_map),
        out_shape=jax.ShapeDtypeStruct((B, S, D_MLA_PAD), BF16),
        scratch_shapes=([pltpu.VMEM((S, tq), F32)] * 2 + [pltpu.VMEM((1, tq), F32)] * 2
                        + [pltpu.VMEM((LANES, tq), F32)] * 2),
        compiler_params=pltpu.CompilerParams(dimension_semantics=("arbitrary",), vmem_limit_bytes=VMEM_LIMIT),
        name="mla_attn",
    )(q, k, vt)


def _diff_kernel(lq1_ref, lk1_ref, lq2_ref, lk2_ref, g_ref, q_ref, k_ref, vt_ref, o_ref,
                 sa0_ref, sa1_ref, sb0_ref, sb1_ref, ma0_ref, ma1_ref, mb0_ref, mb1_ref, o0_ref, o1_ref):
    sa_refs, sb_refs = (sa0_ref, sa1_ref), (sb0_ref, sb1_ref)
    ma_refs, mb_refs = (ma0_ref, ma1_ref), (mb0_ref, mb1_ref)
    o_refs = (o0_ref, o1_ref)

    def step(src, dst):
        o_ref[...] = (_rms(o_refs[src][...].T, g_ref[...]) * (1.0 - LAMBDA_INIT)).astype(BF16)

        lam = (jnp.exp(jnp.sum(lq1_ref[...] * lk1_ref[...], axis=-1, keepdims=True))
               - jnp.exp(jnp.sum(lq2_ref[...] * lk2_ref[...], axis=-1, keepdims=True)) + LAMBDA_INIT)
        q = q_ref[...]
        lane = lax.broadcasted_iota(jnp.int32, q.shape, 1)
        (a0, a1), (a2, a3) = DIFF_MAP_A_LANES
        first = ((lane >= a0) & (lane < a1)) | ((lane >= a2) & (lane < a3))
        zero = jnp.zeros_like(q)
        qs = (jnp.where(first, q, zero), jnp.where(first, zero, q))
        s_maps, m_maps = (sa_refs, sb_refs), (ma_refs, mb_refs)
        m_src = [m_refs[src][...] for m_refs in m_maps]
        scores = lambda c: [_dot_nt(k_ref[_key_chunk(c), :], qc) for qc in qs]
        n_chunks = k_ref.shape[0] // KEY_CHUNK
        ahead = scores(0)
        m_run, tot, ot = [None, None], [None, None], [None, None]
        for c in range(n_chunks):
            keys = _key_chunk(c)
            now, ahead = ahead, (scores(c + 1) if c + 1 < n_chunks else None)
            vt = vt_ref[:, keys]
            for i in range(2):
                s_maps[i][dst][keys, :] = now[i]
                m_run[i] = _running_max(m_run[i], now[i])
                e = jnp.exp2(s_maps[i][src][keys, :] - m_src[i])
                tot[i] = _accumulate(tot[i], jnp.sum(e, axis=0, keepdims=True))
                ot[i] = _accumulate(ot[i], _dot(vt, e.astype(BF16)))
        for i in range(2):
            m_maps[i][dst][...] = m_run[i]
        o_refs[dst][...] = ot[0] * (1.0 / tot[0]) - ot[1] * (lam / tot[1])

    def zero_scratch():
        for ref in (sa1_ref, sb1_ref, ma1_ref, mb1_ref, o1_ref):
            ref[...] = jnp.zeros_like(ref)

    _pipelined_steps(step, zero_scratch)


def _diff_call(lams, subln, dq, dk, dvt, tq):
    B, S, _ = dq.shape
    n_blocks = B * DIFF_HEADS * (S // tq)
    q_map, k_map, o_map, vt_map = _block_index_maps(DIFF_HEADS, S // tq, n_blocks)
    small = lambda a: pl.BlockSpec(a.shape, lambda t: (0, 0))
    return pl.pallas_call(
        _diff_kernel,
        grid=(n_blocks + 2,),
        in_specs=[small(a) for a in lams] + [small(subln), pl.BlockSpec((None, tq, LANES), q_map),
                                             pl.BlockSpec((None, S, LANES), k_map),
                                             pl.BlockSpec((None, LANES, S), vt_map)],
        out_specs=pl.BlockSpec((None, tq, LANES), o_map),
        out_shape=jax.ShapeDtypeStruct((B, S, D_DIFF), BF16),
        scratch_shapes=([pltpu.VMEM((S, tq), F32)] * 4 + [pltpu.VMEM((1, tq), F32)] * 4
                        + [pltpu.VMEM((LANES, tq), F32)] * 2),
        compiler_params=pltpu.CompilerParams(dimension_semantics=("arbitrary",), vmem_limit_bytes=VMEM_LIMIT),
        name="diff_attn",
    )(*lams, subln, dq, dk, dvt)


def _mix_kernel(x_ref, a_ref, b_ref, woa_ref, wob_ref, fn_ref, wrg_ref, brg_ref, wre_ref, bre_ref, earlier_ref,
                xn_ref, hn_ref, gate_ref, route_ref, count_ref, base_ref, lg0_ref, lg1_ref, le0_ref, le1_ref):
    t = pl.program_id(0)
    group_refs, expert_refs = (lg0_ref, lg1_ref), (le0_ref, le1_ref)

    def step(src, dst):
        xn = x_ref[...] + _dot(a_ref[...], woa_ref[...]) + _dot(b_ref[...], wob_ref[...])
        xn_ref[...] = xn
        hn32 = _rms(xn, fn_ref[...])
        _store_row_tiles(hn_ref, hn32)
        hn = hn32.astype(BF16)

        lane = lax.broadcasted_iota(jnp.int32, (xn.shape[0], LANES), 1)
        neg = jnp.float32(-jnp.inf)
        big = jnp.int32(LANES)

        def first_lane_of_max(vals):
            top = jnp.max(vals, axis=-1, keepdims=True)
            return top, jnp.min(jnp.where(vals == top, lane, big), axis=-1, keepdims=True)

        gl = jnp.where(lane < N_GROUPS, group_refs[src][...], neg)
        ge = jnp.exp(gl - jnp.max(gl, axis=-1, keepdims=True))
        gp = ge / jnp.sum(ge, axis=-1, keepdims=True)
        p_g, g_idx = first_lane_of_max(gp)

        in_group = (lane < N_EXPERTS) & ((lane & -EXPERTS_PER_GROUP) == g_idx * EXPERTS_PER_GROUP)
        wl = jnp.where(in_group, expert_refs[src][...], neg)
        we = jnp.exp(wl - jnp.max(wl, axis=-1, keepdims=True))
        wp = jnp.where(in_group, we / jnp.sum(we, axis=-1, keepdims=True), -1.0)
        w_a, i_a = first_lane_of_max(wp)
        w_b, i_b = first_lane_of_max(jnp.where(lane == i_a, -1.0, wp))
        tot = w_a + w_b
        gate_ref[...] = (jnp.where(lane == 0, p_g * (w_a / tot), 0.0)
                         + jnp.where(lane == 1, p_g * (w_b / tot), 0.0))

        picked = jnp.where((lane == i_a) | (lane == i_b), 1.0, 0.0)
        before = base_ref[...] + _dot(earlier_ref[...], picked.astype(BF16))
        rank_a = jnp.sum(jnp.where(lane == i_a, before, 0.0), axis=-1, keepdims=True).astype(jnp.int32)
        rank_b = jnp.sum(jnp.where(lane == i_b, before, 0.0), axis=-1, keepdims=True).astype(jnp.int32)
        zero = jnp.zeros_like(lane)
        route_ref[...] = (jnp.where(lane == 0, i_a, zero) + jnp.where(lane == 1, i_b, zero)
                          + jnp.where(lane == 2, rank_a, zero) + jnp.where(lane == 3, rank_b, zero))
        base_ref[...] += jnp.where(t > 0, jnp.sum(picked, axis=0, keepdims=True), 0.0)
        count_ref[...] = base_ref[...]

        group_refs[dst][...] = _dot(hn, wrg_ref[...]) + brg_ref[...]
        expert_refs[dst][...] = _dot(hn, wre_ref[...]) + bre_ref[...]

    def zero_scratch():
        for ref in (base_ref, lg1_ref, le1_ref):
            ref[...] = jnp.zeros_like(ref)

    _pipelined_steps(step, zero_scratch)


def _mix_call(x2, a2, b2, packed, tm):
    T = x2.shape[0]
    n_tiles = T // tm
    cur = lambda t: (jnp.minimum(t, n_tiles - 1), 0)
    prev = lambda t: (jnp.maximum(t - 1, 0), 0)
    full = lambda a: pl.BlockSpec(a.shape, lambda t: (0, 0))
    ws = [packed[k] for k in ("woa", "wob", "ffn_norm", "wrg", "brg", "wre", "bre")]
    ws.append(jnp.asarray(np.tril(np.ones((tm, tm), np.float32), -1), BF16))
    return pl.pallas_call(
        _mix_kernel,
        grid=(n_tiles + 1,),
        in_specs=[pl.BlockSpec((tm, D_MODEL), cur), pl.BlockSpec((tm, D_MLA_PAD), cur),
                  pl.BlockSpec((tm, D_DIFF), cur)] + [full(w) for w in ws],
        out_specs=[pl.BlockSpec((tm, D_MODEL), cur), pl.BlockSpec((tm * ROW_TILES, LANES), cur),
                   pl.BlockSpec((tm, LANES), prev), pl.BlockSpec((tm, LANES), prev),
                   pl.BlockSpec((1, LANES), lambda t: (0, 0))],
        out_shape=[jax.ShapeDtypeStruct((T, D_MODEL), F32), jax.ShapeDtypeStruct((T * ROW_TILES, LANES), F32),
                   jax.ShapeDtypeStruct((T, LANES), F32), jax.ShapeDtypeStruct((T, LANES), jnp.int32),
                   jax.ShapeDtypeStruct((1, LANES), F32)],
        scratch_shapes=[pltpu.VMEM((1, LANES), F32)] + [pltpu.VMEM((tm, LANES), F32)] * 4,
        compiler_params=pltpu.CompilerParams(dimension_semantics=("arbitrary",), vmem_limit_bytes=VMEM_LIMIT),
        name="mix_router",
    )(x2, a2, b2, *ws)


def _route_plan(route, count, tile_rows):
    T = route.shape[0]
    P = 2 * T
    nt = P // tile_rows
    i32 = jnp.int32
    eid = jnp.arange(N_EXPERTS, dtype=i32)
    counts = count[0, :N_EXPERTS].astype(i32)
    seg_end = jnp.sum(jnp.where(eid[None, :] <= eid[:, None], counts[None, :], 0), axis=1)
    seg_start = seg_end - counts
    dest = tuple((jnp.take(seg_start, route[:, pick], mode="clip") + route[:, 2 + pick]).astype(i32)
                 for pick in range(2))

    tile_start = jnp.arange(nt, dtype=i32) * tile_rows
    slot_of_tile = jnp.arange(nt, dtype=i32) + jnp.sum(seg_start[None, :] < tile_start[:, None], axis=1).astype(i32)
    slot_of_seg = eid + jnp.minimum(seg_start // tile_rows + 1, nt)
    vals = jnp.concatenate([tile_start, seg_start])
    slots = jnp.concatenate([slot_of_tile, slot_of_seg])
    n_items = nt + N_EXPERTS
    bounds = jnp.sum(jnp.where(slots[None, :] == jnp.arange(n_items, dtype=i32)[:, None], vals[None, :], 0), axis=1)
    nxt = jnp.concatenate([bounds[1:], jnp.full((1,), P, i32)])
    tile = jnp.minimum(bounds // tile_rows, nt - 1)
    lo = bounds - tile * tile_rows
    hi = nxt - tile * tile_rows
    expert = jnp.minimum(jnp.sum(seg_end[None, :] <= bounds[:, None], axis=1).astype(i32), N_EXPERTS - 1)
    return dest, (tile, expert, lo, hi)


def _dispatch_kernel(dest_a_ref, dest_b_ref, hn_ref, xs_hbm, sem, *, tm):
    dests = (dest_a_ref, dest_b_ref)

    def issue(r, carry):
        for pick in range(2):
            pltpu.make_async_copy(_token_tile(hn_ref, r), _token_tile(xs_hbm, dests[pick][0, r]),
                                  sem).start(priority=pick)
        return carry

    lax.fori_loop(0, tm, issue, 0, unroll=4)
    for _ in range(2):
        pltpu.make_async_copy(hn_ref, xs_hbm.at[pl.ds(0, tm * ROW_TILES)], sem).wait()


def _dispatch_call(dest, hn, tm):
    T = hn.shape[0] // ROW_TILES
    dest3 = [d.reshape(T // tm, 1, tm) for d in dest]
    dest_spec = pl.BlockSpec((None, 1, tm), lambda i: (i, 0, 0), memory_space=pltpu.SMEM)
    return pl.pallas_call(
        functools.partial(_dispatch_kernel, tm=tm),
        grid=(T // tm,),
        in_specs=[dest_spec, dest_spec, pl.BlockSpec((tm * ROW_TILES, LANES), lambda i: (i, 0))],
        out_specs=pl.BlockSpec(memory_space=pl.ANY),
        out_shape=jax.ShapeDtypeStruct((2 * T * ROW_TILES, LANES), F32),
        scratch_shapes=[pltpu.SemaphoreType.DMA],
        compiler_params=pltpu.CompilerParams(dimension_semantics=("arbitrary",)),
        name="moe_dispatch",
    )(*dest3, hn)


def _expert_kernel(tile_ref, exp_ref, lo_ref, hi_ref, xs_ref, wg_ref, wu_ref, wd_ref, ys_ref,
                   wg_bf, wu_bf, wd_bf):
    w = pl.program_id(0)
    lo = lo_ref[w]
    hi = hi_ref[w]
    before = jnp.maximum(w - 1, 0)
    first_visit = (w == 0) | (tile_ref[w] != tile_ref[before])

    @pl.when(first_visit)
    def _():
        ys_ref[...] = jnp.zeros_like(ys_ref)

    @pl.when((w == 0) | (exp_ref[w] != exp_ref[before]))
    def _():
        wg_bf[...] = wg_ref[...].astype(BF16)
        wu_bf[...] = wu_ref[...].astype(BF16)
        wd_bf[...] = wd_ref[...].astype(BF16)

    @pl.when(hi > lo)
    def _():
        x = _load_row_tiles(xs_ref).astype(BF16)
        half = x.shape[0] // 2
        gate_up = [(_dot(xh, wg_bf[...]), _dot(xh, wu_bf[...])) for xh in (x[:half], x[half:])]
        y = jnp.concatenate([_dot((g * jax.nn.sigmoid(g) * u).astype(BF16), wd_bf[...]) for g, u in gate_up], axis=0)
        row = lax.broadcasted_iota(jnp.int32, (y.shape[0], 1), 0)
        _store_row_tiles(ys_ref, jnp.where((row >= lo) & (row < hi), y, _load_row_tiles(ys_ref)))


def _expert_call(plan, xs, packed, tm):
    n_items = plan[0].shape[0]
    by_tile = lambda w, tile, exp, lo, hi: (tile[w], 0)
    by_expert = lambda w, tile, exp, lo, hi: (exp[w], 0, 0)
    grid_spec = pltpu.PrefetchScalarGridSpec(
        num_scalar_prefetch=4,
        grid=(n_items,),
        in_specs=[pl.BlockSpec((tm * ROW_TILES, LANES), by_tile),
                  pl.BlockSpec((None, D_MODEL, EXPERT_FF), by_expert),
                  pl.BlockSpec((None, D_MODEL, EXPERT_FF), by_expert),
                  pl.BlockSpec((None, EXPERT_FF, D_MODEL), by_expert)],
        out_specs=pl.BlockSpec((tm * ROW_TILES, LANES), by_tile),
        scratch_shapes=[pltpu.VMEM((D_MODEL, EXPERT_FF), BF16), pltpu.VMEM((D_MODEL, EXPERT_FF), BF16),
                        pltpu.VMEM((EXPERT_FF, D_MODEL), BF16)],
    )
    return pl.pallas_call(
        _expert_kernel,
        grid_spec=grid_spec,
        out_shape=jax.ShapeDtypeStruct(xs.shape, F32),
        compiler_params=pltpu.CompilerParams(dimension_semantics=("arbitrary",), vmem_limit_bytes=VMEM_LIMIT),
        name="moe_experts",
    )(*plan, xs, packed["wg"], packed["wu"], packed["wd"])


def _combine_kernel(dest_a_ref, dest_b_ref, next_a_ref, next_b_ref, ys_hbm, xn_ref, gate_ref, fin_ref, y_ref,
                    buf, sems, *, tm):
    i = pl.program_id(0)

    def start_gather(dests, slot):
        def issue(r, carry):
            for pick in range(2):
                pltpu.make_async_copy(_token_tile(ys_hbm, dests[pick][0, r]),
                                      _token_tile(buf.at[slot, pick], r), sems.at[slot]).start(priority=pick)
            return carry
        lax.fori_loop(0, tm, issue, 0, unroll=4)

    def finish(slot):
        for pick in range(2):
            pltpu.make_async_copy(ys_hbm.at[pl.ds(0, tm * ROW_TILES)], buf.at[slot, pick], sems.at[slot]).wait()
        gate = gate_ref[...]
        y = (xn_ref[...] + gate[:, 0:1] * _load_row_tiles(buf.at[slot, 0])
             + gate[:, 1:2] * _load_row_tiles(buf.at[slot, 1]))
        y_ref[...] = _rms(y, fin_ref[...])

    @pl.when(i == 0)
    def _():
        start_gather((dest_a_ref, dest_b_ref), 0)

    for slot in range(2):
        @pl.when(i % 2 == slot)
        def _(slot=slot):
            @pl.when(i + 1 < pl.num_programs(0))
            def _():
                start_gather((next_a_ref, next_b_ref), 1 - slot)
            finish(slot)


def _combine_call(dest, ys, xn, gate, packed, tm):
    T = xn.shape[0]
    n_tiles = T // tm
    dest3 = [d.reshape(n_tiles, 1, tm) for d in dest]
    row = lambda i: (i, 0)
    dest_spec = lambda index: pl.BlockSpec((None, 1, tm), index, memory_space=pltpu.SMEM)
    this_tile = dest_spec(lambda i: (i, 0, 0))
    next_tile = dest_spec(lambda i: (jnp.minimum(i + 1, n_tiles - 1), 0, 0))
    return pl.pallas_call(
        functools.partial(_combine_kernel, tm=tm),
        grid=(n_tiles,),
        in_specs=[this_tile, this_tile, next_tile, next_tile,
                  pl.BlockSpec(memory_space=pl.ANY),
                  pl.BlockSpec((tm, D_MODEL), row), pl.BlockSpec((tm, LANES), row),
                  pl.BlockSpec((1, D_MODEL), lambda i: (0, 0))],
        out_specs=pl.BlockSpec((tm, D_MODEL), row),
        out_shape=jax.ShapeDtypeStruct((T, D_MODEL), F32),
        scratch_shapes=[pltpu.VMEM((2, 2, tm * ROW_TILES, LANES), F32), pltpu.SemaphoreType.DMA((2,))],
        compiler_params=pltpu.CompilerParams(dimension_semantics=("arbitrary",), vmem_limit_bytes=VMEM_LIMIT),
        name="moe_combine",
    )(*dest3, *dest3, ys, xn, gate, packed["final_norm"])


def _pack_weights(attn_norm, w_in, q_norm, w_uq, kv_norm, w_ukv, w_out, ffn_norm, router_group,
                  router_group_bias, router_expert, router_expert_bias, w_gate, w_up, w_down, final_norm):
    l = 0
    wi = w_in[l]
    o_q, o_kv, o_pe = 0, Q_RANK, Q_RANK + KV_RANK
    o_dq = o_pe + MLA_ROPE
    o_dk = o_dq + D_DIFF
    o_dv = o_dk + D_DIFF
    cat = lambda parts: jnp.concatenate(parts, axis=-1)
    zeros = lambda like, n: jnp.zeros(like.shape[:-1] + (n,), F32)
    h_rope, h_diff = MLA_ROPE // 2, DIFF_ROT // 2
    gap = LANES // 2 - h_rope

    def diff_tiles(w):
        t = w.reshape(w.shape[0], DIFF_HEADS, 2, DIFF_HEAD_DIM)
        a, b = t[:, :, 0], t[:, :, 1]
        x1, x2, rest = slice(0, h_diff), slice(h_diff, DIFF_ROT), slice(DIFF_ROT, DIFF_HEAD_DIM)
        return cat([a[..., x1], b[..., x1], a[..., rest], a[..., x2], b[..., x2], b[..., rest]]).reshape(w.shape)

    w_pe = wi[:, o_pe:o_dq]
    pe_tile = cat([w_pe[:, :h_rope], zeros(w_pe, gap), w_pe[:, h_rope:], zeros(w_pe, gap)])
    w1 = cat([wi[:, o_q:o_pe], pe_tile, diff_tiles(wi[:, o_dq:o_dk]), diff_tiles(wi[:, o_dk:o_dv])]).astype(BF16)
    wdvt = wi[:, o_dv:].T.astype(BF16)

    def mla_tiles(nope, rope):
        r1 = zeros(nope, h_rope) if rope is None else rope[..., :h_rope]
        r2 = zeros(nope, h_rope) if rope is None else rope[..., h_rope:]
        return cat([r1, nope[..., :gap], r2, nope[..., gap:], zeros(nope, LANES - MLA_NOPE - MLA_ROPE)])

    wq3 = w_uq[l].reshape(Q_RANK, MLA_HEADS, MLA_NOPE + MLA_ROPE)
    wq = mla_tiles(wq3[..., :MLA_NOPE], wq3[..., MLA_NOPE:]).reshape(Q_RANK, D_MLA_PAD).astype(BF16)

    wkv3 = w_ukv[l].reshape(KV_RANK, MLA_HEADS, MLA_NOPE + MLA_V)
    padk = lambda a: jnp.pad(a, ((0, 0), (0, 0), (0, LANES - a.shape[-1]))).reshape(KV_RANK, D_MLA_PAD)
    wvt = padk(wkv3[:, :, MLA_NOPE:]).T.astype(BF16)
    place = np.zeros((LANES, MLA_HEADS, LANES), np.float32)
    for r in range(h_rope):
        for lane_ in (ROT_X1_LANES[0] + r, ROT_X1_LANES[0] + LANES // 2 + r):
            place[lane_, :, lane_] = 1.0
    place = jnp.asarray(place.reshape(LANES, D_MLA_PAD))
    wk_nope = mla_tiles(wkv3[:, :, :MLA_NOPE], None).reshape(KV_RANK, D_MLA_PAD)
    wk = jnp.concatenate([wk_nope, place, place], axis=0).astype(BF16)

    wo = w_out[l]
    n_mla = MLA_HEADS * MLA_V
    woa = jnp.pad(wo[:n_mla].reshape(MLA_HEADS, MLA_V, D_MODEL),
                  ((0, 0), (0, LANES - MLA_V), (0, 0))).reshape(D_MLA_PAD, D_MODEL).astype(BF16)
    padl = lambda a: jnp.pad(a, ((0, 0), (0, LANES - a.shape[-1])))
    return {
        "attn_norm": attn_norm[l][None], "w1": w1, "q_norm": q_norm[l][None], "wq": wq,
        "kv_norm": kv_norm[l][None], "wk": wk, "wvt": wvt, "wdvt": wdvt,
        "woa": woa, "wob": wo[n_mla:].astype(BF16), "ffn_norm": ffn_norm[l][None],
        "wrg": padl(router_group[l]).astype(BF16), "brg": padl(router_group_bias[l][None]),
        "wre": padl(router_expert[l]).astype(BF16), "bre": padl(router_expert_bias[l][None]),
        "wg": w_gate[l], "wu": w_up[l], "wd": w_down[l],
        "final_norm": final_norm[None],
    }


def _cos_sin(seq, rot):
    half = rot // 2
    inv = ROPE_THETA ** (-jnp.arange(half, dtype=F32) / half)
    ang = jnp.arange(seq, dtype=F32)[:, None] * inv[None, :]
    return jnp.cos(ang), jnp.sin(ang)


N_ROPE_TABLES = 4


def _rope_tables(seq):
    lane = np.arange(LANES)
    first = (lane >= ROT_X1_LANES[0]) & (lane < ROT_X1_LANES[1])
    second = np.roll(first, LANES // 2)

    def table(half, cos, sin, scale=1.0):
        cos_l = jnp.tile(cos, (1, LANES // half))
        sin_l = jnp.tile(sin, (1, LANES // half))
        return [jnp.where(first | second, cos_l, 1.0) * scale,
                (jnp.where(second, sin_l, 0.0) - jnp.where(first, sin_l, 0.0)) * scale]

    c, s = _cos_sin(seq, MLA_ROPE)
    tq = table(MLA_ROPE // 2, c, s, (MLA_NOPE + MLA_ROPE) ** -0.5 * LOG2E)
    tpe = table(MLA_ROPE // 2, c, s)
    c, s = _cos_sin(seq, DIFF_ROT)
    tdq = table(DIFF_ROT // 2, c, s, DIFF_HEAD_DIM ** -0.5 * LOG2E)
    tdk = table(DIFF_ROT // 2, c, s)
    return jnp.concatenate(tq + tpe + tdq + tdk, axis=1)


def _forward(x, packed, tables, lams, subln):
    B, S, D = x.shape
    T = B * S
    x2 = x.reshape(T, D)
    q, k, vt, dq, dk, dvt = _proj_call(x2, packed, tables, S, tm=512)
    r3 = lambda a: a.reshape(B, S, a.shape[-1])
    a = _mla_call(r3(q), r3(k), vt, tq=512)
    b = _diff_call(lams, subln, r3(dq), r3(dk), dvt, tq=512)
    xn, hn, gate, route, count = _mix_call(x2, a.reshape(T, -1), b.reshape(T, -1), packed, tm=512)
    dest, plan = _route_plan(route, count, tile_rows=256)
    xs = _dispatch_call(dest, hn, tm=512)
    ys = _expert_call(plan, xs, packed, tm=256)
    y = _combine_call(dest, ys, xn, gate, packed, tm=512)
    return y.reshape(B, S, D)


def kernel(x_prompt, x_sample, attn_norm, w_in, q_norm, w_uq, kv_norm, w_ukv, diff_lq1, diff_lk1, diff_lq2,
           diff_lk2, diff_subln, w_out, ffn_norm, router_group, router_group_bias, router_expert,
           router_expert_bias, w_gate, w_up, w_down, final_norm):
    packed = _pack_weights(attn_norm, w_in, q_norm, w_uq, kv_norm, w_ukv, w_out, ffn_norm, router_group,
                           router_group_bias, router_expert, router_expert_bias, w_gate, w_up, w_down,
                           final_norm)
    lams = (diff_lq1, diff_lk1, diff_lq2, diff_lk2)
    subln = diff_subln
    tables = _rope_tables(max(x_prompt.shape[1], x_sample.shape[1]))
    return (_forward(x_prompt, packed, tables, lams, subln), _forward(x_sample, packed, tables, lams, subln))
```

```python
import functools
import math

import numpy as np
import jax
import jax.numpy as jnp
from jax import lax
from jax.experimental import pallas as pl
from jax.experimental.pallas import tpu as pltpu

D_MODEL = 1024
MLA_HEADS = 8
MLA_NOPE = 64
MLA_ROPE = 32
MLA_V = 64
Q_RANK = 256
KV_RANK = 128
DIFF_HEADS = 4
DIFF_HEAD_DIM = 64
DIFF_ROT = DIFF_HEAD_DIM // 4
ROPE_THETA = 500000.0
N_GROUPS = 4
EXPERTS_PER_GROUP = 8
N_EXPERTS = N_GROUPS * EXPERTS_PER_GROUP
EXPERT_FF = 256
EPS = 1e-6
LAMBDA_INIT = 0.8 - 0.6 * math.exp(-0.3 * 0)

LANES = 128
D_DIFF = DIFF_HEADS * 2 * DIFF_HEAD_DIM
D_MLA_PAD = MLA_HEADS * LANES
LOG2E = math.log2(math.e)
VMEM_LIMIT = 56 * 1024 * 1024

BF16 = jnp.bfloat16
F32 = jnp.float32

_C_Q = 0
_C_KV = _C_Q + Q_RANK
_C_PE = _C_KV + KV_RANK
_C_DQ = _C_PE + LANES
_C_DK = _C_DQ + D_DIFF
_C_END = _C_DK + D_DIFF


def _rms(x, g):
    return x * lax.rsqrt(jnp.mean(x * x, axis=-1, keepdims=True) + EPS) * g


def _dot(a, b):
    return jnp.dot(a, b, preferred_element_type=F32)


def _dot_nt(a, b):
    return lax.dot_general(a, b, (((1,), (1,)), ((), ())), preferred_element_type=F32)


ROW_TILES = D_MODEL // LANES
assert ROW_TILES == 8


def _token_tile(ref, t):
    return ref.at[pl.ds(pl.multiple_of(t * ROW_TILES, ROW_TILES), ROW_TILES)]


def _store_row_tiles(ref, x):
    for j in range(ROW_TILES):
        ref[pl.ds(j, x.shape[0], stride=ROW_TILES), :] = x[:, j * LANES:(j + 1) * LANES]


def _load_row_tiles(ref):
    rows = ref.shape[0] // ROW_TILES
    return jnp.concatenate([ref[pl.ds(j, rows, stride=ROW_TILES), :] for j in range(ROW_TILES)], axis=1)


ROT_X1_LANES = (0, 16)
DIFF_MAP_A_LANES = ((0, 8), (16, 72))


def _rotary(x, table):
    return x * table[:, 0:LANES] + pltpu.roll(x, LANES // 2, 1) * table[:, LANES:2 * LANES]


def _proj_kernel(x_ref, an_ref, w1_ref, qn_ref, wq_ref, kvn_ref, wk_ref, wvt_ref, wdvt_ref,
                 tq_ref, tpe_ref, tdq_ref, tdk_ref,
                 q_ref, k_ref, vt_ref, dq_ref, dk_ref, dvt_ref):
    hn = _rms(x_ref[...], an_ref[...]).astype(BF16)

    def proj(lo, hi):
        return _dot(hn, w1_ref[:, lo:hi])

    cq_raw = proj(_C_Q, _C_KV)
    kv_pe = proj(_C_KV, _C_DQ)
    dq, dk = proj(_C_DQ, _C_DK), proj(_C_DK, _C_END)
    dvt_ref[...] = _dot_nt(wdvt_ref[...], hn).astype(BF16)

    cq = _rms(cq_raw, qn_ref[...]).astype(BF16)
    tq = tq_ref[...]
    qa = _dot(cq, wq_ref[...])
    for h in range(MLA_HEADS):
        blk = slice(h * LANES, (h + 1) * LANES)
        q_ref[:, blk] = _rotary(qa[:, blk], tq).astype(BF16)

    ckv = _rms(kv_pe[:, 0:KV_RANK], kvn_ref[...]).astype(BF16)
    pe = _rotary(kv_pe[:, KV_RANK:], tpe_ref[...])
    pe_hi = pe.astype(BF16)
    pe_lo = (pe - pe_hi.astype(F32)).astype(BF16)
    k_all = _dot(jnp.concatenate([ckv, pe_hi, pe_lo], axis=1), wk_ref[...])
    k_ref[...] = k_all.astype(BF16)
    vt = _dot_nt(wvt_ref[...], ckv)
    ones_row = (lax.broadcasted_iota(jnp.int32, vt.shape, 0) & (LANES - 1)) == MLA_V
    vt_ref[...] = jnp.where(ones_row, 1.0, vt).astype(BF16)

    tdq, tdk = tdq_ref[...], tdk_ref[...]
    for h in range(DIFF_HEADS):
        blk = slice(h * LANES, (h + 1) * LANES)
        dq_ref[:, blk] = _rotary(dq[:, blk], tdq).astype(BF16)
        dk_ref[:, blk] = _rotary(dk[:, blk], tdk).astype(BF16)


def _proj_call(x2, packed, tables, seq, tm):
    T = x2.shape[0]
    nper = seq // tm
    B = T // seq
    row = lambda i: (i, 0)
    const = lambda i: (0, 0)
    tab = lambda i: (i % nper, 0)
    col = lambda i: (i // nper, 0, i % nper)
    full = lambda a: pl.BlockSpec(a.shape, const)
    tspecs = [pl.BlockSpec((tm, 2 * LANES), lambda i, j=j: (i % nper, j)) for j in range(N_ROPE_TABLES)]
    tok = lambda d: (jax.ShapeDtypeStruct((T, d), BF16), pl.BlockSpec((tm, d), row))
    tr = lambda d: (jax.ShapeDtypeStruct((B, d, seq), BF16), pl.BlockSpec((None, d, tm), col))
    outs, ospecs = zip(tok(D_MLA_PAD), tok(D_MLA_PAD), tr(D_MLA_PAD), tok(D_DIFF), tok(D_DIFF), tr(D_DIFF))
    ws = [packed[k] for k in ("attn_norm", "w1", "q_norm", "wq", "kv_norm", "wk", "wvt", "wdvt")]
    return pl.pallas_call(
        _proj_kernel,
        grid=(T // tm,),
        in_specs=[pl.BlockSpec((tm, D_MODEL), row)] + [full(w) for w in ws] + tspecs,
        out_specs=ospecs,
        out_shape=outs,
        compiler_params=pltpu.CompilerParams(dimension_semantics=("arbitrary",), vmem_limit_bytes=VMEM_LIMIT),
        name="proj",
    )(x2, *ws, *([tables] * N_ROPE_TABLES))


def _block_index_maps(n_heads, n_qblocks, n_blocks):
    per_batch = n_heads * n_qblocks

    def split(blk):
        b = blk // per_batch
        r = blk % per_batch
        return b, r // n_qblocks, r % n_qblocks

    def rows(blk):
        b, h, i = split(blk)
        return b, i, h

    def keys(blk):
        b, h, _ = split(blk)
        return b, 0, h

    def values_t(blk):
        b, h, _ = split(blk)
        return b, h, 0

    cur = lambda t: jnp.minimum(t, n_blocks - 1)
    prev = lambda t: jnp.clip(t - 1, 0, n_blocks - 1)
    out = lambda t: jnp.maximum(t - 2, 0)
    return (lambda t: rows(cur(t)), lambda t: keys(cur(t)), lambda t: rows(out(t)), lambda t: values_t(prev(t)))


KEY_CHUNK = 256


def _key_chunk(c):
    return slice(c * KEY_CHUNK, (c + 1) * KEY_CHUNK)


def _running_max(m, st):
    mc = jnp.max(st, axis=0, keepdims=True)
    return mc if m is None else jnp.maximum(m, mc)


def _accumulate(acc, part):
    return part if acc is None else acc + part


def _pipelined_steps(step, zero_scratch):
    t = pl.program_id(0)

    @pl.when(t == 0)
    def _():
        zero_scratch()

    @pl.when(t % 2 == 0)
    def _():
        step(1, 0)

    @pl.when(t % 2 == 1)
    def _():
        step(0, 1)


def _mla_kernel(q_ref, k_ref, vt_ref, o_ref, s0_ref, s1_ref, m0_ref, m1_ref, o0_ref, o1_ref):
    s_refs, m_refs, o_refs = (s0_ref, s1_ref), (m0_ref, m1_ref), (o0_ref, o1_ref)

    def step(src, dst):
        done = o_refs[src][...]
        o_ref[...] = (done / done[MLA_V:MLA_V + 1, :]).T.astype(BF16)

        q = q_ref[...]
        m_src = m_refs[src][...]
        scores = lambda c: _dot_nt(k_ref[_key_chunk(c), :], q)
        n_chunks = k_ref.shape[0] // KEY_CHUNK
        ahead, ot, m_run = scores(0), None, None
        for c in range(n_chunks):
            keys = _key_chunk(c)
            st, ahead = ahead, (scores(c + 1) if c + 1 < n_chunks else None)
            s_refs[dst][keys, :] = st
            m_run = _running_max(m_run, st)
            pt = jnp.exp2(s_refs[src][keys, :] - m_src).astype(BF16)
            ot = _accumulate(ot, _dot(vt_ref[:, keys], pt))
        m_refs[dst][...] = m_run
        o_refs[dst][...] = ot

    def zero_scratch():
        s1_ref[...] = jnp.zeros_like(s1_ref)
        m1_ref[...] = jnp.zeros_like(m1_ref)
        o1_ref[...] = jnp.ones_like(o1_ref)

    _pipelined_steps(step, zero_scratch)


def _mla_call(q, k, vt, tq):
    B, S, _ = q.shape
    n_blocks = B * MLA_HEADS * (S // tq)
    q_map, k_map, o_map, vt_map = _block_index_maps(MLA_HEADS, S // tq, n_blocks)
    return pl.pallas_call(
        _mla_kernel,
        grid=(n_blocks + 2,),
        in_specs=[pl.BlockSpec((None, tq, LANES), q_map), pl.BlockSpec((None, S, LANES), k_map),
                  pl.BlockSpec((None, LANES, S), vt_map)],
        out_specs=pl.BlockSpec((None, tq, LANES), o_map),
        out_shape=jax.ShapeDtypeStruct((B, S, D_MLA_PAD), BF16),
        scratch_shapes=([pltpu.VMEM((S, tq), F32)] * 2 + [pltpu.VMEM((1, tq), F32)] * 2
                        + [pltpu.VMEM((LANES, tq), F32)] * 2),
        compiler_params=pltpu.CompilerParams(dimension_semantics=("arbitrary",), vmem_limit_bytes=VMEM_LIMIT),
        name="mla_attn",
    )(q, k, vt)


def _diff_kernel(lq1_ref, lk1_ref, lq2_ref, lk2_ref, g_ref, q_ref, k_ref, vt_ref, o_ref,
                 sa0_ref, sa1_ref, sb0_ref, sb1_ref, ma0_ref, ma1_ref, mb0_ref, mb1_ref, o0_ref, o1_ref):
    sa_refs, sb_refs = (sa0_ref, sa1_ref), (sb0_ref, sb1_ref)
    ma_refs, mb_refs = (ma0_ref, ma1_ref), (mb0_ref, mb1_ref)
    o_refs = (o0_ref, o1_ref)

    def step(src, dst):
        o_ref[...] = (_rms(o_refs[src][...].T, g_ref[...]) * (1.0 - LAMBDA_INIT)).astype(BF16)

        lam = (jnp.exp(jnp.sum(lq1_ref[...] * lk1_ref[...], axis=-1, keepdims=True))
               - jnp.exp(jnp.sum(lq2_ref[...] * lk2_ref[...], axis=-1, keepdims=True)) + LAMBDA_INIT)
        q = q_ref[...]
        lane = lax.broadcasted_iota(jnp.int32, q.shape, 1)
        (a0, a1), (a2, a3) = DIFF_MAP_A_LANES
        first = ((lane >= a0) & (lane < a1)) | ((lane >= a2) & (lane < a3))
        zero = jnp.zeros_like(q)
        qs = (jnp.where(first, q, zero), jnp.where(first, zero, q))
        s_maps, m_maps = (sa_refs, sb_refs), (ma_refs, mb_refs)
        m_src = [m_refs[src][...] for m_refs in m_maps]
        scores = lambda c: [_dot_nt(k_ref[_key_chunk(c), :], qc) for qc in qs]
        n_chunks = k_ref.shape[0] // KEY_CHUNK
        ahead = scores(0)
        m_run, tot, ot = [None, None], [None, None], [None, None]
        for c in range(n_chunks):
            keys = _key_chunk(c)
            now, ahead = ahead, (scores(c + 1) if c + 1 < n_chunks else None)
            vt = vt_ref[:, keys]
            for i in range(2):
                s_maps[i][dst][keys, :] = now[i]
                m_run[i] = _running_max(m_run[i], now[i])
                e = jnp.exp2(s_maps[i][src][keys, :] - m_src[i])
                tot[i] = _accumulate(tot[i], jnp.sum(e, axis=0, keepdims=True))
                ot[i] = _accumulate(ot[i], _dot(vt, e.astype(BF16)))
        for i in range(2):
            m_maps[i][dst][...] = m_run[i]
        o_refs[dst][...] = ot[0] * (1.0 / tot[0]) - ot[1] * (lam / tot[1])

    def zero_scratch():
        for ref in (sa1_ref, sb1_ref, ma1_ref, mb1_ref, o1_ref):
            ref[...] = jnp.zeros_like(ref)

    _pipelined_steps(step, zero_scratch)


def _diff_call(lams, subln, dq, dk, dvt, tq):
    B, S, _ = dq.shape
    n_blocks = B * DIFF_HEADS * (S // tq)
    q_map, k_map, o_map, vt_map = _block_index_maps(DIFF_HEADS, S // tq, n_blocks)
    small = lambda a: pl.BlockSpec(a.shape, lambda t: (0, 0))
    return pl.pallas_call(
        _diff_kernel,
        grid=(n_blocks + 2,),
        in_specs=[small(a) for a in lams] + [small(subln), pl.BlockSpec((None, tq, LANES), q_map),
                                             pl.BlockSpec((None, S, LANES), k_map),
                                             pl.BlockSpec((None, LANES, S), vt_map)],
        out_specs=pl.BlockSpec((None, tq, LANES), o_map),
        out_shape=jax.ShapeDtypeStruct((B, S, D_DIFF), BF16),
        scratch_shapes=([pltpu.VMEM((S, tq), F32)] * 4 + [pltpu.VMEM((1, tq), F32)] * 4
                        + [pltpu.VMEM((LANES, tq), F32)] * 2),
        compiler_params=pltpu.CompilerParams(dimension_semantics=("arbitrary",), vmem_limit_bytes=VMEM_LIMIT),
        name="diff_attn",
    )(*lams, subln, dq, dk, dvt)


def _mix_kernel(x_ref, a_ref, b_ref, woa_ref, wob_ref, fn_ref, wrg_ref, brg_ref, wre_ref, bre_ref, earlier_ref,
                xn_ref, hn_ref, gate_ref, route_ref, count_ref, base_ref, lg0_ref, lg1_ref, le0_ref, le1_ref):
    t = pl.program_id(0)
    group_refs, expert_refs = (lg0_ref, lg1_ref), (le0_ref, le1_ref)

    def step(src, dst):
        xn = x_ref[...] + _dot(a_ref[...], woa_ref[...]) + _dot(b_ref[...], wob_ref[...])
        xn_ref[...] = xn
        hn32 = _rms(xn, fn_ref[...])
        _store_row_tiles(hn_ref, hn32)
        hn = hn32.astype(BF16)

        lane = lax.broadcasted_iota(jnp.int32, (xn.shape[0], LANES), 1)
        neg = jnp.float32(-jnp.inf)
        big = jnp.int32(LANES)

        def first_lane_of_max(vals):
            top = jnp.max(vals, axis=-1, keepdims=True)
            return top, jnp.min(jnp.where(vals == top, lane, big), axis=-1, keepdims=True)

        gl = jnp.where(lane < N_GROUPS, group_refs[src][...], neg)
        ge = jnp.exp(gl - jnp.max(gl, axis=-1, keepdims=True))
        gp = ge / jnp.sum(ge, axis=-1, keepdims=True)
        p_g, g_idx = first_lane_of_max(gp)

        in_group = (lane < N_EXPERTS) & ((lane & -EXPERTS_PER_GROUP) == g_idx * EXPERTS_PER_GROUP)
        wl = jnp.where(in_group, expert_refs[src][...], neg)
        we = jnp.exp(wl - jnp.max(wl, axis=-1, keepdims=True))
        wp = jnp.where(in_group, we / jnp.sum(we, axis=-1, keepdims=True), -1.0)
        w_a, i_a = first_lane_of_max(wp)
        w_b, i_b = first_lane_of_max(jnp.where(lane == i_a, -1.0, wp))
        tot = w_a + w_b
        gate_ref[...] = (jnp.where(lane == 0, p_g * (w_a / tot), 0.0)
                         + jnp.where(lane == 1, p_g * (w_b / tot), 0.0))

        picked = jnp.where((lane == i_a) | (lane == i_b), 1.0, 0.0)
        before = base_ref[...] + _dot(earlier_ref[...], picked.astype(BF16))
        rank_a = jnp.sum(jnp.where(lane == i_a, before, 0.0), axis=-1, keepdims=True).astype(jnp.int32)
        rank_b = jnp.sum(jnp.where(lane == i_b, before, 0.0), axis=-1, keepdims=True).astype(jnp.int32)
        zero = jnp.zeros_like(lane)
        route_ref[...] = (jnp.where(lane == 0, i_a, zero) + jnp.where(lane == 1, i_b, zero)
                          + jnp.where(lane == 2, rank_a, zero) + jnp.where(lane == 3, rank_b, zero))
        base_ref[...] += jnp.where(t > 0, jnp.sum(picked, axis=0, keepdims=True), 0.0)
        count_ref[...] = base_ref[...]

        group_refs[dst][...] = _dot(hn, wrg_ref[...]) + brg_ref[...]
        expert_refs[dst][...] = _dot(hn, wre_ref[...]) + bre_ref[...]

    def zero_scratch():
        for ref in (base_ref, lg1_ref, le1_ref):
            ref[...] = jnp.zeros_like(ref)

    _pipelined_steps(step, zero_scratch)


def _mix_call(x2, a2, b2, packed, tm):
    T = x2.shape[0]
    n_tiles = T // tm
    cur = lambda t: (jnp.minimum(t, n_tiles - 1), 0)
    prev = lambda t: (jnp.maximum(t - 1, 0), 0)
    full = lambda a: pl.BlockSpec(a.shape, lambda t: (0, 0))
    ws = [packed[k] for k in ("woa", "wob", "ffn_norm", "wrg", "brg", "wre", "bre")]
    ws.append(jnp.asarray(np.tril(np.ones((tm, tm), np.float32), -1), BF16))
    return pl.pallas_call(
        _mix_kernel,
        grid=(n_tiles + 1,),
        in_specs=[pl.BlockSpec((tm, D_MODEL), cur), pl.BlockSpec((tm, D_MLA_PAD), cur),
                  pl.BlockSpec((tm, D_DIFF), cur)] + [full(w) for w in ws],
        out_specs=[pl.BlockSpec((tm, D_MODEL), cur), pl.BlockSpec((tm * ROW_TILES, LANES), cur),
                   pl.BlockSpec((tm, LANES), prev), pl.BlockSpec((tm, LANES), prev),
                   pl.BlockSpec((1, LANES), lambda t: (0, 0))],
        out_shape=[jax.ShapeDtypeStruct((T, D_MODEL), F32), jax.ShapeDtypeStruct((T * ROW_TILES, LANES), F32),
                   jax.ShapeDtypeStruct((T, LANES), F32), jax.ShapeDtypeStruct((T, LANES), jnp.int32),
                   jax.ShapeDtypeStruct((1, LANES), F32)],
        scratch_shapes=[pltpu.VMEM((1, LANES), F32)] + [pltpu.VMEM((tm, LANES), F32)] * 4,
        compiler_params=pltpu.CompilerParams(dimension_semantics=("arbitrary",), vmem_limit_bytes=VMEM_LIMIT),
        name="mix_router",
    )(x2, a2, b2, *ws)


def _route_plan(route, count, tile_rows):
    T = route.shape[0]
    P = 2 * T
    nt = P // tile_rows
    i32 = jnp.int32
    eid = jnp.arange(N_EXPERTS, dtype=i32)
    counts = count[0, :N_EXPERTS].astype(i32)
    seg_end = jnp.sum(jnp.where(eid[None, :] <= eid[:, None], counts[None, :], 0), axis=1)
    seg_start = seg_end - counts
    dest = tuple((jnp.sum(jnp.where(route[:, pick, None] == eid, seg_start, 0), axis=-1)
                  + route[:, 2 + pick]).astype(i32) for pick in range(2))

    tile_start = jnp.arange(nt, dtype=i32) * tile_rows
    slot_of_tile = jnp.arange(nt, dtype=i32) + jnp.sum(seg_start[None, :] < tile_start[:, None], axis=1).astype(i32)
    slot_of_seg = eid + jnp.minimum(seg_start // tile_rows + 1, nt)
    vals = jnp.concatenate([tile_start, seg_start])
    slots = jnp.concatenate([slot_of_tile, slot_of_seg])
    n_items = nt + N_EXPERTS
    bounds = jnp.sum(jnp.where(slots[None, :] == jnp.arange(n_items, dtype=i32)[:, None], vals[None, :], 0), axis=1)
    nxt = jnp.concatenate([bounds[1:], jnp.full((1,), P, i32)])
    tile = jnp.minimum(bounds // tile_rows, nt - 1)
    lo = bounds - tile * tile_rows
    hi = nxt - tile * tile_rows
    expert = jnp.minimum(jnp.sum(seg_end[None, :] <= bounds[:, None], axis=1).astype(i32), N_EXPERTS - 1)
    return dest, (tile, expert, lo, hi)


def _dispatch_kernel(dest_a_ref, dest_b_ref, hn_ref, xs_hbm, sem, *, tm):
    dests = (dest_a_ref, dest_b_ref)

    def issue(r, carry):
        for pick in range(2):
            pltpu.make_async_copy(_token_tile(hn_ref, r), _token_tile(xs_hbm, dests[pick][0, r]),
                                  sem).start(priority=pick)
        return carry

    lax.fori_loop(0, tm, issue, 0, unroll=4)
    for _ in range(2):
        pltpu.make_async_copy(hn_ref, xs_hbm.at[pl.ds(0, tm * ROW_TILES)], sem).wait()


def _dispatch_call(dest, hn, tm):
    T = hn.shape[0] // ROW_TILES
    dest3 = [d.reshape(T // tm, 1, tm) for d in dest]
    dest_spec = pl.BlockSpec((None, 1, tm), lambda i: (i, 0, 0), memory_space=pltpu.SMEM)
    return pl.pallas_call(
        functools.partial(_dispatch_kernel, tm=tm),
        grid=(T // tm,),
        in_specs=[dest_spec, dest_spec, pl.BlockSpec((tm * ROW_TILES, LANES), lambda i: (i, 0))],
        out_specs=pl.BlockSpec(memory_space=pl.ANY),
        out_shape=jax.ShapeDtypeStruct((2 * T * ROW_TILES, LANES), F32),
        scratch_shapes=[pltpu.SemaphoreType.DMA],
        compiler_params=pltpu.CompilerParams(dimension_semantics=("arbitrary",)),
        name="moe_dispatch",
    )(*dest3, hn)


def _expert_kernel(tile_ref, exp_ref, lo_ref, hi_ref, xs_ref, wg_ref, wu_ref, wd_ref, ys_ref,
                   wg_bf, wu_bf, wd_bf):
    w = pl.program_id(0)
    lo = lo_ref[w]
    hi = hi_ref[w]
    before = jnp.maximum(w - 1, 0)
    first_visit = (w == 0) | (tile_ref[w] != tile_ref[before])

    @pl.when(first_visit)
    def _():
        ys_ref[...] = jnp.zeros_like(ys_ref)

    @pl.when((w == 0) | (exp_ref[w] != exp_ref[before]))
    def _():
        wg_bf[...] = wg_ref[...].astype(BF16)
        wu_bf[...] = wu_ref[...].astype(BF16)
        wd_bf[...] = wd_ref[...].astype(BF16)

    @pl.when(hi > lo)
    def _():
        x = _load_row_tiles(xs_ref).astype(BF16)
        half = x.shape[0] // 2
        gate_up = [(_dot(xh, wg_bf[...]), _dot(xh, wu_bf[...])) for xh in (x[:half], x[half:])]
        y = jnp.concatenate([_dot((g * jax.nn.sigmoid(g) * u).astype(BF16), wd_bf[...]) for g, u in gate_up], axis=0)
        row = lax.broadcasted_iota(jnp.int32, (y.shape[0], 1), 0)
        _store_row_tiles(ys_ref, jnp.where((row >= lo) & (row < hi), y, _load_row_tiles(ys_ref)))


def _expert_call(plan, xs, packed, tm):
    n_items = plan[0].shape[0]
    by_tile = lambda w, tile, exp, lo, hi: (tile[w], 0)
    by_expert = lambda w, tile, exp, lo, hi: (exp[w], 0, 0)
    grid_spec = pltpu.PrefetchScalarGridSpec(
        num_scalar_prefetch=4,
        grid=(n_items,),
        in_specs=[pl.BlockSpec((tm * ROW_TILES, LANES), by_tile),
                  pl.BlockSpec((None, D_MODEL, EXPERT_FF), by_expert),
                  pl.BlockSpec((None, D_MODEL, EXPERT_FF), by_expert),
                  pl.BlockSpec((None, EXPERT_FF, D_MODEL), by_expert)],
        out_specs=pl.BlockSpec((tm * ROW_TILES, LANES), by_tile),
        scratch_shapes=[pltpu.VMEM((D_MODEL, EXPERT_FF), BF16), pltpu.VMEM((D_MODEL, EXPERT_FF), BF16),
                        pltpu.VMEM((EXPERT_FF, D_MODEL), BF16)],
    )
    return pl.pallas_call(
        _expert_kernel,
        grid_spec=grid_spec,
        out_shape=jax.ShapeDtypeStruct(xs.shape, F32),
        compiler_params=pltpu.CompilerParams(dimension_semantics=("arbitrary",), vmem_limit_bytes=VMEM_LIMIT),
        name="moe_experts",
    )(*plan, xs, packed["wg"], packed["wu"], packed["wd"])


def _combine_kernel(dest_a_ref, dest_b_ref, next_a_ref, next_b_ref, ys_hbm, xn_ref, gate_ref, fin_ref, y_ref,
                    buf, sems, *, tm):
    i = pl.program_id(0)

    def start_gather(dests, slot):
        def issue(r, carry):
            for pick in range(2):
                pltpu.make_async_copy(_token_tile(ys_hbm, dests[pick][0, r]),
                                      _token_tile(buf.at[slot, pick], r), sems.at[slot]).start(priority=pick)
            return carry
        lax.fori_loop(0, tm, issue, 0, unroll=4)

    def finish(slot):
        for pick in range(2):
            pltpu.make_async_copy(ys_hbm.at[pl.ds(0, tm * ROW_TILES)], buf.at[slot, pick], sems.at[slot]).wait()
        gate = gate_ref[...]
        y = (xn_ref[...] + gate[:, 0:1] * _load_row_tiles(buf.at[slot, 0])
             + gate[:, 1:2] * _load_row_tiles(buf.at[slot, 1]))
        y_ref[...] = _rms(y, fin_ref[...])

    @pl.when(i == 0)
    def _():
        start_gather((dest_a_ref, dest_b_ref), 0)

    for slot in range(2):
        @pl.when(i % 2 == slot)
        def _(slot=slot):
            @pl.when(i + 1 < pl.num_programs(0))
            def _():
                start_gather((next_a_ref, next_b_ref), 1 - slot)
            finish(slot)


def _combine_call(dest, ys, xn, gate, packed, tm):
    T = xn.shape[0]
    n_tiles = T // tm
    dest3 = [d.reshape(n_tiles, 1, tm) for d in dest]
    row = lambda i: (i, 0)
    dest_spec = lambda index: pl.BlockSpec((None, 1, tm), index, memory_space=pltpu.SMEM)
    this_tile = dest_spec(lambda i: (i, 0, 0))
    next_tile = dest_spec(lambda i: (jnp.minimum(i + 1, n_tiles - 1), 0, 0))
    return pl.pallas_call(
        functools.partial(_combine_kernel, tm=tm),
        grid=(n_tiles,),
        in_specs=[this_tile, this_tile, next_tile, next_tile,
                  pl.BlockSpec(memory_space=pl.ANY),
                  pl.BlockSpec((tm, D_MODEL), row), pl.BlockSpec((tm, LANES), row),
                  pl.BlockSpec((1, D_MODEL), lambda i: (0, 0))],
        out_specs=pl.BlockSpec((tm, D_MODEL), row),
        out_shape=jax.ShapeDtypeStruct((T, D_MODEL), F32),
        scratch_shapes=[pltpu.VMEM((2, 2, tm * ROW_TILES, LANES), F32), pltpu.SemaphoreType.DMA((2,))],
        compiler_params=pltpu.CompilerParams(dimension_semantics=("arbitrary",), vmem_limit_bytes=VMEM_LIMIT),
        name="moe_combine",
    )(*dest3, *dest3, ys, xn, gate, packed["final_norm"])


def _pack_weights(attn_norm, w_in, q_norm, w_uq, kv_norm, w_ukv, w_out, ffn_norm, router_group,
                  router_group_bias, router_expert, router_expert_bias, w_gate, w_up, w_down, final_norm):
    l = 0
    wi = w_in[l]
    o_q, o_kv, o_pe = 0, Q_RANK, Q_RANK + KV_RANK
    o_dq = o_pe + MLA_ROPE
    o_dk = o_dq + D_DIFF
    o_dv = o_dk + D_DIFF
    cat = lambda parts: jnp.concatenate(parts, axis=-1)
    zeros = lambda like, n: jnp.zeros(like.shape[:-1] + (n,), F32)
    h_rope, h_diff = MLA_ROPE // 2, DIFF_ROT // 2
    gap = LANES // 2 - h_rope

    def diff_tiles(w):
        t = w.reshape(w.shape[0], DIFF_HEADS, 2, DIFF_HEAD_DIM)
        a, b = t[:, :, 0], t[:, :, 1]
        x1, x2, rest = slice(0, h_diff), slice(h_diff, DIFF_ROT), slice(DIFF_ROT, DIFF_HEAD_DIM)
        return cat([a[..., x1], b[..., x1], a[..., rest], a[..., x2], b[..., x2], b[..., rest]]).reshape(w.shape)

    w_pe = wi[:, o_pe:o_dq]
    pe_tile = cat([w_pe[:, :h_rope], zeros(w_pe, gap), w_pe[:, h_rope:], zeros(w_pe, gap)])
    w1 = cat([wi[:, o_q:o_pe], pe_tile, diff_tiles(wi[:, o_dq:o_dk]), diff_tiles(wi[:, o_dk:o_dv])]).astype(BF16)
    wdvt = wi[:, o_dv:].T.astype(BF16)

    def mla_tiles(nope, rope):
        r1 = zeros(nope, h_rope) if rope is None else rope[..., :h_rope]
        r2 = zeros(nope, h_rope) if rope is None else rope[..., h_rope:]
        return cat([r1, nope[..., :gap], r2, nope[..., gap:], zeros(nope, LANES - MLA_NOPE - MLA_ROPE)])

    wq3 = w_uq[l].reshape(Q_RANK, MLA_HEADS, MLA_NOPE + MLA_ROPE)
    wq = mla_tiles(wq3[..., :MLA_NOPE], wq3[..., MLA_NOPE:]).reshape(Q_RANK, D_MLA_PAD).astype(BF16)

    wkv3 = w_ukv[l].reshape(KV_RANK, MLA_HEADS, MLA_NOPE + MLA_V)
    padk = lambda a: jnp.pad(a, ((0, 0), (0, 0), (0, LANES - a.shape[-1]))).reshape(KV_RANK, D_MLA_PAD)
    wvt = padk(wkv3[:, :, MLA_NOPE:]).T.astype(BF16)
    place = np.zeros((LANES, MLA_HEADS, LANES), np.float32)
    for r in range(h_rope):
        for lane_ in (ROT_X1_LANES[0] + r, ROT_X1_LANES[0] + LANES // 2 + r):
            place[lane_, :, lane_] = 1.0
    place = jnp.asarray(place.reshape(LANES, D_MLA_PAD))
    wk_nope = mla_tiles(wkv3[:, :, :MLA_NOPE], None).reshape(KV_RANK, D_MLA_PAD)
    wk = jnp.concatenate([wk_nope, place, place], axis=0).astype(BF16)

    wo = w_out[l]
    n_mla = MLA_HEADS * MLA_V
    woa = jnp.pad(wo[:n_mla].reshape(MLA_HEADS, MLA_V, D_MODEL),
                  ((0, 0), (0, LANES - MLA_V), (0, 0))).reshape(D_MLA_PAD, D_MODEL).astype(BF16)
    padl = lambda a: jnp.pad(a, ((0, 0), (0, LANES - a.shape[-1])))
    return {
        "attn_norm": attn_norm[l][None], "w1": w1, "q_norm": q_norm[l][None], "wq": wq,
        "kv_norm": kv_norm[l][None], "wk": wk, "wvt": wvt, "wdvt": wdvt,
        "woa": woa, "wob": wo[n_mla:].astype(BF16), "ffn_norm": ffn_norm[l][None],
        "wrg": padl(router_group[l]).astype(BF16), "brg": padl(router_group_bias[l][None]),
        "wre": padl(router_expert[l]).astype(BF16), "bre": padl(router_expert_bias[l][None]),
        "wg": w_gate[l], "wu": w_up[l], "wd": w_down[l],
        "final_norm": final_norm[None],
    }


def _cos_sin(seq, rot):
    half = rot // 2
    inv = ROPE_THETA ** (-jnp.arange(half, dtype=F32) / half)
    ang = jnp.arange(seq, dtype=F32)[:, None] * inv[None, :]
    return jnp.cos(ang), jnp.sin(ang)


N_ROPE_TABLES = 4


def _rope_tables(seq):
    lane = np.arange(LANES)
    first = (lane >= ROT_X1_LANES[0]) & (lane < ROT_X1_LANES[1])
    second = np.roll(first, LANES // 2)

    def table(half, cos, sin, scale=1.0):
        cos_l = jnp.tile(cos, (1, LANES // half))
        sin_l = jnp.tile(sin, (1, LANES // half))
        return [jnp.where(first | second, cos_l, 1.0) * scale,
                (jnp.where(second, sin_l, 0.0) - jnp.where(first, sin_l, 0.0)) * scale]

    c, s = _cos_sin(seq, MLA_ROPE)
    tq = table(MLA_ROPE // 2, c, s, (MLA_NOPE + MLA_ROPE) ** -0.5 * LOG2E)
    tpe = table(MLA_ROPE // 2, c, s)
    c, s = _cos_sin(seq, DIFF_ROT)
    tdq = table(DIFF_ROT // 2, c, s, DIFF_HEAD_DIM ** -0.5 * LOG2E)
    tdk = table(DIFF_ROT // 2, c, s)
    return jnp.concatenate(tq + tpe + tdq + tdk, axis=1)


def _forward(x, packed, tables, lams, subln):
    B, S, D = x.shape
    T = B * S
    x2 = x.reshape(T, D)
    q, k, vt, dq, dk, dvt = _proj_call(x2, packed, tables, S, tm=512)
    r3 = lambda a: a.reshape(B, S, a.shape[-1])
    a = _mla_call(r3(q), r3(k), vt, tq=512)
    b = _diff_call(lams, subln, r3(dq), r3(dk), dvt, tq=512)
    xn, hn, gate, route, count = _mix_call(x2, a.reshape(T, -1), b.reshape(T, -1), packed, tm=512)
    dest, plan = _route_plan(route, count, tile_rows=256)
    xs = _dispatch_call(dest, hn, tm=1024)
    ys = _expert_call(plan, xs, packed, tm=256)
    y = _combine_call(dest, ys, xn, gate, packed, tm=512)
    return y.reshape(B, S, D)


def kernel(x_prompt, x_sample, attn_norm, w_in, q_norm, w_uq, kv_norm, w_ukv, diff_lq1, diff_lk1, diff_lq2,
           diff_lk2, diff_subln, w_out, ffn_norm, router_group, router_group_bias, router_expert,
           router_expert_bias, w_gate, w_up, w_down, final_norm):
    packed = _pack_weights(attn_norm, w_in, q_norm, w_uq, kv_norm, w_ukv, w_out, ffn_norm, router_group,
                           router_group_bias, router_expert, router_expert_bias, w_gate, w_up, w_down,
                           final_norm)
    lams = (diff_lq1, diff_lk1, diff_lq2, diff_lk2)
    subln = diff_subln
    tables = _rope_tables(max(x_prompt.shape[1], x_sample.shape[1]))
    return (_forward(x_prompt, packed, tables, lams, subln), _forward(x_sample, packed, tables, lams, subln))
```
